```python
import functools
import jax, jax.numpy as jnp
from jax import lax
import numpy as np

D_MODEL = 2048
BATCH = 1
SEQ = 8192
DEPTH = 1

N_META = 16
HEAD_DIM = 128
Q_BLOCK = 128
ROPE_THETA = 10000.0
EPS = 1e-6
NEG = -1e30
A_HEADS = 8
A_WIDTH = A_HEADS * HEAD_DIM
A_KV_RANK = 512
IDX_HEADS = 16
IDX_DIM = 64
TOPK_MAX = 256
B_HEADS = 8
B_WIDTH = B_HEADS * HEAD_DIM
N_GROUPS = 4
EXPERTS_PER_GROUP = 4
N_EXPERTS = N_GROUPS * EXPERTS_PER_GROUP
TOP_K_IN_GROUP = 2
D_FF_EXPERT = 512
IN_SPLITS = (A_WIDTH, A_KV_RANK, IDX_HEADS * IDX_DIM, IDX_DIM, IDX_HEADS,
             B_WIDTH, B_WIDTH, B_WIDTH, B_HEADS, D_MODEL, D_MODEL)
IN_WIDTH = (A_WIDTH + A_KV_RANK + IDX_HEADS * IDX_DIM + IDX_DIM + IDX_HEADS
            + 3 * B_WIDTH + B_HEADS + 2 * D_MODEL)

kernel_name = 'hybrid_dsa_fox_hier_moe'


def rms_norm(x, g):
    xf = x.astype(jnp.float32)
    y = xf * lax.rsqrt(jnp.mean(xf * xf, axis=-1, keepdims=True) + EPS)
    return (y * g.astype(jnp.float32)).astype(x.dtype)


def rope(x, pos):
    half = x.shape[-1] // 2
    inv_freq = ROPE_THETA ** (-jnp.arange(half, dtype=jnp.float32) / half)
    ang = pos.astype(jnp.float32)[:, None] * inv_freq[None, :]
    cos = jnp.cos(ang)[:, None, :]
    sin = jnp.sin(ang)[:, None, :]
    xf = x.astype(jnp.float32)
    x1, x2 = xf[..., :half], xf[..., half:]
    return jnp.concatenate([x1 * cos - x2 * sin, x2 * cos + x1 * sin], axis=-1).astype(x.dtype)


def _dsa_block(q, qi, wi, pos, k, v, ki, topk):
    L = k.shape[1]
    causal = jnp.arange(L)[None, :] <= pos[:, None]
    dots = jnp.einsum('bqhd,bsd->bqhs', qi, ki).astype(jnp.float32)
    score = jnp.einsum('bqh,bqhs->bqs', wi.astype(jnp.float32), jax.nn.relu(dots)) * (IDX_DIM ** -0.5)
    score = jnp.where(causal[None], score, NEG)
    _, sel = lax.top_k(score, topk)
    valid = sel <= pos[None, :, None]
    k_sel = jax.vmap(lambda kb, sb: kb[sb])(k, sel)
    v_sel = jax.vmap(lambda vb, sb: vb[sb])(v, sel)
    logits = jnp.einsum('bqhd,bqkhd->bhqk', q, k_sel).astype(jnp.float32) * (HEAD_DIM ** -0.5)
    logits = jnp.where(valid[:, None], logits, NEG)
    p = jax.nn.softmax(logits, axis=-1).astype(v.dtype)
    return jnp.einsum('bhqk,bqkhd->bqhd', p, v_sel)


def _fox_block(q, cq, pos, k, v, c):
    L = k.shape[1]
    causal = jnp.arange(L)[None, :] <= pos[:, None]
    logits = jnp.einsum('bqhd,bshd->bhqs', q, k).astype(jnp.float32) * (HEAD_DIM ** -0.5)
    decay = jnp.transpose(cq, (0, 2, 1))[:, :, :, None] - jnp.transpose(c, (0, 2, 1))[:, :, None, :]
    logits = jnp.where(causal[None, None], logits + decay, NEG)
    p = jax.nn.softmax(logits, axis=-1).astype(v.dtype)
    return jnp.einsum('bhqs,bshd->bqhd', p, v)


def _sweep(block_fn, q_args, kv_args):
    B, L = q_args[0].shape[:2]
    n_real = L - N_META
    nb = n_real // Q_BLOCK
    pos = jnp.arange(L)
    meta_out = block_fn(*[a[:, :N_META] for a in q_args], pos[:N_META], *kv_args)

    def to_blocks(a):
        a = a[:, N_META:]
        a = a.reshape((B, nb, Q_BLOCK) + a.shape[2:])
        return jnp.moveaxis(a, 1, 0)

    blocks = tuple(to_blocks(a) for a in q_args)
    pos_blocks = pos[N_META:].reshape(nb, Q_BLOCK)
    out = lax.map(lambda xs: block_fn(*xs[0], xs[1], *kv_args), (blocks, pos_blocks))
    out = jnp.moveaxis(out, 0, 1).reshape((B, n_real) + out.shape[3:])
    return jnp.concatenate([meta_out, out], axis=1)


def token_mixer(x, g_mix, w_in, g_kv, w_kv_up, g_idx_k, b_f, w_branch_a, w_branch_b, w_out):
    B, L, _ = x.shape
    pos = jnp.arange(L)
    h = rms_norm(x, g_mix)
    proj = h @ w_in
    split_idx = np.cumsum(IN_SPLITS)[:-1].tolist()
    qa, ckv, qi, ki, wi, qb, kb, vb, fb, ga, gb = jnp.split(proj, split_idx, axis=-1)

    qa = rope(qa.reshape(B, L, A_HEADS, HEAD_DIM), pos)
    kva = rms_norm(ckv, g_kv) @ w_kv_up
    ka, va = jnp.split(kva, 2, axis=-1)
    ka = rope(ka.reshape(B, L, A_HEADS, HEAD_DIM), pos)
    va = va.reshape(B, L, A_HEADS, HEAD_DIM)
    qi = rope(qi.reshape(B, L, IDX_HEADS, IDX_DIM), pos)
    ki = rope(rms_norm(ki, g_idx_k)[:, :, None, :], pos)[:, :, 0, :]
    wi = wi * (IDX_HEADS ** -0.5)
    topk = min(TOPK_MAX, L // 4)
    ya = _sweep(functools.partial(_dsa_block, topk=topk), (qa, qi, wi), (ka, va, ki))
    ya = ya.reshape(B, L, A_WIDTH)

    logf = jax.nn.log_sigmoid(fb.astype(jnp.float32) + b_f.astype(jnp.float32))
    c = jnp.cumsum(logf, axis=1)
    qb = qb.reshape(B, L, B_HEADS, HEAD_DIM)
    kb = kb.reshape(B, L, B_HEADS, HEAD_DIM)
    vb = vb.reshape(B, L, B_HEADS, HEAD_DIM)
    yb = _sweep(_fox_block, (qb, c), (kb, vb, c)).reshape(B, L, B_WIDTH)

    merged = jax.nn.sigmoid(ga) * (ya @ w_branch_a) + jax.nn.sigmoid(gb) * (yb @ w_branch_b)
    return merged @ w_out


def hier_moe(x, g_ffn, w_group, b_group, w_expert, b_expert, w_gate_e, w_up_e, w_down_e):
    B, L, D = x.shape
    t = rms_norm(x, g_ffn).reshape(B * L, D)
    T = t.shape[0]
    g_logits = (t @ w_group).astype(jnp.float32) + b_group.astype(jnp.float32)
    g_prob = jax.nn.softmax(g_logits, axis=-1)
    g_sel = jnp.argmax(g_logits, axis=-1)
    p_group = jnp.take_along_axis(g_prob, g_sel[:, None], axis=-1)
    e_logits = ((t @ w_expert).astype(jnp.float32) + b_expert.astype(jnp.float32)).reshape(T, N_GROUPS, EXPERTS_PER_GROUP)
    e_logits = jnp.take_along_axis(e_logits, g_sel[:, None, None], axis=1)[:, 0]
    e_prob = jax.nn.softmax(e_logits, axis=-1)
    top_p, top_i = lax.top_k(e_prob, TOP_K_IN_GROUP)
    top_p = top_p / jnp.sum(top_p, axis=-1, keepdims=True)
    weight = p_group * top_p
    expert_id = g_sel[:, None] * EXPERTS_PER_GROUP + top_i
    gates = jnp.sum(jax.nn.one_hot(expert_id, N_EXPERTS, dtype=jnp.float32) * weight[..., None], axis=1)
    hg = jnp.einsum('td,edf->etf', t, w_gate_e)
    hu = jnp.einsum('td,edf->etf', t, w_up_e)
    act = jax.nn.silu(hg) * hu * gates.T[:, :, None].astype(hu.dtype)
    y = jnp.einsum('etf,efd->td', act, w_down_e)
    return y.reshape(B, L, D)


def setup_inputs(seed: int = 0) -> dict:
    key = jax.random.key(seed)
    ks = jax.random.split(key, 24)
    f32 = jnp.float32

    def nrm(k, shape, scale):
        return jax.random.normal(k, shape, f32) * scale

    def gain(k, shape):
        return 1.0 + 0.05 * jax.random.normal(k, shape, f32)

    return {
        'x': nrm(ks[0], (BATCH, SEQ, D_MODEL), 1.0),
        'meta_tokens': nrm(ks[1], (N_META, D_MODEL), 1.0),
        'g_mix': gain(ks[2], (DEPTH, D_MODEL)),
        'w_in': nrm(ks[3], (DEPTH, D_MODEL, IN_WIDTH), D_MODEL ** -0.5),
        'g_kv': gain(ks[4], (DEPTH, A_KV_RANK)),
        'w_kv_up': nrm(ks[5], (DEPTH, A_KV_RANK, 2 * A_WIDTH), A_KV_RANK ** -0.5),
        'g_idx_k': gain(ks[6], (DEPTH, IDX_DIM)),
        'b_f': 3.0 + nrm(ks[7], (DEPTH, B_HEADS), 0.5),
        'w_branch_a': nrm(ks[8], (DEPTH, A_WIDTH, D_MODEL), A_WIDTH ** -0.5),
        'w_branch_b': nrm(ks[9], (DEPTH, B_WIDTH, D_MODEL), B_WIDTH ** -0.5),
        'w_out': nrm(ks[10], (DEPTH, D_MODEL, D_MODEL), D_MODEL ** -0.5),
        'g_ffn': gain(ks[11], (DEPTH, D_MODEL)),
        'w_group': nrm(ks[12], (DEPTH, D_MODEL, N_GROUPS), D_MODEL ** -0.5),
        'b_group': nrm(ks[13], (DEPTH, N_GROUPS), 0.01),
        'w_expert': nrm(ks[14], (DEPTH, D_MODEL, N_EXPERTS), D_MODEL ** -0.5),
        'b_expert': nrm(ks[15], (DEPTH, N_EXPERTS), 0.01),
        'w_gate_e': nrm(ks[16], (DEPTH, N_EXPERTS, D_MODEL, D_FF_EXPERT), D_MODEL ** -0.5),
        'w_up_e': nrm(ks[17], (DEPTH, N_EXPERTS, D_MODEL, D_FF_EXPERT), D_MODEL ** -0.5),
        'w_down_e': nrm(ks[18], (DEPTH, N_EXPERTS, D_FF_EXPERT, D_MODEL), D_FF_EXPERT ** -0.5),
        'g_final': gain(ks[19], (D_MODEL,)),
    }


def reference(x, meta_tokens, g_mix, w_in, g_kv, w_kv_up, g_idx_k, b_f, w_branch_a, w_branch_b,
              w_out, g_ffn, w_group, b_group, w_expert, b_expert, w_gate_e, w_up_e, w_down_e, g_final):
    B = x.shape[0]
    meta = jnp.broadcast_to(meta_tokens[None].astype(x.dtype), (B, N_META, x.shape[-1]))
    h = jnp.concatenate([meta, x], axis=1)
    for l in range(DEPTH):
        h = h + token_mixer(h, g_mix[l], w_in[l], g_kv[l], w_kv_up[l], g_idx_k[l], b_f[l],
                            w_branch_a[l], w_branch_b[l], w_out[l])
        h = h + hier_moe(h, g_ffn[l], w_group[l], b_group[l], w_expert[l], b_expert[l],
                         w_gate_e[l], w_up_e[l], w_down_e[l])
    h = rms_norm(h, g_final)
    return h[:, N_META:]
```

```python
import functools

import jax
import jax.numpy as jnp
from jax import lax
from jax.experimental import pallas as pl
from jax.experimental.pallas import tpu as pltpu

F32 = jnp.float32
BF16 = jnp.bfloat16

D_MODEL = 2048
N_META = 16
HEAD_DIM = 128
ROPE_THETA = 10000.0
EPS = 1e-6
NEG = -1e30
A_HEADS = 8
A_WIDTH = A_HEADS * HEAD_DIM
A_KV_RANK = 512
IDX_HEADS = 16
IDX_DIM = 64
TOPK_MAX = 256
B_HEADS = 8
B_WIDTH = B_HEADS * HEAD_DIM
N_GROUPS = 4
EXPERTS_PER_GROUP = 4
N_EXPERTS = N_GROUPS * EXPERTS_PER_GROUP
D_FF_EXPERT = 512

LANES = 128
META_PAD = LANES
VMEM_LIMIT = 56 * 1024 * 1024

SM_KI = 0
SM_WI = IDX_DIM
SM_FB = IDX_DIM + IDX_HEADS
RT_E0 = N_GROUPS


def _cparams(sem):
    return pltpu.CompilerParams(dimension_semantics=sem, vmem_limit_bytes=VMEM_LIMIT)


def _rms_kernel(x_ref, g_ref, o_ref):
    x = x_ref[...]
    ms = jnp.mean(x * x, axis=-1, keepdims=True)
    o_ref[...] = (x * lax.rsqrt(ms + EPS) * g_ref[...]).astype(o_ref.dtype)


def _rmsnorm(x, g, tm):
    n, d = x.shape
    return pl.pallas_call(
        _rms_kernel,
        grid=(n // tm,),
        in_specs=[pl.BlockSpec((tm, d), lambda i: (i, 0)), pl.BlockSpec((1, d), lambda i: (0, 0))],
        out_specs=pl.BlockSpec((tm, d), lambda i: (i, 0)),
        out_shape=jax.ShapeDtypeStruct((n, d), BF16),
        compiler_params=_cparams(("parallel",)),
        name="rmsnorm",
    )(x, g.reshape(1, d))


def _rope_tile(y, cos, sin_signed, half):
    if 2 * half == LANES:
        partner = pltpu.roll(y, half, axis=1)
    else:
        lane = lax.broadcasted_iota(jnp.int32, y.shape, 1)
        first = (lane % (2 * half)) < half
        partner = jnp.where(first, pltpu.roll(y, LANES - half, axis=1), pltpu.roll(y, half, axis=1))
    return y * cos + partner * sin_signed


def _rope_tables(pos, half):
    inv_freq = ROPE_THETA ** (-jnp.arange(half, dtype=F32) / half)
    ang = pos.astype(F32)[:, None] * inv_freq[None, :]
    cos, sin = jnp.cos(ang), jnp.sin(ang)
    reps = LANES // (2 * half)
    cos_t = jnp.tile(jnp.concatenate([cos, cos], axis=1), (1, reps))
    sin_t = jnp.tile(jnp.concatenate([-sin, sin], axis=1), (1, reps))
    return cos_t, sin_t


def _proj_plain_kernel(h_ref, w_ref, o_ref):
    o_ref[...] = jnp.dot(h_ref[...], w_ref[...], preferred_element_type=F32).astype(o_ref.dtype)


def _proj_plain(h, w, tm, tn):
    n, d = h.shape
    nc = w.shape[1]
    return pl.pallas_call(
        _proj_plain_kernel,
        grid=(n // tm, nc // tn),
        in_specs=[pl.BlockSpec((tm, d), lambda i, j: (i, 0)), pl.BlockSpec((d, tn), lambda i, j: (0, j))],
        out_specs=pl.BlockSpec((tm, tn), lambda i, j: (i, j)),
        out_shape=jax.ShapeDtypeStruct((n, nc), BF16),
        compiler_params=_cparams(("parallel", "parallel")),
        name="proj_plain",
    )(h, w)


def _proj_rope_kernel(h_ref, w_ref, cos_ref, sin_ref, o_ref, *, half):
    y = jnp.dot(h_ref[...], w_ref[...], preferred_element_type=F32)
    cos, sin = cos_ref[...], sin_ref[...]
    for c in range(y.shape[1] // LANES):
        sl = slice(c * LANES, (c + 1) * LANES)
        o_ref[:, sl] = _rope_tile(y[:, sl], cos, sin, half).astype(o_ref.dtype)


def _proj_rope(h, w, cos, sin, half, tm, tn):
    n, d = h.shape
    nc = w.shape[1]
    return pl.pallas_call(
        functools.partial(_proj_rope_kernel, half=half),
        grid=(n // tm, nc // tn),
        in_specs=[pl.BlockSpec((tm, d), lambda i, j: (i, 0)), pl.BlockSpec((d, tn), lambda i, j: (0, j)),
                  pl.BlockSpec((tm, LANES), lambda i, j: (i, 0)), pl.BlockSpec((tm, LANES), lambda i, j: (i, 0))],
        out_specs=pl.BlockSpec((tm, tn), lambda i, j: (i, j)),
        out_shape=jax.ShapeDtypeStruct((n, nc), BF16),
        compiler_params=_cparams(("parallel", "parallel")),
        name="proj_rope",
    )(h, w, cos, sin)


def _kv_kernel(h_ref, wc_ref, g_ref, wup_ref, cos_ref, sin_ref, ka_ref, va_ref):
    c = jnp.dot(h_ref[...], wc_ref[...], preferred_element_type=F32)
    ms = jnp.mean(c * c, axis=-1, keepdims=True)
    cn = (c * lax.rsqrt(ms + EPS) * g_ref[...]).astype(BF16)
    kv = jnp.dot(cn, wup_ref[...], preferred_element_type=F32)
    cos, sin = cos_ref[...], sin_ref[...]
    for hd in range(A_HEADS):
        sl = slice(hd * HEAD_DIM, (hd + 1) * HEAD_DIM)
        ka_ref[:, sl] = _rope_tile(kv[:, sl], cos, sin, HEAD_DIM // 2).astype(ka_ref.dtype)
    va_ref[...] = kv[:, A_WIDTH:].astype(va_ref.dtype)


def _proj_kv(h, wc, g_kv, wup, cos, sin, tm):
    n, d = h.shape
    return pl.pallas_call(
        _kv_kernel,
        grid=(n // tm,),
        in_specs=[pl.BlockSpec((tm, d), lambda i: (i, 0)),
                  pl.BlockSpec((d, A_KV_RANK), lambda i: (0, 0)),
                  pl.BlockSpec((1, A_KV_RANK), lambda i: (0, 0)),
                  pl.BlockSpec((A_KV_RANK, 2 * A_WIDTH), lambda i: (0, 0)),
                  pl.BlockSpec((tm, LANES), lambda i: (i, 0)),
                  pl.BlockSpec((tm, LANES), lambda i: (i, 0))],
        out_specs=[pl.BlockSpec((tm, A_WIDTH), lambda i: (i, 0)), pl.BlockSpec((tm, A_WIDTH), lambda i: (i, 0))],
        out_shape=[jax.ShapeDtypeStruct((n, A_WIDTH), BF16), jax.ShapeDtypeStruct((n, A_WIDTH), BF16)],
        compiler_params=_cparams(("parallel",)),
        name="proj_kv",
    )(h, wc, g_kv.reshape(1, A_KV_RANK), wup, cos, sin)


def _small_kernel(h_ref, w_ref, g_ref, bf_ref, c0_ref, cos_ref, sin_ref, o_ref, carry_ref):
    tm = h_ref.shape[0]

    @pl.when(pl.program_id(0) == 0)
    def _():
        carry_ref[...] = c0_ref[...]

    y = jnp.dot(h_ref[...], w_ref[...], preferred_element_type=F32)
    lane = lax.broadcasted_iota(jnp.int32, y.shape, 1)
    is_ki = lane < SM_WI
    ms = jnp.sum(jnp.where(is_ki, y * y, 0.0), axis=-1, keepdims=True) * (1.0 / IDX_DIM)
    ki = _rope_tile(y * lax.rsqrt(ms + EPS) * g_ref[...], cos_ref[...], sin_ref[...], IDX_DIM // 2)
    wi = y * (IDX_HEADS ** -0.5 * IDX_DIM ** -0.5)
    z = y + bf_ref[...]
    logf = -(jnp.maximum(-z, 0.0) + jnp.log(1.0 + jnp.exp(-jnp.abs(z))))
    hi = logf.astype(BF16)
    r1 = logf - hi.astype(F32)
    mid = r1.astype(BF16)
    lo = (r1 - mid.astype(F32)).astype(BF16)
    row = lax.broadcasted_iota(jnp.int32, (tm, tm), 0)
    col = lax.broadcasted_iota(jnp.int32, (tm, tm), 1)
    tri = (col <= row).astype(BF16)
    csum = (jnp.dot(tri, hi, preferred_element_type=F32) + jnp.dot(tri, mid, preferred_element_type=F32)
            + jnp.dot(tri, lo, preferred_element_type=F32)) + carry_ref[0:1, :]
    carry_ref[...] = jnp.broadcast_to(csum[tm - 1:tm, :], carry_ref.shape)
    o_ref[...] = jnp.where(is_ki, ki, jnp.where(lane < SM_FB, wi, csum))


def _proj_small(h, w, g_idx_pad, bf_pad, c0, cos, sin, tm):
    n, d = h.shape
    return pl.pallas_call(
        _small_kernel,
        grid=(n // tm,),
        in_specs=[pl.BlockSpec((tm, d), lambda i: (i, 0)),
                  pl.BlockSpec((d, LANES), lambda i: (0, 0)),
                  pl.BlockSpec((1, LANES), lambda i: (0, 0)),
                  pl.BlockSpec((1, LANES), lambda i: (0, 0)),
                  pl.BlockSpec((8, LANES), lambda i: (0, 0)),
                  pl.BlockSpec((tm, LANES), lambda i: (i, 0)),
                  pl.BlockSpec((tm, LANES), lambda i: (i, 0))],
        out_specs=pl.BlockSpec((tm, LANES), lambda i: (i, 0)),
        out_shape=jax.ShapeDtypeStruct((n, LANES), F32),
        scratch_shapes=[pltpu.VMEM((8, LANES), F32)],
        compiler_params=_cparams(("arbitrary",)),
        name="proj_small",
    )(h, w, g_idx_pad, bf_pad, c0, cos, sin)


def _key_to_float(key):
    bits = key ^ ((key >> 31) & jnp.int32(0x7FFFFFFF))
    return lax.bitcast_convert_type(bits, F32)


def _select_kernel(qi_ref, kit_ref, wi_ref, o_ref, s_ref, *, tq, tc, topk):
    i = pl.program_id(0)
    n_real = o_ref.shape[1] - META_PAD
    n_chunks = n_real // tc
    diag = (i * tq) // tc
    row = lax.broadcasted_iota(jnp.int32, (tq, tc), 0) + i * tq
    col = lax.broadcasted_iota(jnp.int32, (tq, tc), 1)
    wi = wi_ref[...]

    def scores(kc):
        s = jnp.zeros((tq, kc.shape[1]), F32)
        for h in range(IDX_HEADS):
            d = jnp.dot(qi_ref[h], kc, preferred_element_type=F32)
            s = s + wi[:, h:h + 1] * jnp.maximum(d, 0.0)
        return s

    def score_chunk(c, _):
        off = pl.multiple_of(c * tc, tc)
        s = scores(kit_ref[:, pl.ds(off, tc)])
        s_ref[:, pl.ds(off, tc)] = jnp.where(col + off <= row, s, NEG)
        return 0

    lax.fori_loop(0, diag + 1, score_chunk, 0)
    mcol = lax.broadcasted_iota(jnp.int32, (tq, META_PAD), 1)
    s_ref[:, n_real:] = jnp.where(mcol < N_META, scores(kit_ref[:, n_real:]), NEG)

    def count_ge(thr):
        thr_c = jnp.broadcast_to(thr, (tq, tc))

        def body(c, acc):
            off = pl.multiple_of(c * tc, tc)
            return acc + jnp.where(s_ref[:, pl.ds(off, tc)] >= thr_c, 1.0, 0.0)

        acc = lax.fori_loop(0, diag + 1, body, jnp.zeros((tq, tc), F32))
        macc = jnp.where(s_ref[:, n_real:] >= jnp.broadcast_to(thr, (tq, META_PAD)), 1.0, 0.0)
        return jnp.sum(acc, axis=1, keepdims=True) + jnp.sum(macc, axis=1, keepdims=True)

    kf = jnp.float32(topk)

    def bit_step(b, carry):
        t_key, cnt_t = carry
        cand = t_key + lax.shift_left(jnp.int32(1), 31 - b)
        cnt = count_ge(_key_to_float(cand))
        take = cnt >= kf
        return jnp.where(take, cand, t_key), jnp.where(take, cnt, cnt_t)

    t0 = jnp.full((tq, 1), jnp.iinfo(jnp.int32).min, jnp.int32)
    c0 = jnp.broadcast_to(((diag + 1) * tc + META_PAD).astype(F32), (tq, 1))
    t_key, cnt_t = lax.fori_loop(0, 32, bit_step, (t0, c0))
    thr = _key_to_float(t_key)

    def fill_chunk(c, _):
        off = pl.multiple_of(c * tc, tc)
        o_ref[:, pl.ds(off, tc)] = jnp.full((tq, tc), NEG, o_ref.dtype)
        return 0

    lax.fori_loop(diag + 1, n_chunks, fill_chunk, 0)

    ambiguous = jnp.logical_and(cnt_t > kf, thr > NEG)
    any_amb = jnp.max(jnp.where(ambiguous, 1.0, 0.0)) > 0.0

    @pl.when(jnp.logical_not(any_amb))
    def _():
        def out_chunk(c, _):
            off = pl.multiple_of(c * tc, tc)
            s = s_ref[:, pl.ds(off, tc)]
            sel = jnp.where(s >= jnp.broadcast_to(thr, (tq, tc)), 0.0, NEG)
            o_ref[:, pl.ds(off, tc)] = jnp.where(col + off <= row, sel, NEG).astype(o_ref.dtype)
            return 0

        lax.fori_loop(0, diag + 1, out_chunk, 0)
        sm = s_ref[:, n_real:]
        sel = jnp.where(sm >= jnp.broadcast_to(thr, (tq, META_PAD)), 0.0, NEG)
        o_ref[:, n_real:] = jnp.where(mcol < N_META, sel, NEG).astype(o_ref.dtype)

    @pl.when(any_amb)
    def _():
        def gt_body(c, acc):
            off = pl.multiple_of(c * tc, tc)
            return acc + jnp.where(s_ref[:, pl.ds(off, tc)] > jnp.broadcast_to(thr, (tq, tc)), 1.0, 0.0)

        acc = lax.fori_loop(0, diag + 1, gt_body, jnp.zeros((tq, tc), F32))
        sm = s_ref[:, n_real:]
        thr_m = jnp.broadcast_to(thr, (tq, META_PAD))
        cnt_gt = (jnp.sum(acc, axis=1, keepdims=True)
                  + jnp.sum(jnp.where(sm > thr_m, 1.0, 0.0), axis=1, keepdims=True))
        quota = jnp.where(ambiguous, kf - cnt_gt, jnp.float32(3e38))

        def tie_select(s, valid, running, width):
            r = lax.broadcasted_iota(jnp.int32, (width, width), 0)
            c_ = lax.broadcasted_iota(jnp.int32, (width, width), 1)
            upper = (r <= c_).astype(BF16)
            thr_b = jnp.broadcast_to(thr, s.shape)
            eq = jnp.where(s == thr_b, 1.0, 0.0)
            rank = jnp.dot(eq.astype(BF16), upper, preferred_element_type=F32) + running
            keep = jnp.where(s > thr_b, 0.0, jnp.where(jnp.logical_and(eq > 0.0, rank <= quota), 0.0, NEG))
            return jnp.where(valid, keep, NEG), running + jnp.sum(eq, axis=1, keepdims=True)

        bias_m, running = tie_select(sm, mcol < N_META, jnp.zeros((tq, 1), F32), META_PAD)
        o_ref[:, n_real:] = bias_m.astype(o_ref.dtype)

        def out_chunk(c, running):
            off = pl.multiple_of(c * tc, tc)
            bias, running = tie_select(s_ref[:, pl.ds(off, tc)], col + off <= row, running, tc)
            o_ref[:, pl.ds(off, tc)] = bias.astype(o_ref.dtype)
            return running

        lax.fori_loop(0, diag + 1, out_chunk, running)


def _select_bias(qi_hm, kit, wi, topk, tq, tc):
    _, t, _ = qi_hm.shape
    nk = kit.shape[1]
    return pl.pallas_call(
        functools.partial(_select_kernel, tq=tq, tc=tc, topk=topk),
        grid=(t // tq,),
        in_specs=[pl.BlockSpec((IDX_HEADS, tq, IDX_DIM), lambda i: (0, i, 0)),
                  pl.BlockSpec((IDX_DIM, nk), lambda i: (0, 0)),
                  pl.BlockSpec((tq, IDX_HEADS), lambda i: (i, 0))],
        out_specs=pl.BlockSpec((tq, nk), lambda i: (i, 0)),
        out_shape=jax.ShapeDtypeStruct((t, nk), BF16),
        scratch_shapes=[pltpu.VMEM((tq, nk), F32)],
        compiler_params=_cparams(("parallel",)),
        name="select_bias",
    )(qi_hm, kit, wi)


def _attn_kernel(*refs, tq, tk, fox):
    if fox:
        (q_ref, kt_ref, v_ref, ktm_ref, vm_ref, cq_ref, ck_ref, ckm_ref, o_ref, m_ref, l_ref, acc_ref) = refs
    else:
        (q_ref, kt_ref, v_ref, ktm_ref, vm_ref, b_ref, bm_ref, o_ref, m_ref, l_ref, acc_ref) = refs
    i = pl.program_id(0)
    j = pl.program_id(1)
    nheads = q_ref.shape[1] // HEAD_DIM
    scale = HEAD_DIM ** -0.5
    jmax = ((i + 1) * tq - 1) // tk

    def block(kt_blk_ref, v_blk_ref, bias_fn):
        for h in range(nheads):
            sl = slice(h * HEAD_DIM, (h + 1) * HEAD_DIM)
            s = jnp.dot(q_ref[:, sl], kt_blk_ref[sl, :], preferred_element_type=F32) * scale
            s = bias_fn(s, h)
            m_prev = m_ref[h]
            l_prev = l_ref[h]
            m_new = jnp.maximum(m_prev, jnp.max(s, axis=1, keepdims=True))
            alpha = jnp.exp(m_prev - m_new)
            p = jnp.exp(s - m_new)
            l_ref[h] = alpha * l_prev + jnp.sum(p, axis=1, keepdims=True)
            m_ref[h] = m_new
            acc_ref[h] = alpha * acc_ref[h] + jnp.dot(p.astype(BF16), v_blk_ref[:, sl],
                                                      preferred_element_type=F32)

    @pl.when(j == 0)
    def _():
        m_ref[...] = jnp.full(m_ref.shape, NEG, F32)
        l_ref[...] = jnp.zeros(l_ref.shape, F32)
        acc_ref[...] = jnp.zeros(acc_ref.shape, F32)
        mcol = lax.broadcasted_iota(jnp.int32, (tq, META_PAD), 1)
        if fox:
            def bias_fn(s, h):
                dec = cq_ref[:, h:h + 1] - ckm_ref[h:h + 1, :]
                return jnp.where(mcol < N_META, s + dec, NEG)
        else:
            def bias_fn(s, h):
                return s + bm_ref[...].astype(F32)
        block(ktm_ref, vm_ref, bias_fn)

    @pl.when(j <= jmax)
    def _():
        if fox:
            row = lax.broadcasted_iota(jnp.int32, (tq, tk), 0) + i * tq
            col = lax.broadcasted_iota(jnp.int32, (tq, tk), 1) + j * tk
            causal = col <= row

            def bias_fn(s, h):
                dec = cq_ref[:, h:h + 1] - ck_ref[h:h + 1, :]
                return jnp.where(causal, s + dec, NEG)
        else:
            def bias_fn(s, h):
                return s + b_ref[...].astype(F32)
        block(kt_ref, v_ref, bias_fn)

    @pl.when(j == pl.num_programs(1) - 1)
    def _():
        for h in range(nheads):
            sl = slice(h * HEAD_DIM, (h + 1) * HEAD_DIM)
            o_ref[:, sl] = (acc_ref[h] / l_ref[h]).astype(o_ref.dtype)


def _attention(q, kt, v, extra, *, fox, tq, tk):
    t, w = q.shape
    n_real = t
    nheads = w // HEAD_DIM
    mblk = n_real // META_PAD

    def jc(i, j):
        return jnp.minimum(j, ((i + 1) * tq - 1) // tk)

    in_specs = [pl.BlockSpec((tq, w), lambda i, j: (i, 0)),
                pl.BlockSpec((w, tk), lambda i, j: (0, jc(i, j))),
                pl.BlockSpec((tk, w), lambda i, j: (jc(i, j), 0)),
                pl.BlockSpec((w, META_PAD), lambda i, j: (0, mblk)),
                pl.BlockSpec((META_PAD, w), lambda i, j: (mblk, 0))]
    if fox:
        cq, ct = extra
        in_specs += [pl.BlockSpec((tq, nheads), lambda i, j: (i, 0)),
                     pl.BlockSpec((nheads, tk), lambda i, j: (0, jc(i, j))),
                     pl.BlockSpec((nheads, META_PAD), lambda i, j: (0, mblk))]
        args = (q, kt, v, kt, v, cq, ct, ct)
    else:
        (bias,) = extra
        in_specs += [pl.BlockSpec((tq, tk), lambda i, j: (i, jc(i, j))),
                     pl.BlockSpec((tq, META_PAD), lambda i, j: (i, mblk))]
        args = (q, kt, v, kt, v, bias, bias)
    return pl.pallas_call(
        functools.partial(_attn_kernel, tq=tq, tk=tk, fox=fox),
        grid=(t // tq, n_real // tk),
        in_specs=in_specs,
        out_specs=pl.BlockSpec((tq, w), lambda i, j: (i, 0)),
        out_shape=jax.ShapeDtypeStruct((t, w), BF16),
        scratch_shapes=[pltpu.VMEM((nheads, tq, 1), F32), pltpu.VMEM((nheads, tq, 1), F32),
                        pltpu.VMEM((nheads, tq, HEAD_DIM), F32)],
        compiler_params=_cparams(("parallel", "arbitrary")),
        name="attn_fox" if fox else "attn_dsa",
    )(*args)


def _route(logits):
    lane = lax.broadcasted_iota(jnp.int32, logits.shape, 1).astype(F32)
    big = jnp.float32(LANES)
    ninf = -jnp.inf
    gl = jnp.where(lane < N_GROUPS, logits, ninf)
    gmax = jnp.max(gl, axis=1, keepdims=True)
    g_sel = jnp.min(jnp.where(gl == gmax, lane, big), axis=1, keepdims=True)
    p_group = 1.0 / jnp.sum(jnp.exp(gl - gmax), axis=1, keepdims=True)
    e_lo = RT_E0 + g_sel * EXPERTS_PER_GROUP
    in_grp = jnp.logical_and(lane >= e_lo, lane < e_lo + EXPERTS_PER_GROUP)
    el = jnp.where(in_grp, logits, ninf)
    emax = jnp.max(el, axis=1, keepdims=True)
    ex = jnp.exp(el - emax)
    prob = ex / jnp.sum(ex, axis=1, keepdims=True)
    prob = jnp.where(in_grp, prob, -1.0)
    p1 = jnp.max(prob, axis=1, keepdims=True)
    i1 = jnp.min(jnp.where(prob == p1, lane, big), axis=1, keepdims=True)
    rest = jnp.where(lane == i1, -1.0, prob)
    p2 = jnp.max(rest, axis=1, keepdims=True)
    i2 = jnp.min(jnp.where(rest == p2, lane, big), axis=1, keepdims=True)
    tot = p1 + p2
    w1 = p_group * (p1 / tot)
    w2 = p_group * (p2 / tot)
    return jnp.where(lane == i1, w1, jnp.where(lane == i2, w2, 0.0))


def _merge_kernel(h_ref, ya_ref, yb_ref, wga_ref, wgb_ref, wa_ref, wb_ref, wo_ref, x_ref, g_ref,
                  wrh_ref, wrl_ref, br_ref, h1_ref, t_ref, gate_ref, acc_ref):
    n = pl.program_id(1)

    @pl.when(n == 0)
    def _():
        acc_ref[...] = jnp.zeros(acc_ref.shape, F32)

    h = h_ref[...]
    ga = jnp.dot(h, wga_ref[...], preferred_element_type=F32)
    gb = jnp.dot(h, wgb_ref[...], preferred_element_type=F32)
    pa = jnp.dot(ya_ref[...], wa_ref[...], preferred_element_type=F32)
    pb = jnp.dot(yb_ref[...], wb_ref[...], preferred_element_type=F32)
    merged = jax.nn.sigmoid(ga) * pa + jax.nn.sigmoid(gb) * pb
    acc_ref[...] += jnp.dot(merged.astype(BF16), wo_ref[...], preferred_element_type=F32)

    @pl.when(n == pl.num_programs(1) - 1)
    def _():
        h1 = x_ref[...] + acc_ref[...]
        h1_ref[...] = h1
        ms = jnp.mean(h1 * h1, axis=-1, keepdims=True)
        t = h1 * lax.rsqrt(ms + EPS) * g_ref[...]
        t_hi = t.astype(BF16)
        t_ref[...] = t_hi
        t_lo = (t - t_hi.astype(F32)).astype(BF16)
        logits = (jnp.dot(t_hi, wrh_ref[...], preferred_element_type=F32)
                  + jnp.dot(t_hi, wrl_ref[...], preferred_element_type=F32)
                  + jnp.dot(t_lo, wrh_ref[...], preferred_element_type=F32)) + br_ref[...]
        gate_ref[...] = _route(logits)


def _merge(h, ya, yb, wga, wgb, wa, wb, wo, x, g_ffn, wr_hi, wr_lo, br, tm, tn):
    t, d = h.shape
    row = lambda i, n: (i, 0)
    return pl.pallas_call(
        _merge_kernel,
        grid=(t // tm, d // tn),
        in_specs=[pl.BlockSpec((tm, d), row),
                  pl.BlockSpec((tm, A_WIDTH), row),
                  pl.BlockSpec((tm, B_WIDTH), row),
                  pl.BlockSpec((d, tn), lambda i, n: (0, n)),
                  pl.BlockSpec((d, tn), lambda i, n: (0, n)),
                  pl.BlockSpec((A_WIDTH, tn), lambda i, n: (0, n)),
                  pl.BlockSpec((B_WIDTH, tn), lambda i, n: (0, n)),
                  pl.BlockSpec((tn, d), lambda i, n: (n, 0)),
                  pl.BlockSpec((tm, d), row),
                  pl.BlockSpec((1, d), lambda i, n: (0, 0)),
                  pl.BlockSpec((d, LANES), lambda i, n: (0, 0)),
                  pl.BlockSpec((d, LANES), lambda i, n: (0, 0)),
                  pl.BlockSpec((1, LANES), lambda i, n: (0, 0))],
        out_specs=[pl.BlockSpec((tm, d), row), pl.BlockSpec((tm, d), row), pl.BlockSpec((tm, LANES), row)],
        out_shape=[jax.ShapeDtypeStruct((t, d), F32), jax.ShapeDtypeStruct((t, d), BF16),
                   jax.ShapeDtypeStruct((t, LANES), F32)],
        scratch_shapes=[pltpu.VMEM((tm, d), F32)],
        compiler_params=_cparams(("parallel", "arbitrary")),
        name="merge_out_router",
    )(h, ya, yb, wga, wgb, wa, wb, wo, x, g_ffn.reshape(1, d), wr_hi, wr_lo, br)


def _moe_kernel(t_ref, gate_ref, wg_ref, wu_ref, wd_ref, h1_ref, gf_ref, o_ref, acc_ref):
    e = pl.program_id(1)

    @pl.when(e == 0)
    def _():
        acc_ref[...] = jnp.zeros(acc_ref.shape, F32)

    t = t_ref[...]
    gates = gate_ref[...]
    lane = lax.broadcasted_iota(jnp.int32, gates.shape, 1)
    g_e = jnp.sum(jnp.where(lane == RT_E0 + e, gates, 0.0), axis=1, keepdims=True)
    hg = jnp.dot(t, wg_ref[0], preferred_element_type=F32)
    hu = jnp.dot(t, wu_ref[0], preferred_element_type=F32)
    act = jax.nn.silu(hg) * hu * g_e
    acc_ref[...] += jnp.dot(act.astype(BF16), wd_ref[0], preferred_element_type=F32)

    @pl.when(e == pl.num_programs(1) - 1)
    def _():
        h2 = h1_ref[...] + acc_ref[...]
        ms = jnp.mean(h2 * h2, axis=-1, keepdims=True)
        o_ref[...] = h2 * lax.rsqrt(ms + EPS) * gf_ref[...]


def _moe(t_bf, gates, wg, wu, wd, h1, g_final, tm):
    t, d = t_bf.shape
    f = wg.shape[2]
    row = lambda i, e: (i, 0)
    return pl.pallas_call(
        _moe_kernel,
        grid=(t // tm, N_EXPERTS),
        in_specs=[pl.BlockSpec((tm, d), row),
                  pl.BlockSpec((tm, LANES), row),
                  pl.BlockSpec((1, d, f), lambda i, e: (e, 0, 0)),
                  pl.BlockSpec((1, d, f), lambda i, e: (e, 0, 0)),
                  pl.BlockSpec((1, f, d), lambda i, e: (e, 0, 0)),
                  pl.BlockSpec((tm, d), row),
                  pl.BlockSpec((1, d), lambda i, e: (0, 0))],
        out_specs=pl.BlockSpec((tm, d), row),
        out_shape=jax.ShapeDtypeStruct((t, d), F32),
        scratch_shapes=[pltpu.VMEM((tm, d), F32)],
        compiler_params=_cparams(("parallel", "arbitrary")),
        name="moe_final",
    )(t_bf, gates, wg, wu, wd, h1, g_final.reshape(1, d))


def _pick(n, pref):
    return pref if n % pref == 0 else n


def _layer(x, meta, g_mix, w_in, g_kv, w_kv_up, g_idx_k, b_f, w_branch_a, w_branch_b, w_out,
           g_ffn, w_group, b_group, w_expert, b_expert, w_gate_e, w_up_e, w_down_e, g_final):
    t, d = x.shape
    n_all = t + N_META
    topk = min(TOPK_MAX, n_all // 4)

    o = [0]
    for wdt in (A_WIDTH, A_KV_RANK, IDX_HEADS * IDX_DIM, IDX_DIM, IDX_HEADS, B_WIDTH, B_WIDTH, B_WIDTH,
                B_HEADS, D_MODEL, D_MODEL):
        o.append(o[-1] + wdt)
    wb = w_in.astype(BF16)
    w_qa, w_ckv, w_qi = wb[:, o[0]:o[1]], wb[:, o[1]:o[2]], wb[:, o[2]:o[3]]
    w_small = jnp.concatenate(
        [wb[:, o[3]:o[5]], wb[:, o[8]:o[9]], jnp.zeros((d, LANES - (o[5] - o[3]) - B_HEADS), BF16)], axis=1)
    w_b = wb[:, o[5]:o[8]]
    w_ga, w_gb = wb[:, o[9]:o[10]], wb[:, o[10]:o[11]]
    w_up_bf = w_kv_up.astype(BF16)
    g_idx_pad = jnp.zeros((1, LANES), F32).at[0, SM_KI:SM_KI + IDX_DIM].set(g_idx_k)
    bf_pad = jnp.zeros((1, LANES), F32).at[0, SM_FB:SM_FB + B_HEADS].set(b_f)

    pos_meta = jnp.arange(N_META)
    pos_real = jnp.arange(N_META, n_all)
    cos64_m, sin64_m = _rope_tables(pos_meta, HEAD_DIM // 2)
    cos64_r, sin64_r = _rope_tables(pos_real, HEAD_DIM // 2)
    cos32_m, sin32_m = _rope_tables(pos_meta, IDX_DIM // 2)
    cos32_r, sin32_r = _rope_tables(pos_real, IDX_DIM // 2)

    tm = _pick(t, 512)

    hm = _rmsnorm(meta, g_mix, N_META)
    ka_m, va_m = _proj_kv(hm, w_ckv, g_kv, w_up_bf, cos64_m, sin64_m, N_META)
    small_m = _proj_small(hm, w_small, g_idx_pad, bf_pad, jnp.zeros((8, LANES), F32), cos32_m, sin32_m, N_META)
    b_m = _proj_plain(hm, w_b, N_META, 1024)
    c_carry = jnp.broadcast_to(small_m[N_META - 1:N_META, :], (8, LANES))

    h = _rmsnorm(x, g_mix, tm)
    qa = _proj_rope(h, w_qa, cos64_r, sin64_r, HEAD_DIM // 2, tm, 1024)
    ka_r, va_r = _proj_kv(h, w_ckv, g_kv, w_up_bf, cos64_r, sin64_r, tm)
    qi = _proj_rope(h, w_qi, cos32_r, sin32_r, IDX_DIM // 2, tm, 1024)
    small_r = _proj_small(h, w_small, g_idx_pad, bf_pad, c_carry, cos32_r, sin32_r, tm)
    b_r = _proj_plain(h, w_b, tm, 1024)

    def with_meta(real, meta_rows):
        pad = jnp.zeros((META_PAD - N_META, real.shape[1]), real.dtype)
        return jnp.concatenate([real, meta_rows.astype(real.dtype), pad], axis=0)

    ka = with_meta(ka_r, ka_m)
    va = with_meta(va_r, va_m)
    kb = with_meta(b_r[:, B_WIDTH:2 * B_WIDTH], b_m[:, B_WIDTH:2 * B_WIDTH])
    vb = with_meta(b_r[:, 2 * B_WIDTH:], b_m[:, 2 * B_WIDTH:])
    qb = b_r[:, :B_WIDTH]
    ki = with_meta(small_r[:, SM_KI:SM_KI + IDX_DIM], small_m[:, SM_KI:SM_KI + IDX_DIM]).astype(BF16)
    c_all = with_meta(small_r[:, SM_FB:SM_FB + B_HEADS], small_m[:, SM_FB:SM_FB + B_HEADS])
    wi = small_r[:, SM_WI:SM_WI + IDX_HEADS]
    cq = small_r[:, SM_FB:SM_FB + B_HEADS]

    qi_hm = qi.reshape(t, IDX_HEADS, IDX_DIM).transpose(1, 0, 2)
    bias = _select_bias(qi_hm, ki.T, wi, topk, _pick(t, 256), 256)
    ya = _attention(qa, ka.T, va, (bias,), fox=False, tq=_pick(t, 256), tk=_pick(t, 512))
    yb = _attention(qb, kb.T, vb, (cq, c_all.T), fox=True, tq=_pick(t, 256), tk=_pick(t, 512))

    w_r = jnp.concatenate([w_group, w_expert, jnp.zeros((d, LANES - N_GROUPS - N_EXPERTS), F32)], axis=1)
    wr_hi = w_r.astype(BF16)
    wr_lo = (w_r - wr_hi.astype(F32)).astype(BF16)
    b_r_pad = jnp.concatenate([b_group, b_expert, jnp.zeros((LANES - N_GROUPS - N_EXPERTS,), F32)]).reshape(1, LANES)
    h1, t_bf, gates = _merge(h, ya, yb, w_ga, w_gb, w_branch_a.astype(BF16), w_branch_b.astype(BF16),
                             w_out.astype(BF16), x, g_ffn, wr_hi, wr_lo, b_r_pad, tm, 256)

    return _moe(t_bf, gates, w_gate_e.astype(BF16), w_up_e.astype(BF16), w_down_e.astype(BF16), h1, g_final, tm)


def kernel(x, meta_tokens, g_mix, w_in, g_kv, w_kv_up, g_idx_k, b_f, w_branch_a, w_branch_b, w_out, g_ffn,
           w_group, b_group, w_expert, b_expert, w_gate_e, w_up_e, w_down_e, g_final):
    assert x.shape[0] == 1 and g_mix.shape[0] == 1, "single batch, single layer"
    out = _layer(x[0], meta_tokens.astype(x.dtype), g_mix[0], w_in[0], g_kv[0], w_kv_up[0], g_idx_k[0], b_f[0],
                 w_branch_a[0], w_branch_b[0], w_out[0], g_ffn[0], w_group[0], b_group[0], w_expert[0],
                 b_expert[0], w_gate_e[0], w_up_e[0], w_down_e[0], g_final)
    return out[None]
```

```python
import functools

import jax
import jax.numpy as jnp
from jax import lax
from jax.experimental import pallas as pl
from jax.experimental.pallas import tpu as pltpu

F32 = jnp.float32
BF16 = jnp.bfloat16

D_MODEL = 2048
N_META = 16
HEAD_DIM = 128
ROPE_THETA = 10000.0
EPS = 1e-6
NEG = -1e30
A_HEADS = 8
A_WIDTH = A_HEADS * HEAD_DIM
A_KV_RANK = 512
IDX_HEADS = 16
IDX_DIM = 64
TOPK_MAX = 256
B_HEADS = 8
B_WIDTH = B_HEADS * HEAD_DIM
N_GROUPS = 4
EXPERTS_PER_GROUP = 4
N_EXPERTS = N_GROUPS * EXPERTS_PER_GROUP
D_FF_EXPERT = 512

LANES = 128
META_PAD = LANES
VMEM_LIMIT = 56 * 1024 * 1024

SM_KI = 0
SM_WI = IDX_DIM
SM_FB = IDX_DIM + IDX_HEADS
RT_E0 = N_GROUPS


def _cparams(sem):
    return pltpu.CompilerParams(dimension_semantics=sem, vmem_limit_bytes=VMEM_LIMIT)


def _rms_kernel(x_ref, g_ref, o_ref):
    x = x_ref[...]
    ms = jnp.mean(x * x, axis=-1, keepdims=True)
    o_ref[...] = (x * lax.rsqrt(ms + EPS) * g_ref[...]).astype(o_ref.dtype)


def _rmsnorm(x, g, tm):
    n, d = x.shape
    return pl.pallas_call(
        _rms_kernel,
        grid=(n // tm,),
        in_specs=[pl.BlockSpec((tm, d), lambda i: (i, 0)), pl.BlockSpec((1, d), lambda i: (0, 0))],
        out_specs=pl.BlockSpec((tm, d), lambda i: (i, 0)),
        out_shape=jax.ShapeDtypeStruct((n, d), BF16),
        compiler_params=_cparams(("parallel",)),
        name="rmsnorm",
    )(x, g.reshape(1, d))


def _rope_tile(y, cos, sin_signed, half):
    if 2 * half == LANES:
        partner = pltpu.roll(y, half, axis=1)
    else:
        lane = lax.broadcasted_iota(jnp.int32, y.shape, 1)
        first = (lane % (2 * half)) < half
        partner = jnp.where(first, pltpu.roll(y, LANES - half, axis=1), pltpu.roll(y, half, axis=1))
    return y * cos + partner * sin_signed


def _rope_tables(pos, half):
    inv_freq = ROPE_THETA ** (-jnp.arange(half, dtype=F32) / half)
    ang = pos.astype(F32)[:, None] * inv_freq[None, :]
    cos, sin = jnp.cos(ang), jnp.sin(ang)
    reps = LANES // (2 * half)
    cos_t = jnp.tile(jnp.concatenate([cos, cos], axis=1), (1, reps))
    sin_t = jnp.tile(jnp.concatenate([-sin, sin], axis=1), (1, reps))
    return cos_t, sin_t


def _proj_plain_kernel(h_ref, w_ref, o_ref):
    o_ref[...] = jnp.dot(h_ref[...], w_ref[...], preferred_element_type=F32).astype(o_ref.dtype)


def _proj_plain(h, w, tm, tn):
    n, d = h.shape
    nc = w.shape[1]
    return pl.pallas_call(
        _proj_plain_kernel,
        grid=(n // tm, nc // tn),
        in_specs=[pl.BlockSpec((tm, d), lambda i, j: (i, 0)), pl.BlockSpec((d, tn), lambda i, j: (0, j))],
        out_specs=pl.BlockSpec((tm, tn), lambda i, j: (i, j)),
        out_shape=jax.ShapeDtypeStruct((n, nc), BF16),
        compiler_params=_cparams(("parallel", "parallel")),
        name="proj_plain",
    )(h, w)


def _proj_rope_kernel(h_ref, w_ref, cos_ref, sin_ref, o_ref, *, half):
    y = jnp.dot(h_ref[...], w_ref[...], preferred_element_type=F32)
    cos, sin = cos_ref[...], sin_ref[...]
    for c in range(y.shape[1] // LANES):
        sl = slice(c * LANES, (c + 1) * LANES)
        o_ref[:, sl] = _rope_tile(y[:, sl], cos, sin, half).astype(o_ref.dtype)


def _proj_rope(h, w, cos, sin, half, tm, tn):
    n, d = h.shape
    nc = w.shape[1]
    return pl.pallas_call(
        functools.partial(_proj_rope_kernel, half=half),
        grid=(n // tm, nc // tn),
        in_specs=[pl.BlockSpec((tm, d), lambda i, j: (i, 0)), pl.BlockSpec((d, tn), lambda i, j: (0, j)),
                  pl.BlockSpec((tm, LANES), lambda i, j: (i, 0)), pl.BlockSpec((tm, LANES), lambda i, j: (i, 0))],
        out_specs=pl.BlockSpec((tm, tn), lambda i, j: (i, j)),
        out_shape=jax.ShapeDtypeStruct((n, nc), BF16),
        compiler_params=_cparams(("parallel", "parallel")),
        name="proj_rope",
    )(h, w, cos, sin)


def _kv_kernel(h_ref, wc_ref, g_ref, wup_ref, cos_ref, sin_ref, ka_ref, va_ref):
    c = jnp.dot(h_ref[...], wc_ref[...], preferred_element_type=F32)
    ms = jnp.mean(c * c, axis=-1, keepdims=True)
    cn = (c * lax.rsqrt(ms + EPS) * g_ref[...]).astype(BF16)
    kv = jnp.dot(cn, wup_ref[...], preferred_element_type=F32)
    cos, sin = cos_ref[...], sin_ref[...]
    for hd in range(A_HEADS):
        sl = slice(hd * HEAD_DIM, (hd + 1) * HEAD_DIM)
        ka_ref[:, sl] = _rope_tile(kv[:, sl], cos, sin, HEAD_DIM // 2).astype(ka_ref.dtype)
    va_ref[...] = kv[:, A_WIDTH:].astype(va_ref.dtype)


def _proj_kv(h, wc, g_kv, wup, cos, sin, tm):
    n, d = h.shape
    return pl.pallas_call(
        _kv_kernel,
        grid=(n // tm,),
        in_specs=[pl.BlockSpec((tm, d), lambda i: (i, 0)),
                  pl.BlockSpec((d, A_KV_RANK), lambda i: (0, 0)),
                  pl.BlockSpec((1, A_KV_RANK), lambda i: (0, 0)),
                  pl.BlockSpec((A_KV_RANK, 2 * A_WIDTH), lambda i: (0, 0)),
                  pl.BlockSpec((tm, LANES), lambda i: (i, 0)),
                  pl.BlockSpec((tm, LANES), lambda i: (i, 0))],
        out_specs=[pl.BlockSpec((tm, A_WIDTH), lambda i: (i, 0)), pl.BlockSpec((tm, A_WIDTH), lambda i: (i, 0))],
        out_shape=[jax.ShapeDtypeStruct((n, A_WIDTH), BF16), jax.ShapeDtypeStruct((n, A_WIDTH), BF16)],
        compiler_params=_cparams(("parallel",)),
        name="proj_kv",
    )(h, wc, g_kv.reshape(1, A_KV_RANK), wup, cos, sin)


def _small_kernel(h_ref, w_ref, g_ref, bf_ref, c0_ref, cos_ref, sin_ref, o_ref, carry_ref):
    tm = h_ref.shape[0]

    @pl.when(pl.program_id(0) == 0)
    def _():
        carry_ref[...] = c0_ref[...]

    y = jnp.dot(h_ref[...], w_ref[...], preferred_element_type=F32)
    lane = lax.broadcasted_iota(jnp.int32, y.shape, 1)
    is_ki = lane < SM_WI
    ms = jnp.sum(jnp.where(is_ki, y * y, 0.0), axis=-1, keepdims=True) * (1.0 / IDX_DIM)
    ki = _rope_tile(y * lax.rsqrt(ms + EPS) * g_ref[...], cos_ref[...], sin_ref[...], IDX_DIM // 2)
    wi = y * (IDX_HEADS ** -0.5 * IDX_DIM ** -0.5)
    z = y + bf_ref[...]
    logf = -(jnp.maximum(-z, 0.0) + jnp.log(1.0 + jnp.exp(-jnp.abs(z))))
    hi = logf.astype(BF16)
    r1 = logf - hi.astype(F32)
    mid = r1.astype(BF16)
    lo = (r1 - mid.astype(F32)).astype(BF16)
    row = lax.broadcasted_iota(jnp.int32, (tm, tm), 0)
    col = lax.broadcasted_iota(jnp.int32, (tm, tm), 1)
    tri = (col <= row).astype(BF16)
    csum = (jnp.dot(tri, hi, preferred_element_type=F32) + jnp.dot(tri, mid, preferred_element_type=F32)
            + jnp.dot(tri, lo, preferred_element_type=F32)) + carry_ref[0:1, :]
    carry_ref[...] = jnp.broadcast_to(csum[tm - 1:tm, :], carry_ref.shape)
    o_ref[...] = jnp.where(is_ki, ki, jnp.where(lane < SM_FB, wi, csum))


def _proj_small(h, w, g_idx_pad, bf_pad, c0, cos, sin, tm):
    n, d = h.shape
    return pl.pallas_call(
        _small_kernel,
        grid=(n // tm,),
        in_specs=[pl.BlockSpec((tm, d), lambda i: (i, 0)),
                  pl.BlockSpec((d, LANES), lambda i: (0, 0)),
                  pl.BlockSpec((1, LANES), lambda i: (0, 0)),
                  pl.BlockSpec((1, LANES), lambda i: (0, 0)),
                  pl.BlockSpec((8, LANES), lambda i: (0, 0)),
                  pl.BlockSpec((tm, LANES), lambda i: (i, 0)),
                  pl.BlockSpec((tm, LANES), lambda i: (i, 0))],
        out_specs=pl.BlockSpec((tm, LANES), lambda i: (i, 0)),
        out_shape=jax.ShapeDtypeStruct((n, LANES), F32),
        scratch_shapes=[pltpu.VMEM((8, LANES), F32)],
        compiler_params=_cparams(("arbitrary",)),
        name="proj_small",
    )(h, w, g_idx_pad, bf_pad, c0, cos, sin)


def _key_to_float(key):
    bits = key ^ ((key >> 31) & jnp.int32(0x7FFFFFFF))
    return lax.bitcast_convert_type(bits, F32)


def _select_kernel(qi_ref, kit_ref, wi_ref, o_ref, s_ref, *, tq, tc, topk):
    i = pl.program_id(0)
    n_real = o_ref.shape[1] - META_PAD
    n_chunks = n_real // tc
    diag = (i * tq) // tc
    row = lax.broadcasted_iota(jnp.int32, (tq, tc), 0) + i * tq
    col = lax.broadcasted_iota(jnp.int32, (tq, tc), 1)
    wi = wi_ref[...]

    def scores(kc):
        s = jnp.zeros((tq, kc.shape[1]), F32)
        for h in range(IDX_HEADS):
            d = jnp.dot(qi_ref[h], kc, preferred_element_type=F32)
            s = s + wi[:, h:h + 1] * jnp.maximum(d, 0.0)
        return s

    def score_chunk(c, _):
        off = pl.multiple_of(c * tc, tc)
        s = scores(kit_ref[:, pl.ds(off, tc)])
        s_ref[:, pl.ds(off, tc)] = jnp.where(col + off <= row, s, NEG)
        return 0

    lax.fori_loop(0, diag + 1, score_chunk, 0)
    mcol = lax.broadcasted_iota(jnp.int32, (tq, META_PAD), 1)
    s_ref[:, n_real:] = jnp.where(mcol < N_META, scores(kit_ref[:, n_real:]), NEG)

    def count_ge(thr):
        thr_c = jnp.broadcast_to(thr, (tq, tc))

        def body(c, acc):
            off = pl.multiple_of(c * tc, tc)
            return acc + jnp.where(s_ref[:, pl.ds(off, tc)] >= thr_c, 1.0, 0.0)

        acc = lax.fori_loop(0, diag + 1, body, jnp.zeros((tq, tc), F32))
        macc = jnp.where(s_ref[:, n_real:] >= jnp.broadcast_to(thr, (tq, META_PAD)), 1.0, 0.0)
        return jnp.sum(acc, axis=1, keepdims=True) + jnp.sum(macc, axis=1, keepdims=True)

    kf = jnp.float32(topk)

    def bit_step(b, carry):
        t_key, cnt_t = carry
        cand = t_key + lax.shift_left(jnp.int32(1), 31 - b)
        cnt = count_ge(_key_to_float(cand))
        take = cnt >= kf
        return jnp.where(take, cand, t_key), jnp.where(take, cnt, cnt_t)

    t0 = jnp.full((tq, 1), jnp.iinfo(jnp.int32).min, jnp.int32)
    c0 = jnp.broadcast_to(((diag + 1) * tc + META_PAD).astype(F32), (tq, 1))
    t_key, cnt_t = lax.fori_loop(0, 32, bit_step, (t0, c0))
    thr = _key_to_float(t_key)

    def fill_chunk(c, _):
        off = pl.multiple_of(c * tc, tc)
        o_ref[:, pl.ds(off, tc)] = jnp.full((tq, tc), NEG, o_ref.dtype)
        return 0

    lax.fori_loop(diag + 1, n_chunks, fill_chunk, 0)

    ambiguous = jnp.logical_and(cnt_t > kf, thr > NEG)
    any_amb = jnp.max(jnp.where(ambiguous, 1.0, 0.0)) > 0.0

    @pl.when(jnp.logical_not(any_amb))
    def _():
        def out_chunk(c, _):
            off = pl.multiple_of(c * tc, tc)
            s = s_ref[:, pl.ds(off, tc)]
            sel = jnp.where(s >= jnp.broadcast_to(thr, (tq, tc)), 0.0, NEG)
            o_ref[:, pl.ds(off, tc)] = jnp.where(col + off <= row, sel, NEG).astype(o_ref.dtype)
            return 0

        lax.fori_loop(0, diag + 1, out_chunk, 0)
        sm = s_ref[:, n_real:]
        sel = jnp.where(sm >= jnp.broadcast_to(thr, (tq, META_PAD)), 0.0, NEG)
        o_ref[:, n_real:] = jnp.where(mcol < N_META, sel, NEG).astype(o_ref.dtype)

    @pl.when(any_amb)
    def _():
        def gt_body(c, acc):
            off = pl.multiple_of(c * tc, tc)
            return acc + jnp.where(s_ref[:, pl.ds(off, tc)] > jnp.broadcast_to(thr, (tq, tc)), 1.0, 0.0)

        acc = lax.fori_loop(0, diag + 1, gt_body, jnp.zeros((tq, tc), F32))
        sm = s_ref[:, n_real:]
        thr_m = jnp.broadcast_to(thr, (tq, META_PAD))
        cnt_gt = (jnp.sum(acc, axis=1, keepdims=True)
                  + jnp.sum(jnp.where(sm > thr_m, 1.0, 0.0), axis=1, keepdims=True))
        quota = jnp.where(ambiguous, kf - cnt_gt, jnp.float32(3e38))

        def tie_select(s, valid, running, width):
            r = lax.broadcasted_iota(jnp.int32, (width, width), 0)
            c_ = lax.broadcasted_iota(jnp.int32, (width, width), 1)
            upper = (r <= c_).astype(BF16)
            thr_b = jnp.broadcast_to(thr, s.shape)
            eq = jnp.where(s == thr_b, 1.0, 0.0)
            rank = jnp.dot(eq.astype(BF16), upper, preferred_element_type=F32) + running
            keep = jnp.where(s > thr_b, 0.0, jnp.where(jnp.logical_and(eq > 0.0, rank <= quota), 0.0, NEG))
            return jnp.where(valid, keep, NEG), running + jnp.sum(eq, axis=1, keepdims=True)

        bias_m, running = tie_select(sm, mcol < N_META, jnp.zeros((tq, 1), F32), META_PAD)
        o_ref[:, n_real:] = bias_m.astype(o_ref.dtype)

        def out_chunk(c, running):
            off = pl.multiple_of(c * tc, tc)
            bias, running = tie_select(s_ref[:, pl.ds(off, tc)], col + off <= row, running, tc)
            o_ref[:, pl.ds(off, tc)] = bias.astype(o_ref.dtype)
            return running

        lax.fori_loop(0, diag + 1, out_chunk, running)


def _select_bias(qi_hm, kit, wi, topk, tq, tc):
    _, t, _ = qi_hm.shape
    nk = kit.shape[1]
    return pl.pallas_call(
        functools.partial(_select_kernel, tq=tq, tc=tc, topk=topk),
        grid=(t // tq,),
        in_specs=[pl.BlockSpec((IDX_HEADS, tq, IDX_DIM), lambda i: (0, i, 0)),
                  pl.BlockSpec((IDX_DIM, nk), lambda i: (0, 0)),
                  pl.BlockSpec((tq, IDX_HEADS), lambda i: (i, 0))],
        out_specs=pl.BlockSpec((tq, nk), lambda i: (i, 0)),
        out_shape=jax.ShapeDtypeStruct((t, nk), BF16),
        scratch_shapes=[pltpu.VMEM((tq, nk), F32)],
        compiler_params=_cparams(("parallel",)),
        name="select_bias",
    )(qi_hm, kit, wi)


LOG2E = 1.4426950408889634


def _attn_kernel(*refs, t, fox):
    if fox:
        (qi_ref, kj_ref, q_ref, kt_ref, v_ref, ktm_ref, vm_ref, cq_ref, ck_ref, ckm_ref,
         o_ref, m_ref, l_ref, acc_ref, r_ref) = refs
    else:
        (qi_ref, kj_ref, q_ref, kt_ref, v_ref, ktm_ref, vm_ref, b_ref, bm_ref,
         o_ref, m_ref, l_ref, acc_ref) = refs
    step = pl.program_id(0)
    i = qi_ref[step]
    j = kj_ref[step]
    nheads = q_ref.shape[1] // HEAD_DIM
    c1 = HEAD_DIM ** -0.5 * LOG2E

    def block(kt_blk_ref, v_blk_ref, bias_fn):
        width = kt_blk_ref.shape[1]
        for h in range(nheads):
            sl = slice(h * HEAD_DIM, (h + 1) * HEAD_DIM)
            u = bias_fn(jnp.dot(q_ref[:, sl], kt_blk_ref[sl, :], preferred_element_type=F32) * c1, h)
            m_prev = m_ref[h]
            mx = jnp.max(u, axis=1, keepdims=True)
            if fox:
                r = r_ref[h]
                m_next = jnp.maximum(m_prev, mx + r)
                sub = m_next - r
            else:
                m_next = jnp.maximum(m_prev, mx)
                sub = m_next
            alpha = jnp.exp2(m_prev - m_next)
            ps = [jnp.exp2(u[:, c * LANES:(c + 1) * LANES] - sub) for c in range(width // LANES)]
            psum = ps[0]
            for pc in ps[1:]:
                psum = psum + pc
            p = ps[0] if len(ps) == 1 else jnp.concatenate(ps, axis=1)
            l_ref[h] = alpha * l_ref[h] + psum
            m_ref[h] = m_next
            acc_ref[h] = alpha * acc_ref[h] + jnp.dot(p.astype(BF16), v_blk_ref[:, sl],
                                                      preferred_element_type=F32)

    @pl.when(j == 0)
    def _():
        m_ref[...] = jnp.full(m_ref.shape, NEG, F32)
        l_ref[...] = jnp.zeros(l_ref.shape, F32)
        acc_ref[...] = jnp.zeros(acc_ref.shape, F32)
        mcol = lax.broadcasted_iota(jnp.int32, (t, META_PAD), 1)
        if fox:
            for h in range(nheads):
                r_ref[h] = jnp.broadcast_to(cq_ref[:, h:h + 1] * LOG2E, (t, LANES))

            def bias_fn(u, h):
                return jnp.where(mcol < N_META, u - ckm_ref[h:h + 1, :] * LOG2E, NEG)
        else:
            def bias_fn(u, h):
                return u + bm_ref[...].astype(F32)
        block(ktm_ref, vm_ref, bias_fn)

    if fox:
        @pl.when(j < i)
        def _():
            block(kt_ref, v_ref, lambda u, h: u - ck_ref[h:h + 1, :] * LOG2E)

        @pl.when(j == i)
        def _():
            row = lax.broadcasted_iota(jnp.int32, (t, t), 0)
            col = lax.broadcasted_iota(jnp.int32, (t, t), 1)
            block(kt_ref, v_ref, lambda u, h: jnp.where(col <= row, u - ck_ref[h:h + 1, :] * LOG2E, NEG))
    else:
        block(kt_ref, v_ref, lambda u, h: u + b_ref[...].astype(F32))

    @pl.when(j == i)
    def _():
        for h in range(nheads):
            sl = slice(h * HEAD_DIM, (h + 1) * HEAD_DIM)
            o_ref[:, sl] = (acc_ref[h] / jnp.sum(l_ref[h], axis=1, keepdims=True)).astype(o_ref.dtype)


def _attention(q, kt, v, extra, *, fox, t_blk):
    t, w = q.shape
    nheads = w // HEAD_DIM
    nq = t // t_blk
    mblk = t // META_PAD
    qi = jnp.asarray([i for i in range(nq) for _ in range(i + 1)], jnp.int32)
    kj = jnp.asarray([j for i in range(nq) for j in range(i + 1)], jnp.int32)

    in_specs = [pl.BlockSpec((t_blk, w), lambda s, qi, kj: (qi[s], 0)),
                pl.BlockSpec((w, t_blk), lambda s, qi, kj: (0, kj[s])),
                pl.BlockSpec((t_blk, w), lambda s, qi, kj: (kj[s], 0)),
                pl.BlockSpec((w, META_PAD), lambda s, qi, kj: (0, mblk)),
                pl.BlockSpec((META_PAD, w), lambda s, qi, kj: (mblk, 0))]
    scratch = [pltpu.VMEM((nheads, t_blk, LANES), F32), pltpu.VMEM((nheads, t_blk, LANES), F32),
               pltpu.VMEM((nheads, t_blk, HEAD_DIM), F32)]
    if fox:
        cq, ct = extra
        in_specs += [pl.BlockSpec((t_blk, nheads), lambda s, qi, kj: (qi[s], 0)),
                     pl.BlockSpec((nheads, t_blk), lambda s, qi, kj: (0, kj[s])),
                     pl.BlockSpec((nheads, META_PAD), lambda s, qi, kj: (0, mblk))]
        args = (q, kt, v, kt, v, cq, ct, ct)
        scratch.append(pltpu.VMEM((nheads, t_blk, LANES), F32))
    else:
        (bias,) = extra
        in_specs += [pl.BlockSpec((t_blk, t_blk), lambda s, qi, kj: (qi[s], kj[s])),
                     pl.BlockSpec((t_blk, META_PAD), lambda s, qi, kj: (qi[s], mblk))]
        args = (q, kt, v, kt, v, bias, bias)
    return pl.pallas_call(
        functools.partial(_attn_kernel, t=t_blk, fox=fox),
        grid_spec=pltpu.PrefetchScalarGridSpec(
            num_scalar_prefetch=2,
            grid=(int(qi.shape[0]),),
            in_specs=in_specs,
            out_specs=pl.BlockSpec((t_blk, w), lambda s, qi, kj: (qi[s], 0)),
            scratch_shapes=scratch),
        out_shape=jax.ShapeDtypeStruct((t, w), BF16),
        compiler_params=_cparams(("arbitrary",)),
        name="attn_fox" if fox else "attn_dsa",
    )(qi, kj, *args)


def _route(logits):
    lane = lax.broadcasted_iota(jnp.int32, logits.shape, 1).astype(F32)
    big = jnp.float32(LANES)
    ninf = -jnp.inf
    gl = jnp.where(lane < N_GROUPS, logits, ninf)
    gmax = jnp.max(gl, axis=1, keepdims=True)
    g_sel = jnp.min(jnp.where(gl == gmax, lane, big), axis=1, keepdims=True)
    p_group = 1.0 / jnp.sum(jnp.exp(gl - gmax), axis=1, keepdims=True)
    e_lo = RT_E0 + g_sel * EXPERTS_PER_GROUP
    in_grp = jnp.logical_and(lane >= e_lo, lane < e_lo + EXPERTS_PER_GROUP)
    el = jnp.where(in_grp, logits, ninf)
    emax = jnp.max(el, axis=1, keepdims=True)
    ex = jnp.exp(el - emax)
    prob = ex / jnp.sum(ex, axis=1, keepdims=True)
    prob = jnp.where(in_grp, prob, -1.0)
    p1 = jnp.max(prob, axis=1, keepdims=True)
    i1 = jnp.min(jnp.where(prob == p1, lane, big), axis=1, keepdims=True)
    rest = jnp.where(lane == i1, -1.0, prob)
    p2 = jnp.max(rest, axis=1, keepdims=True)
    i2 = jnp.min(jnp.where(rest == p2, lane, big), axis=1, keepdims=True)
    tot = p1 + p2
    w1 = p_group * (p1 / tot)
    w2 = p_group * (p2 / tot)
    return jnp.where(lane == i1, w1, jnp.where(lane == i2, w2, 0.0))


def _merge_kernel(h_ref, ya_ref, yb_ref, wga_ref, wgb_ref, wa_ref, wb_ref, wo_ref, x_ref, g_ref,
                  wrh_ref, wrl_ref, br_ref, h1_ref, t_ref, gate_ref, acc_ref):
    n = pl.program_id(1)

    @pl.when(n == 0)
    def _():
        acc_ref[...] = jnp.zeros(acc_ref.shape, F32)

    h = h_ref[...]
    ga = jnp.dot(h, wga_ref[...], preferred_element_type=F32)
    gb = jnp.dot(h, wgb_ref[...], preferred_element_type=F32)
    pa = jnp.dot(ya_ref[...], wa_ref[...], preferred_element_type=F32)
    pb = jnp.dot(yb_ref[...], wb_ref[...], preferred_element_type=F32)
    merged = jax.nn.sigmoid(ga) * pa + jax.nn.sigmoid(gb) * pb
    acc_ref[...] += jnp.dot(merged.astype(BF16), wo_ref[...], preferred_element_type=F32)

    @pl.when(n == pl.num_programs(1) - 1)
    def _():
        h1 = x_ref[...] + acc_ref[...]
        h1_ref[...] = h1
        ms = jnp.mean(h1 * h1, axis=-1, keepdims=True)
        t = h1 * lax.rsqrt(ms + EPS) * g_ref[...]
        t_hi = t.astype(BF16)
        t_ref[...] = t_hi
        t_lo = (t - t_hi.astype(F32)).astype(BF16)
        logits = (jnp.dot(t_hi, wrh_ref[...], preferred_element_type=F32)
                  + jnp.dot(t_hi, wrl_ref[...], preferred_element_type=F32)
                  + jnp.dot(t_lo, wrh_ref[...], preferred_element_type=F32)) + br_ref[...]
        gate_ref[...] = _route(logits)


def _merge(h, ya, yb, wga, wgb, wa, wb, wo, x, g_ffn, wr_hi, wr_lo, br, tm, tn):
    t, d = h.shape
    row = lambda i, n: (i, 0)
    return pl.pallas_call(
        _merge_kernel,
        grid=(t // tm, d // tn),
        in_specs=[pl.BlockSpec((tm, d), row),
                  pl.BlockSpec((tm, A_WIDTH), row),
                  pl.BlockSpec((tm, B_WIDTH), row),
                  pl.BlockSpec((d, tn), lambda i, n: (0, n)),
                  pl.BlockSpec((d, tn), lambda i, n: (0, n)),
                  pl.BlockSpec((A_WIDTH, tn), lambda i, n: (0, n)),
                  pl.BlockSpec((B_WIDTH, tn), lambda i, n: (0, n)),
                  pl.BlockSpec((tn, d), lambda i, n: (n, 0)),
                  pl.BlockSpec((tm, d), row),
                  pl.BlockSpec((1, d), lambda i, n: (0, 0)),
                  pl.BlockSpec((d, LANES), lambda i, n: (0, 0)),
                  pl.BlockSpec((d, LANES), lambda i, n: (0, 0)),
                  pl.BlockSpec((1, LANES), lambda i, n: (0, 0))],
        out_specs=[pl.BlockSpec((tm, d), row), pl.BlockSpec((tm, d), row), pl.BlockSpec((tm, LANES), row)],
        out_shape=[jax.ShapeDtypeStruct((t, d), F32), jax.ShapeDtypeStruct((t, d), BF16),
                   jax.ShapeDtypeStruct((t, LANES), F32)],
        scratch_shapes=[pltpu.VMEM((tm, d), F32)],
        compiler_params=_cparams(("parallel", "arbitrary")),
        name="merge_out_router",
    )(h, ya, yb, wga, wgb, wa, wb, wo, x, g_ffn.reshape(1, d), wr_hi, wr_lo, br)


def _moe_kernel(t_ref, gate_ref, wg_ref, wu_ref, wd_ref, h1_ref, gf_ref, o_ref, acc_ref):
    e = pl.program_id(1)

    @pl.when(e == 0)
    def _():
        acc_ref[...] = jnp.zeros(acc_ref.shape, F32)

    t = t_ref[...]
    gates = gate_ref[...]
    lane = lax.broadcasted_iota(jnp.int32, gates.shape, 1)
    g_e = jnp.sum(jnp.where(lane == RT_E0 + e, gates, 0.0), axis=1, keepdims=True)
    hg = jnp.dot(t, wg_ref[0], preferred_element_type=F32)
    hu = jnp.dot(t, wu_ref[0], preferred_element_type=F32)
    act = jax.nn.silu(hg) * hu * g_e
    acc_ref[...] += jnp.dot(act.astype(BF16), wd_ref[0], preferred_element_type=F32)

    @pl.when(e == pl.num_programs(1) - 1)
    def _():
        h2 = h1_ref[...] + acc_ref[...]
        ms = jnp.mean(h2 * h2, axis=-1, keepdims=True)
        o_ref[...] = h2 * lax.rsqrt(ms + EPS) * gf_ref[...]


def _moe(t_bf, gates, wg, wu, wd, h1, g_final, tm):
    t, d = t_bf.shape
    f = wg.shape[2]
    row = lambda i, e: (i, 0)
    return pl.pallas_call(
        _moe_kernel,
        grid=(t // tm, N_EXPERTS),
        in_specs=[pl.BlockSpec((tm, d), row),
                  pl.BlockSpec((tm, LANES), row),
                  pl.BlockSpec((1, d, f), lambda i, e: (e, 0, 0)),
                  pl.BlockSpec((1, d, f), lambda i, e: (e, 0, 0)),
                  pl.BlockSpec((1, f, d), lambda i, e: (e, 0, 0)),
                  pl.BlockSpec((tm, d), row),
                  pl.BlockSpec((1, d), lambda i, e: (0, 0))],
        out_specs=pl.BlockSpec((tm, d), row),
        out_shape=jax.ShapeDtypeStruct((t, d), F32),
        scratch_shapes=[pltpu.VMEM((tm, d), F32)],
        compiler_params=_cparams(("parallel", "arbitrary")),
        name="moe_final",
    )(t_bf, gates, wg, wu, wd, h1, g_final.reshape(1, d))


def _pick(n, pref):
    return pref if n % pref == 0 else n


def _layer(x, meta, g_mix, w_in, g_kv, w_kv_up, g_idx_k, b_f, w_branch_a, w_branch_b, w_out,
           g_ffn, w_group, b_group, w_expert, b_expert, w_gate_e, w_up_e, w_down_e, g_final):
    t, d = x.shape
    n_all = t + N_META
    topk = min(TOPK_MAX, n_all // 4)

    o = [0]
    for wdt in (A_WIDTH, A_KV_RANK, IDX_HEADS * IDX_DIM, IDX_DIM, IDX_HEADS, B_WIDTH, B_WIDTH, B_WIDTH,
                B_HEADS, D_MODEL, D_MODEL):
        o.append(o[-1] + wdt)
    wb = w_in.astype(BF16)
    w_qa, w_ckv, w_qi = wb[:, o[0]:o[1]], wb[:, o[1]:o[2]], wb[:, o[2]:o[3]]
    w_small = jnp.concatenate(
        [wb[:, o[3]:o[5]], wb[:, o[8]:o[9]], jnp.zeros((d, LANES - (o[5] - o[3]) - B_HEADS), BF16)], axis=1)
    w_b = wb[:, o[5]:o[8]]
    w_ga, w_gb = wb[:, o[9]:o[10]], wb[:, o[10]:o[11]]
    w_up_bf = w_kv_up.astype(BF16)
    g_idx_pad = jnp.zeros((1, LANES), F32).at[0, SM_KI:SM_KI + IDX_DIM].set(g_idx_k)
    bf_pad = jnp.zeros((1, LANES), F32).at[0, SM_FB:SM_FB + B_HEADS].set(b_f)

    pos_meta = jnp.arange(N_META)
    pos_real = jnp.arange(N_META, n_all)
    cos64_m, sin64_m = _rope_tables(pos_meta, HEAD_DIM // 2)
    cos64_r, sin64_r = _rope_tables(pos_real, HEAD_DIM // 2)
    cos32_m, sin32_m = _rope_tables(pos_meta, IDX_DIM // 2)
    cos32_r, sin32_r = _rope_tables(pos_real, IDX_DIM // 2)

    tm = _pick(t, 512)

    hm = _rmsnorm(meta, g_mix, N_META)
    ka_m, va_m = _proj_kv(hm, w_ckv, g_kv, w_up_bf, cos64_m, sin64_m, N_META)
    small_m = _proj_small(hm, w_small, g_idx_pad, bf_pad, jnp.zeros((8, LANES), F32), cos32_m, sin32_m, N_META)
    b_m = _proj_plain(hm, w_b, N_META, 1024)
    c_carry = jnp.broadcast_to(small_m[N_META - 1:N_META, :], (8, LANES))

    h = _rmsnorm(x, g_mix, tm)
    qa = _proj_rope(h, w_qa, cos64_r, sin64_r, HEAD_DIM // 2, tm, 1024)
    ka_r, va_r = _proj_kv(h, w_ckv, g_kv, w_up_bf, cos64_r, sin64_r, tm)
    qi = _proj_rope(h, w_qi, cos32_r, sin32_r, IDX_DIM // 2, tm, 1024)
    small_r = _proj_small(h, w_small, g_idx_pad, bf_pad, c_carry, cos32_r, sin32_r, tm)
    b_r = _proj_plain(h, w_b, tm, 1024)

    def with_meta(real, meta_rows):
        pad = jnp.zeros((META_PAD - N_META, real.shape[1]), real.dtype)
        return jnp.concatenate([real, meta_rows.astype(real.dtype), pad], axis=0)

    ka = with_meta(ka_r, ka_m)
    va = with_meta(va_r, va_m)
    kb = with_meta(b_r[:, B_WIDTH:2 * B_WIDTH], b_m[:, B_WIDTH:2 * B_WIDTH])
    vb = with_meta(b_r[:, 2 * B_WIDTH:], b_m[:, 2 * B_WIDTH:])
    qb = b_r[:, :B_WIDTH]
    ki = with_meta(small_r[:, SM_KI:SM_KI + IDX_DIM], small_m[:, SM_KI:SM_KI + IDX_DIM]).astype(BF16)
    c_all = with_meta(small_r[:, SM_FB:SM_FB + B_HEADS], small_m[:, SM_FB:SM_FB + B_HEADS])
    wi = small_r[:, SM_WI:SM_WI + IDX_HEADS]
    cq = small_r[:, SM_FB:SM_FB + B_HEADS]

    qi_hm = qi.reshape(t, IDX_HEADS, IDX_DIM).transpose(1, 0, 2)
    bias = _select_bias(qi_hm, ki.T, wi, topk, _pick(t, 256), 256)
    ya = _attention(qa, ka.T, va, (bias,), fox=False, t_blk=_pick(t, 512))
    yb = _attention(qb, kb.T, vb, (cq, c_all.T), fox=True, t_blk=_pick(t, 512))

    w_r = jnp.concatenate([w_group, w_expert, jnp.zeros((d, LANES - N_GROUPS - N_EXPERTS), F32)], axis=1)
    wr_hi = w_r.astype(BF16)
    wr_lo = (w_r - wr_hi.astype(F32)).astype(BF16)
    b_r_pad = jnp.concatenate([b_group, b_expert, jnp.zeros((LANES - N_GROUPS - N_EXPERTS,), F32)]).reshape(1, LANES)
    h1, t_bf, gates = _merge(h, ya, yb, w_ga, w_gb, w_branch_a.astype(BF16), w_branch_b.astype(BF16),
                             w_out.astype(BF16), x, g_ffn, wr_hi, wr_lo, b_r_pad, tm, 256)

    return _moe(t_bf, gates, w_gate_e.astype(BF16), w_up_e.astype(BF16), w_down_e.astype(BF16), h1, g_final, tm)


def kernel(x, meta_tokens, g_mix, w_in, g_kv, w_kv_up, g_idx_k, b_f, w_branch_a, w_branch_b, w_out, g_ffn,
           w_group, b_group, w_expert, b_expert, w_gate_e, w_up_e, w_down_e, g_final):
    assert x.shape[0] == 1 and g_mix.shape[0] == 1, "single batch, single layer"
    out = _layer(x[0], meta_tokens.astype(x.dtype), g_mix[0], w_in[0], g_kv[0], w_kv_up[0], g_idx_k[0], b_f[0],
                 w_branch_a[0], w_branch_b[0], w_out[0], g_ffn[0], w_group[0], b_group[0], w_expert[0],
                 b_expert[0], w_gate_e[0], w_up_e[0], w_down_e[0], g_final)
    return out[None]
```

```python
import functools

import jax
import jax.numpy as jnp
from jax import lax
from jax.experimental import pallas as pl
from jax.experimental.pallas import tpu as pltpu

F32 = jnp.float32
BF16 = jnp.bfloat16

D_MODEL = 2048
N_META = 16
HEAD_DIM = 128
ROPE_THETA = 10000.0
EPS = 1e-6
NEG = -1e30
A_HEADS = 8
A_WIDTH = A_HEADS * HEAD_DIM
A_KV_RANK = 512
IDX_HEADS = 16
IDX_DIM = 64
TOPK_MAX = 256
B_HEADS = 8
B_WIDTH = B_HEADS * HEAD_DIM
N_GROUPS = 4
EXPERTS_PER_GROUP = 4
N_EXPERTS = N_GROUPS * EXPERTS_PER_GROUP
D_FF_EXPERT = 512

LANES = 128
META_PAD = LANES
VMEM_LIMIT = 56 * 1024 * 1024
LOG2E = 1.4426950408889634
NT_DIMS = (((1,), (1,)), ((), ()))

SM_KI = 0
SM_WI = IDX_DIM
SM_FB = IDX_DIM + IDX_HEADS
RT_E0 = N_GROUPS


def _cparams(sem):
    return pltpu.CompilerParams(dimension_semantics=sem, vmem_limit_bytes=VMEM_LIMIT)


def _rms_kernel(x_ref, g_ref, o_ref):
    x = x_ref[...]
    ms = jnp.mean(x * x, axis=-1, keepdims=True)
    o_ref[...] = (x * lax.rsqrt(ms + EPS) * g_ref[...]).astype(o_ref.dtype)


def _rmsnorm(x, g, tm):
    n, d = x.shape
    return pl.pallas_call(
        _rms_kernel,
        grid=(n // tm,),
        in_specs=[pl.BlockSpec((tm, d), lambda i: (i, 0)), pl.BlockSpec((1, d), lambda i: (0, 0))],
        out_specs=pl.BlockSpec((tm, d), lambda i: (i, 0)),
        out_shape=jax.ShapeDtypeStruct((n, d), BF16),
        compiler_params=_cparams(("parallel",)),
        name="rmsnorm",
    )(x, g.reshape(1, d))


def _rope_tile(y, cos, sin_signed, half):
    if 2 * half == LANES:
        partner = pltpu.roll(y, half, axis=1)
    else:
        lane = lax.broadcasted_iota(jnp.int32, y.shape, 1)
        first = (lane % (2 * half)) < half
        partner = jnp.where(first, pltpu.roll(y, LANES - half, axis=1), pltpu.roll(y, half, axis=1))
    return y * cos + partner * sin_signed


def _rope_tables(pos, half):
    inv_freq = ROPE_THETA ** (-jnp.arange(half, dtype=F32) / half)
    ang = pos.astype(F32)[:, None] * inv_freq[None, :]
    cos, sin = jnp.cos(ang), jnp.sin(ang)
    reps = LANES // (2 * half)
    cos_t = jnp.tile(jnp.concatenate([cos, cos], axis=1), (1, reps))
    sin_t = jnp.tile(jnp.concatenate([-sin, sin], axis=1), (1, reps))
    return cos_t, sin_t


def _proj_plain_kernel(h_ref, w_ref, o_ref):
    o_ref[...] = jnp.dot(h_ref[...], w_ref[...], preferred_element_type=F32).astype(o_ref.dtype)


def _proj_plain(h, w, tm, tn):
    n, d = h.shape
    nc = w.shape[1]
    return pl.pallas_call(
        _proj_plain_kernel,
        grid=(n // tm, nc // tn),
        in_specs=[pl.BlockSpec((tm, d), lambda i, j: (i, 0)), pl.BlockSpec((d, tn), lambda i, j: (0, j))],
        out_specs=pl.BlockSpec((tm, tn), lambda i, j: (i, j)),
        out_shape=jax.ShapeDtypeStruct((n, nc), BF16),
        compiler_params=_cparams(("parallel", "parallel")),
        name="proj_plain",
    )(h, w)


def _proj_rope_kernel(h_ref, w_ref, cos_ref, sin_ref, o_ref, *, half, head_major):
    y = jnp.dot(h_ref[...], w_ref[...], preferred_element_type=F32)
    cos, sin = cos_ref[...], sin_ref[...]
    per_tile = LANES // (2 * half)
    for c in range(y.shape[1] // LANES):
        sl = slice(c * LANES, (c + 1) * LANES)
        roped = _rope_tile(y[:, sl], cos, sin, half).astype(o_ref.dtype)
        if head_major:
            for k in range(per_tile):
                o_ref[c * per_tile + k] = roped[:, k * 2 * half:(k + 1) * 2 * half]
        else:
            o_ref[:, sl] = roped


def _proj_rope(h, w, cos, sin, half, tm, tn, head_major=False):
    n, d = h.shape
    nc = w.shape[1]
    if head_major:
        assert tn == nc
        heads = nc // (2 * half)
        out_spec = pl.BlockSpec((heads, tm, 2 * half), lambda i, j: (0, i, 0))
        out_shape = jax.ShapeDtypeStruct((heads, n, 2 * half), BF16)
    else:
        out_spec = pl.BlockSpec((tm, tn), lambda i, j: (i, j))
        out_shape = jax.ShapeDtypeStruct((n, nc), BF16)
    return pl.pallas_call(
        functools.partial(_proj_rope_kernel, half=half, head_major=head_major),
        grid=(n // tm, nc // tn),
        in_specs=[pl.BlockSpec((tm, d), lambda i, j: (i, 0)), pl.BlockSpec((d, tn), lambda i, j: (0, j)),
                  pl.BlockSpec((tm, LANES), lambda i, j: (i, 0)), pl.BlockSpec((tm, LANES), lambda i, j: (i, 0))],
        out_specs=out_spec,
        out_shape=out_shape,
        compiler_params=_cparams(("parallel", "parallel")),
        name="proj_rope",
    )(h, w, cos, sin)


def _kv_kernel(h_ref, wc_ref, g_ref, wup_ref, cos_ref, sin_ref, ka_ref, va_ref):
    c = jnp.dot(h_ref[...], wc_ref[...], preferred_element_type=F32)
    ms = jnp.mean(c * c, axis=-1, keepdims=True)
    cn = (c * lax.rsqrt(ms + EPS) * g_ref[...]).astype(BF16)
    kv = jnp.dot(cn, wup_ref[...], preferred_element_type=F32)
    cos, sin = cos_ref[...], sin_ref[...]
    for hd in range(A_HEADS):
        sl = slice(hd * HEAD_DIM, (hd + 1) * HEAD_DIM)
        ka_ref[:, sl] = _rope_tile(kv[:, sl], cos, sin, HEAD_DIM // 2).astype(ka_ref.dtype)
    va_ref[...] = kv[:, A_WIDTH:].astype(va_ref.dtype)


def _proj_kv(h, wc, g_kv, wup, cos, sin, tm):
    n, d = h.shape
    return pl.pallas_call(
        _kv_kernel,
        grid=(n // tm,),
        in_specs=[pl.BlockSpec((tm, d), lambda i: (i, 0)),
                  pl.BlockSpec((d, A_KV_RANK), lambda i: (0, 0)),
                  pl.BlockSpec((1, A_KV_RANK), lambda i: (0, 0)),
                  pl.BlockSpec((A_KV_RANK, 2 * A_WIDTH), lambda i: (0, 0)),
                  pl.BlockSpec((tm, LANES), lambda i: (i, 0)),
                  pl.BlockSpec((tm, LANES), lambda i: (i, 0))],
        out_specs=[pl.BlockSpec((tm, A_WIDTH), lambda i: (i, 0)), pl.BlockSpec((tm, A_WIDTH), lambda i: (i, 0))],
        out_shape=[jax.ShapeDtypeStruct((n, A_WIDTH), BF16), jax.ShapeDtypeStruct((n, A_WIDTH), BF16)],
        compiler_params=_cparams(("parallel",)),
        name="proj_kv",
    )(h, wc, g_kv.reshape(1, A_KV_RANK), wup, cos, sin)


def _small_kernel(h_ref, w_ref, g_ref, bf_ref, c0_ref, cos_ref, sin_ref, o_ref, carry_ref):
    tm = h_ref.shape[0]

    @pl.when(pl.program_id(0) == 0)
    def _():
        carry_ref[...] = c0_ref[...]

    y = jnp.dot(h_ref[...], w_ref[...], preferred_element_type=F32)
    lane = lax.broadcasted_iota(jnp.int32, y.shape, 1)
    is_ki = lane < SM_WI
    ms = jnp.sum(jnp.where(is_ki, y * y, 0.0), axis=-1, keepdims=True) * (1.0 / IDX_DIM)
    ki = _rope_tile(y * lax.rsqrt(ms + EPS) * g_ref[...], cos_ref[...], sin_ref[...], IDX_DIM // 2)
    wi = y * (IDX_HEADS ** -0.5 * IDX_DIM ** -0.5)
    z = y + bf_ref[...]
    logf = -(jnp.maximum(-z, 0.0) + jnp.log(1.0 + jnp.exp(-jnp.abs(z))))
    hi = logf.astype(BF16)
    r1 = logf - hi.astype(F32)
    mid = r1.astype(BF16)
    lo = (r1 - mid.astype(F32)).astype(BF16)
    row = lax.broadcasted_iota(jnp.int32, (tm, tm), 0)
    col = lax.broadcasted_iota(jnp.int32, (tm, tm), 1)
    tri = (col <= row).astype(BF16)
    csum = (jnp.dot(tri, hi, preferred_element_type=F32) + jnp.dot(tri, mid, preferred_element_type=F32)
            + jnp.dot(tri, lo, preferred_element_type=F32)) + carry_ref[0:1, :]
    carry_ref[...] = jnp.broadcast_to(csum[tm - 1:tm, :], carry_ref.shape)
    o_ref[...] = jnp.where(is_ki, ki, jnp.where(lane < SM_FB, wi, csum))


def _proj_small(h, w, g_idx_pad, bf_pad, c0, cos, sin, tm):
    n, d = h.shape
    return pl.pallas_call(
        _small_kernel,
        grid=(n // tm,),
        in_specs=[pl.BlockSpec((tm, d), lambda i: (i, 0)),
                  pl.BlockSpec((d, LANES), lambda i: (0, 0)),
                  pl.BlockSpec((1, LANES), lambda i: (0, 0)),
                  pl.BlockSpec((1, LANES), lambda i: (0, 0)),
                  pl.BlockSpec((8, LANES), lambda i: (0, 0)),
                  pl.BlockSpec((tm, LANES), lambda i: (i, 0)),
                  pl.BlockSpec((tm, LANES), lambda i: (i, 0))],
        out_specs=pl.BlockSpec((tm, LANES), lambda i: (i, 0)),
        out_shape=jax.ShapeDtypeStruct((n, LANES), F32),
        scratch_shapes=[pltpu.VMEM((8, LANES), F32)],
        compiler_params=_cparams(("arbitrary",)),
        name="proj_small",
    )(h, w, g_idx_pad, bf_pad, c0, cos, sin)


COUNT_ROWS = 64


def _key_to_float(key):
    bits = key ^ ((key >> 31) & jnp.int32(0x7FFFFFFF))
    return lax.bitcast_convert_type(bits, F32)


def _select_kernel(qi_ref, kir_ref, kim_ref, wt_ref, or_ref, om_ref, s_ref, *, tq, topk):
    i = pl.program_id(0)
    n_real = or_ref.shape[1]
    n_chunks = n_real // tq
    krow = lax.broadcasted_iota(jnp.int32, (tq, tq), 0)
    qcol = lax.broadcasted_iota(jnp.int32, (tq, tq), 1)
    mrow = lax.broadcasted_iota(jnp.int32, (META_PAD, tq), 0)
    wt = wt_ref[...]

    def scores(kc):
        s = jnp.zeros((kc.shape[0], tq), F32)
        for h in range(IDX_HEADS):
            d = lax.dot_general(kc, qi_ref[h], NT_DIMS, preferred_element_type=F32)
            s = s + wt[h:h + 1, :] * jnp.maximum(d, 0.0)
        return s

    def causal(c):
        return krow + (c - i) * tq <= qcol

    def score_chunk(c, _):
        off = pl.multiple_of(c * tq, tq)
        s_ref[pl.ds(off, tq), :] = jnp.where(causal(c), scores(kir_ref[pl.ds(off, tq), :]), NEG)
        return 0

    lax.fori_loop(0, i + 1, score_chunk, 0)
    s_ref[n_real:, :] = jnp.where(mrow < N_META, scores(kim_ref[...]), NEG)

    def count(cmp, thr):
        def body(c, acc):
            off = c * tq
            for r in range(tq // COUNT_ROWS):
                blk = s_ref[pl.ds(pl.multiple_of(off + r * COUNT_ROWS, COUNT_ROWS), COUNT_ROWS), :]
                acc = acc + jnp.where(cmp(blk, thr), 1.0, 0.0)
            return acc

        acc = lax.fori_loop(0, i + 1, body, jnp.zeros((COUNT_ROWS, tq), F32))
        for r in range(META_PAD // COUNT_ROWS):
            blk = s_ref[n_real + r * COUNT_ROWS:n_real + (r + 1) * COUNT_ROWS, :]
            acc = acc + jnp.where(cmp(blk, thr), 1.0, 0.0)
        return jnp.sum(acc, axis=0, keepdims=True)

    kf = jnp.float32(topk)

    def bit_step(b, carry):
        t_key, cnt_t = carry
        cand = t_key + lax.shift_left(jnp.int32(1), 31 - b)
        cnt = count(lambda s, t: s >= t, _key_to_float(cand))
        take = cnt >= kf
        return jnp.where(take, cand, t_key), jnp.where(take, cnt, cnt_t)

    t0 = jnp.full((1, tq), jnp.iinfo(jnp.int32).min, jnp.int32)
    c0 = jnp.broadcast_to(((i + 1) * tq + META_PAD).astype(F32), (1, tq))
    t_key, cnt_t = lax.fori_loop(0, 32, bit_step, (t0, c0))
    thr = _key_to_float(t_key)

    def fill_chunk(c, _):
        or_ref[:, pl.ds(pl.multiple_of(c * tq, tq), tq)] = jnp.full((tq, tq), NEG, or_ref.dtype)
        return 0

    lax.fori_loop(i + 1, n_chunks, fill_chunk, 0)

    ambiguous = jnp.logical_and(cnt_t > kf, thr > NEG)
    any_amb = jnp.max(jnp.where(ambiguous, 1.0, 0.0)) > 0.0

    @pl.when(jnp.logical_not(any_amb))
    def _():
        def out_chunk(c, _):
            off = pl.multiple_of(c * tq, tq)
            sel = jnp.where(s_ref[pl.ds(off, tq), :] >= thr, 0.0, NEG)
            or_ref[:, pl.ds(off, tq)] = jnp.where(causal(c), sel, NEG).T.astype(or_ref.dtype)
            return 0

        lax.fori_loop(0, i + 1, out_chunk, 0)
        sel = jnp.where(s_ref[n_real:, :] >= thr, 0.0, NEG)
        om_ref[...] = jnp.where(mrow < N_META, sel, NEG).T.astype(om_ref.dtype)

    @pl.when(any_amb)
    def _():
        cnt_gt = count(lambda s, t: s > t, thr)
        quota = jnp.where(ambiguous, kf - cnt_gt, jnp.float32(3e38))

        def tie_select(s, valid, running):
            rows = s.shape[0]
            r = lax.broadcasted_iota(jnp.int32, (rows, rows), 0)
            c_ = lax.broadcasted_iota(jnp.int32, (rows, rows), 1)
            lower = (c_ <= r).astype(BF16)
            eq = jnp.where(s == thr, 1.0, 0.0)
            rank = jnp.dot(lower, eq.astype(BF16), preferred_element_type=F32) + running
            keep = jnp.where(s > thr, 0.0, jnp.where(jnp.logical_and(eq > 0.0, rank <= quota), 0.0, NEG))
            return jnp.where(valid, keep, NEG), running + jnp.sum(eq, axis=0, keepdims=True)

        bias_m, running = tie_select(s_ref[n_real:, :], mrow < N_META, jnp.zeros((1, tq), F32))
        om_ref[...] = bias_m.T.astype(om_ref.dtype)

        def out_chunk(c, running):
            off = pl.multiple_of(c * tq, tq)
            bias, running = tie_select(s_ref[pl.ds(off, tq), :], causal(c), running)
            or_ref[:, pl.ds(off, tq)] = bias.T.astype(or_ref.dtype)
            return running

        lax.fori_loop(0, i + 1, out_chunk, running)


def _select_bias(qi_hm, ki_real, ki_meta, wt, topk, tq):
    _, t, _ = qi_hm.shape
    return pl.pallas_call(
        functools.partial(_select_kernel, tq=tq, topk=topk),
        grid=(t // tq,),
        in_specs=[pl.BlockSpec((IDX_HEADS, tq, IDX_DIM), lambda i: (0, i, 0)),
                  pl.BlockSpec((t, IDX_DIM), lambda i: (0, 0)),
                  pl.BlockSpec((META_PAD, IDX_DIM), lambda i: (0, 0)),
                  pl.BlockSpec((IDX_HEADS, tq), lambda i: (0, i))],
        out_specs=[pl.BlockSpec((tq, t), lambda i: (i, 0)), pl.BlockSpec((tq, META_PAD), lambda i: (i, 0))],
        out_shape=[jax.ShapeDtypeStruct((t, t), BF16), jax.ShapeDtypeStruct((t, META_PAD), BF16)],
        scratch_shapes=[pltpu.VMEM((t + META_PAD, tq), F32)],
        compiler_params=_cparams(("parallel",)),
        name="select_bias",
    )(qi_hm, ki_real, ki_meta, wt)


def _attn_kernel(*refs, t, fox):
    if fox:
        (qi_ref, kj_ref, q_ref, k_ref, v_ref, km_ref, vm_ref, cq_ref, ck_ref, ckm_ref,
         o_ref, m_ref, l_ref, acc_ref, r_ref) = refs
    else:
        (qi_ref, kj_ref, q_ref, k_ref, v_ref, km_ref, vm_ref, b_ref, bm_ref,
         o_ref, m_ref, l_ref, acc_ref) = refs
    step = pl.program_id(0)
    i = qi_ref[step]
    j = kj_ref[step]
    nheads = q_ref.shape[1] // HEAD_DIM
    c1 = HEAD_DIM ** -0.5 * LOG2E

    def block(k_blk_ref, v_blk_ref, bias_fn):
        width = k_blk_ref.shape[0]
        for h in range(nheads):
            sl = slice(h * HEAD_DIM, (h + 1) * HEAD_DIM)
            s = lax.dot_general(q_ref[:, sl], k_blk_ref[:, sl], NT_DIMS, preferred_element_type=F32)
            u = bias_fn(s * c1, h)
            m_prev = m_ref[h]
            mx = jnp.max(u, axis=1, keepdims=True)
            if fox:
                r = r_ref[h]
                m_next = jnp.maximum(m_prev, mx + r)
                sub = m_next - r
            else:
                m_next = jnp.maximum(m_prev, mx)
                sub = m_next
            alpha = jnp.exp2(m_prev - m_next)
            ps = [jnp.exp2(u[:, c * LANES:(c + 1) * LANES] - sub) for c in range(width // LANES)]
            psum = ps[0]
            for pc in ps[1:]:
                psum = psum + pc
            p = ps[0] if len(ps) == 1 else jnp.concatenate(ps, axis=1)
            l_ref[h] = alpha * l_ref[h] + psum
            m_ref[h] = m_next
            acc_ref[h] = alpha * acc_ref[h] + jnp.dot(p.astype(BF16), v_blk_ref[:, sl],
                                                      preferred_element_type=F32)

    @pl.when(j == 0)
    def _():
        m_ref[...] = jnp.full(m_ref.shape, NEG, F32)
        l_ref[...] = jnp.zeros(l_ref.shape, F32)
        acc_ref[...] = jnp.zeros(acc_ref.shape, F32)
        mcol = lax.broadcasted_iota(jnp.int32, (t, META_PAD), 1)
        if fox:
            for h in range(nheads):
                r_ref[h] = jnp.broadcast_to(cq_ref[:, h:h + 1] * LOG2E, (t, LANES))

            def bias_fn(u, h):
                return jnp.where(mcol < N_META, u - ckm_ref[h:h + 1, :] * LOG2E, NEG)
        else:
            def bias_fn(u, h):
                return u + bm_ref[...].astype(F32)
        block(km_ref, vm_ref, bias_fn)

    if fox:
        @pl.when(j < i)
        def _():
            block(k_ref, v_ref, lambda u, h: u - ck_ref[h:h + 1, :] * LOG2E)

        @pl.when(j == i)
        def _():
            row = lax.broadcasted_iota(jnp.int32, (t, t), 0)
            col = lax.broadcasted_iota(jnp.int32, (t, t), 1)
            block(k_ref, v_ref, lambda u, h: jnp.where(col <= row, u - ck_ref[h:h + 1, :] * LOG2E, NEG))
    else:
        block(k_ref, v_ref, lambda u, h: u + b_ref[...].astype(F32))

    @pl.when(j == i)
    def _():
        for h in range(nheads):
            sl = slice(h * HEAD_DIM, (h + 1) * HEAD_DIM)
            o_ref[:, sl] = (acc_ref[h] / jnp.sum(l_ref[h], axis=1, keepdims=True)).astype(o_ref.dtype)


def _attention(qkv, meta_kv, extra, *, fox, t_blk):
    q_arr, q_col, k_arr, k_col, v_arr, v_col = qkv
    km_arr, km_col, vm_arr, vm_col = meta_kv
    t = q_arr.shape[0]
    w = A_WIDTH
    nheads = w // HEAD_DIM
    nq = t // t_blk
    qi = jnp.asarray([i for i in range(nq) for _ in range(i + 1)], jnp.int32)
    kj = jnp.asarray([j for i in range(nq) for j in range(i + 1)], jnp.int32)

    in_specs = [pl.BlockSpec((t_blk, w), lambda s, qi, kj: (qi[s], q_col)),
                pl.BlockSpec((t_blk, w), lambda s, qi, kj: (kj[s], k_col)),
                pl.BlockSpec((t_blk, w), lambda s, qi, kj: (kj[s], v_col)),
                pl.BlockSpec((META_PAD, w), lambda s, qi, kj: (0, km_col)),
                pl.BlockSpec((META_PAD, w), lambda s, qi, kj: (0, vm_col))]
    scratch = [pltpu.VMEM((nheads, t_blk, LANES), F32), pltpu.VMEM((nheads, t_blk, LANES), F32),
               pltpu.VMEM((nheads, t_blk, HEAD_DIM), F32)]
    args = (q_arr, k_arr, v_arr, km_arr, vm_arr)
    if fox:
        cq, ct_real, ct_meta = extra
        in_specs += [pl.BlockSpec((t_blk, nheads), lambda s, qi, kj: (qi[s], 0)),
                     pl.BlockSpec((nheads, t_blk), lambda s, qi, kj: (0, kj[s])),
                     pl.BlockSpec((nheads, META_PAD), lambda s, qi, kj: (0, 0))]
        args += (cq, ct_real, ct_meta)
        scratch.append(pltpu.VMEM((nheads, t_blk, LANES), F32))
    else:
        bias_real, bias_meta = extra
        in_specs += [pl.BlockSpec((t_blk, t_blk), lambda s, qi, kj: (qi[s], kj[s])),
                     pl.BlockSpec((t_blk, META_PAD), lambda s, qi, kj: (qi[s], 0))]
        args += (bias_real, bias_meta)
    return pl.pallas_call(
        functools.partial(_attn_kernel, t=t_blk, fox=fox),
        grid_spec=pltpu.PrefetchScalarGridSpec(
            num_scalar_prefetch=2,
            grid=(int(qi.shape[0]),),
            in_specs=in_specs,
            out_specs=pl.BlockSpec((t_blk, w), lambda s, qi, kj: (qi[s], 0)),
            scratch_shapes=scratch),
        out_shape=jax.ShapeDtypeStruct((t, w), BF16),
        compiler_params=_cparams(("arbitrary",)),
        name="attn_fox" if fox else "attn_dsa",
    )(qi, kj, *args)


def _route(logits):
    lane = lax.broadcasted_iota(jnp.int32, logits.shape, 1).astype(F32)
    big = jnp.float32(LANES)
    ninf = -jnp.inf
    gl = jnp.where(lane < N_GROUPS, logits, ninf)
    gmax = jnp.max(gl, axis=1, keepdims=True)
    g_sel = jnp.min(jnp.where(gl == gmax, lane, big), axis=1, keepdims=True)
    p_group = 1.0 / jnp.sum(jnp.exp(gl - gmax), axis=1, keepdims=True)
    e_lo = RT_E0 + g_sel * EXPERTS_PER_GROUP
    in_grp = jnp.logical_and(lane >= e_lo, lane < e_lo + EXPERTS_PER_GROUP)
    el = jnp.where(in_grp, logits, ninf)
    emax = jnp.max(el, axis=1, keepdims=True)
    ex = jnp.exp(el - emax)
    prob = ex / jnp.sum(ex, axis=1, keepdims=True)
    prob = jnp.where(in_grp, prob, -1.0)
    p1 = jnp.max(prob, axis=1, keepdims=True)
    i1 = jnp.min(jnp.where(prob == p1, lane, big), axis=1, keepdims=True)
    rest = jnp.where(lane == i1, -1.0, prob)
    p2 = jnp.max(rest, axis=1, keepdims=True)
    i2 = jnp.min(jnp.where(rest == p2, lane, big), axis=1, keepdims=True)
    tot = p1 + p2
    w1 = p_group * (p1 / tot)
    w2 = p_group * (p2 / tot)
    return jnp.where(lane == i1, w1, jnp.where(lane == i2, w2, 0.0))


def _merge_kernel(h_ref, ya_ref, yb_ref, wga_ref, wgb_ref, wa_ref, wb_ref, wo_ref, x_ref, g_ref,
                  wrh_ref, wrl_ref, br_ref, h1_ref, t_ref, gate_ref, acc_ref):
    n = pl.program_id(1)

    @pl.when(n == 0)
    def _():
        acc_ref[...] = jnp.zeros(acc_ref.shape, F32)

    h = h_ref[...]
    ga = jnp.dot(h, wga_ref[...], preferred_element_type=F32)
    gb = jnp.dot(h, wgb_ref[...], preferred_element_type=F32)
    pa = jnp.dot(ya_ref[...], wa_ref[...], preferred_element_type=F32)
    pb = jnp.dot(yb_ref[...], wb_ref[...], preferred_element_type=F32)
    merged = jax.nn.sigmoid(ga) * pa + jax.nn.sigmoid(gb) * pb
    acc_ref[...] += jnp.dot(merged.astype(BF16), wo_ref[...], preferred_element_type=F32)

    @pl.when(n == pl.num_programs(1) - 1)
    def _():
        h1 = x_ref[...] + acc_ref[...]
        h1_ref[...] = h1
        ms = jnp.mean(h1 * h1, axis=-1, keepdims=True)
        t = h1 * lax.rsqrt(ms + EPS) * g_ref[...]
        t_hi = t.astype(BF16)
        t_ref[...] = t_hi
        t_lo = (t - t_hi.astype(F32)).astype(BF16)
        logits = (jnp.dot(t_hi, wrh_ref[...], preferred_element_type=F32)
                  + jnp.dot(t_hi, wrl_ref[...], preferred_element_type=F32)
                  + jnp.dot(t_lo, wrh_ref[...], preferred_element_type=F32)) + br_ref[...]
        gate_ref[...] = _route(logits)


def _merge(h, ya, yb, wga, wgb, wa, wb, wo, x, g_ffn, wr_hi, wr_lo, br, tm, tn):
    t, d = h.shape
    row = lambda i, n: (i, 0)
    return pl.pallas_call(
        _merge_kernel,
        grid=(t // tm, d // tn),
        in_specs=[pl.BlockSpec((tm, d), row),
                  pl.BlockSpec((tm, A_WIDTH), row),
                  pl.BlockSpec((tm, B_WIDTH), row),
                  pl.BlockSpec((d, tn), lambda i, n: (0, n)),
                  pl.BlockSpec((d, tn), lambda i, n: (0, n)),
                  pl.BlockSpec((A_WIDTH, tn), lambda i, n: (0, n)),
                  pl.BlockSpec((B_WIDTH, tn), lambda i, n: (0, n)),
                  pl.BlockSpec((tn, d), lambda i, n: (n, 0)),
                  pl.BlockSpec((tm, d), row),
                  pl.BlockSpec((1, d), lambda i, n: (0, 0)),
                  pl.BlockSpec((d, LANES), lambda i, n: (0, 0)),
                  pl.BlockSpec((d, LANES), lambda i, n: (0, 0)),
                  pl.BlockSpec((1, LANES), lambda i, n: (0, 0))],
        out_specs=[pl.BlockSpec((tm, d), row), pl.BlockSpec((tm, d), row), pl.BlockSpec((tm, LANES), row)],
        out_shape=[jax.ShapeDtypeStruct((t, d), F32), jax.ShapeDtypeStruct((t, d), BF16),
                   jax.ShapeDtypeStruct((t, LANES), F32)],
        scratch_shapes=[pltpu.VMEM((tm, d), F32)],
        compiler_params=_cparams(("parallel", "arbitrary")),
        name="merge_out_router",
    )(h, ya, yb, wga, wgb, wa, wb, wo, x, g_ffn.reshape(1, d), wr_hi, wr_lo, br)


def _moe_kernel(t_ref, gate_ref, wg_ref, wu_ref, wd_ref, h1_ref, gf_ref, o_ref, acc_ref):
    e = pl.program_id(1)

    @pl.when(e == 0)
    def _():
        acc_ref[...] = jnp.zeros(acc_ref.shape, F32)

    t = t_ref[...]
    gates = gate_ref[...]
    lane = lax.broadcasted_iota(jnp.int32, gates.shape, 1)
    g_e = jnp.sum(jnp.where(lane == RT_E0 + e, gates, 0.0), axis=1, keepdims=True)
    hg = jnp.dot(t, wg_ref[0], preferred_element_type=F32)
    hu = jnp.dot(t, wu_ref[0], preferred_element_type=F32)
    act = jax.nn.silu(hg) * hu * g_e
    acc_ref[...] += jnp.dot(act.astype(BF16), wd_ref[0], preferred_element_type=F32)

    @pl.when(e == pl.num_programs(1) - 1)
    def _():
        h2 = h1_ref[...] + acc_ref[...]
        ms = jnp.mean(h2 * h2, axis=-1, keepdims=True)
        o_ref[...] = h2 * lax.rsqrt(ms + EPS) * gf_ref[...]


def _moe(t_bf, gates, wg, wu, wd, h1, g_final, tm):
    t, d = t_bf.shape
    f = wg.shape[2]
    row = lambda i, e: (i, 0)
    return pl.pallas_call(
        _moe_kernel,
        grid=(t // tm, N_EXPERTS),
        in_specs=[pl.BlockSpec((tm, d), row),
                  pl.BlockSpec((tm, LANES), row),
                  pl.BlockSpec((1, d, f), lambda i, e: (e, 0, 0)),
                  pl.BlockSpec((1, d, f), lambda i, e: (e, 0, 0)),
                  pl.BlockSpec((1, f, d), lambda i, e: (e, 0, 0)),
                  pl.BlockSpec((tm, d), row),
                  pl.BlockSpec((1, d), lambda i, e: (0, 0))],
        out_specs=pl.BlockSpec((tm, d), row),
        out_shape=jax.ShapeDtypeStruct((t, d), F32),
        scratch_shapes=[pltpu.VMEM((tm, d), F32)],
        compiler_params=_cparams(("parallel", "arbitrary")),
        name="moe_final",
    )(t_bf, gates, wg, wu, wd, h1, g_final.reshape(1, d))


def _pick(n, pref):
    return pref if n % pref == 0 else n


def _pad_rows(a, rows):
    return jnp.pad(a, ((0, rows - a.shape[0]), (0, 0)))


def _layer(x, meta, g_mix, w_in, g_kv, w_kv_up, g_idx_k, b_f, w_branch_a, w_branch_b, w_out,
           g_ffn, w_group, b_group, w_expert, b_expert, w_gate_e, w_up_e, w_down_e, g_final):
    t, d = x.shape
    n_all = t + N_META
    topk = min(TOPK_MAX, n_all // 4)

    o = [0]
    for wdt in (A_WIDTH, A_KV_RANK, IDX_HEADS * IDX_DIM, IDX_DIM, IDX_HEADS, B_WIDTH, B_WIDTH, B_WIDTH,
                B_HEADS, D_MODEL, D_MODEL):
        o.append(o[-1] + wdt)
    wb = w_in.astype(BF16)
    w_qa, w_ckv, w_qi = wb[:, o[0]:o[1]], wb[:, o[1]:o[2]], wb[:, o[2]:o[3]]
    w_small = jnp.concatenate(
        [wb[:, o[3]:o[5]], wb[:, o[8]:o[9]], jnp.zeros((d, LANES - (o[5] - o[3]) - B_HEADS), BF16)], axis=1)
    w_b = wb[:, o[5]:o[8]]
    w_ga, w_gb = wb[:, o[9]:o[10]], wb[:, o[10]:o[11]]
    w_up_bf = w_kv_up.astype(BF16)
    g_idx_pad = jnp.zeros((1, LANES), F32).at[0, SM_KI:SM_KI + IDX_DIM].set(g_idx_k)
    bf_pad = jnp.zeros((1, LANES), F32).at[0, SM_FB:SM_FB + B_HEADS].set(b_f)

    pos_meta = jnp.arange(N_META)
    pos_real = jnp.arange(N_META, n_all)
    cos64_m, sin64_m = _rope_tables(pos_meta, HEAD_DIM // 2)
    cos64_r, sin64_r = _rope_tables(pos_real, HEAD_DIM // 2)
    cos32_m, sin32_m = _rope_tables(pos_meta, IDX_DIM // 2)
    cos32_r, sin32_r = _rope_tables(pos_real, IDX_DIM // 2)

    tm = _pick(t, 512)

    hm = _rmsnorm(meta, g_mix, N_META)
    ka_m, va_m = _proj_kv(hm, w_ckv, g_kv, w_up_bf, cos64_m, sin64_m, N_META)
    small_m = _proj_small(hm, w_small, g_idx_pad, bf_pad, jnp.zeros((8, LANES), F32), cos32_m, sin32_m, N_META)
    b_m = _proj_plain(hm, w_b, N_META, 1024)
    c_carry = jnp.broadcast_to(small_m[N_META - 1:N_META, :], (8, LANES))

    h = _rmsnorm(x, g_mix, tm)
    qa = _proj_rope(h, w_qa, cos64_r, sin64_r, HEAD_DIM // 2, tm, 1024)
    ka_r, va_r = _proj_kv(h, w_ckv, g_kv, w_up_bf, cos64_r, sin64_r, tm)
    qi_hm = _proj_rope(h, w_qi, cos32_r, sin32_r, IDX_DIM // 2, tm, IDX_HEADS * IDX_DIM, head_major=True)
    small_r = _proj_small(h, w_small, g_idx_pad, bf_pad, c_carry, cos32_r, sin32_r, tm)
    b_r = _proj_plain(h, w_b, tm, 1024)

    ki_real = small_r[:, SM_KI:SM_KI + IDX_DIM].astype(BF16)
    ki_meta = _pad_rows(small_m[:, SM_KI:SM_KI + IDX_DIM], META_PAD).astype(BF16)
    wt = small_r[:, SM_WI:SM_WI + IDX_HEADS].T
    cq = small_r[:, SM_FB:SM_FB + B_HEADS]
    ct_meta = _pad_rows(small_m[:, SM_FB:SM_FB + B_HEADS], META_PAD).T

    bias_real, bias_meta = _select_bias(qi_hm, ki_real, ki_meta, wt, topk, _pick(t, 256))
    ya = _attention((qa, 0, ka_r, 0, va_r, 0),
                    (_pad_rows(ka_m, META_PAD), 0, _pad_rows(va_m, META_PAD), 0),
                    (bias_real, bias_meta), fox=False, t_blk=_pick(t, 512))
    b_m_pad = _pad_rows(b_m, META_PAD)
    yb = _attention((b_r, 0, b_r, 1, b_r, 2), (b_m_pad, 1, b_m_pad, 2),
                    (cq, cq.T, ct_meta), fox=True, t_blk=_pick(t, 512))

    w_r = jnp.concatenate([w_group, w_expert, jnp.zeros((d, LANES - N_GROUPS - N_EXPERTS), F32)], axis=1)
    wr_hi = w_r.astype(BF16)
    wr_lo = (w_r - wr_hi.astype(F32)).astype(BF16)
    b_r_pad = jnp.concatenate([b_group, b_expert, jnp.zeros((LANES - N_GROUPS - N_EXPERTS,), F32)]).reshape(1, LANES)
    h1, t_bf, gates = _merge(h, ya, yb, w_ga, w_gb, w_branch_a.astype(BF16), w_branch_b.astype(BF16),
                             w_out.astype(BF16), x, g_ffn, wr_hi, wr_lo, b_r_pad, tm, 256)

    return _moe(t_bf, gates, w_gate_e.astype(BF16), w_up_e.astype(BF16), w_down_e.astype(BF16), h1, g_final, tm)


def kernel(x, meta_tokens, g_mix, w_in, g_kv, w_kv_up, g_idx_k, b_f, w_branch_a, w_branch_b, w_out, g_ffn,
           w_group, b_group, w_expert, b_expert, w_gate_e, w_up_e, w_down_e, g_final):
    assert x.shape[0] == 1 and g_mix.shape[0] == 1, "single batch, single layer"
    out = _layer(x[0], meta_tokens.astype(x.dtype), g_mix[0], w_in[0], g_kv[0], w_kv_up[0], g_idx_k[0], b_f[0],
                 w_branch_a[0], w_branch_b[0], w_out[0], g_ffn[0], w_group[0], b_group[0], w_expert[0],
                 b_expert[0], w_gate_e[0], w_up_e[0], w_down_e[0], g_final)
    return out[None]
```

```python
import functools

import jax
import jax.numpy as jnp
from jax import lax
from jax.experimental import pallas as pl
from jax.experimental.pallas import tpu as pltpu

F32 = jnp.float32
BF16 = jnp.bfloat16

D_MODEL = 2048
N_META = 16
HEAD_DIM = 128
ROPE_THETA = 10000.0
EPS = 1e-6
NEG = -1e30
A_HEADS = 8
A_WIDTH = A_HEADS * HEAD_DIM
A_KV_RANK = 512
IDX_HEADS = 16
IDX_DIM = 64
TOPK_MAX = 256
B_HEADS = 8
B_WIDTH = B_HEADS * HEAD_DIM
N_GROUPS = 4
EXPERTS_PER_GROUP = 4
N_EXPERTS = N_GROUPS * EXPERTS_PER_GROUP
D_FF_EXPERT = 512

LANES = 128
META_PAD = LANES
VMEM_LIMIT = 56 * 1024 * 1024
LOG2E = 1.4426950408889634
NT_DIMS = (((1,), (1,)), ((), ()))

SM_KI = 0
SM_WI = IDX_DIM
SM_FB = IDX_DIM + IDX_HEADS
RT_E0 = N_GROUPS


def _cparams(sem):
    return pltpu.CompilerParams(dimension_semantics=sem, vmem_limit_bytes=VMEM_LIMIT)


def _rms_kernel(x_ref, g_ref, o_ref):
    x = x_ref[...]
    ms = jnp.mean(x * x, axis=-1, keepdims=True)
    o_ref[...] = (x * lax.rsqrt(ms + EPS) * g_ref[...]).astype(o_ref.dtype)


def _rmsnorm(x, g, tm):
    n, d = x.shape
    return pl.pallas_call(
        _rms_kernel,
        grid=(n // tm,),
        in_specs=[pl.BlockSpec((tm, d), lambda i: (i, 0)), pl.BlockSpec((1, d), lambda i: (0, 0))],
        out_specs=pl.BlockSpec((tm, d), lambda i: (i, 0)),
        out_shape=jax.ShapeDtypeStruct((n, d), BF16),
        compiler_params=_cparams(("parallel",)),
        name="rmsnorm",
    )(x, g.reshape(1, d))


def _rope_tile(y, cos, sin_signed, half):
    if 2 * half == LANES:
        partner = pltpu.roll(y, half, axis=1)
    else:
        lane = lax.broadcasted_iota(jnp.int32, y.shape, 1)
        first = (lane % (2 * half)) < half
        partner = jnp.where(first, pltpu.roll(y, LANES - half, axis=1), pltpu.roll(y, half, axis=1))
    return y * cos + partner * sin_signed


def _rope_tables(pos, half):
    inv_freq = ROPE_THETA ** (-jnp.arange(half, dtype=F32) / half)
    ang = pos.astype(F32)[:, None] * inv_freq[None, :]
    cos, sin = jnp.cos(ang), jnp.sin(ang)
    reps = LANES // (2 * half)
    cos_t = jnp.tile(jnp.concatenate([cos, cos], axis=1), (1, reps))
    sin_t = jnp.tile(jnp.concatenate([-sin, sin], axis=1), (1, reps))
    return cos_t, sin_t


def _proj_plain_kernel(h_ref, w_ref, o_ref):
    o_ref[...] = jnp.dot(h_ref[...], w_ref[...], preferred_element_type=F32).astype(o_ref.dtype)


def _proj_plain(h, w, tm, tn):
    n, d = h.shape
    nc = w.shape[1]
    return pl.pallas_call(
        _proj_plain_kernel,
        grid=(n // tm, nc // tn),
        in_specs=[pl.BlockSpec((tm, d), lambda i, j: (i, 0)), pl.BlockSpec((d, tn), lambda i, j: (0, j))],
        out_specs=pl.BlockSpec((tm, tn), lambda i, j: (i, j)),
        out_shape=jax.ShapeDtypeStruct((n, nc), BF16),
        compiler_params=_cparams(("parallel", "parallel")),
        name="proj_plain",
    )(h, w)


def _proj_rope_kernel(h_ref, w_ref, cos_ref, sin_ref, o_ref, *, half, head_major):
    y = jnp.dot(h_ref[...], w_ref[...], preferred_element_type=F32)
    cos, sin = cos_ref[...], sin_ref[...]
    per_tile = LANES // (2 * half)
    for c in range(y.shape[1] // LANES):
        sl = slice(c * LANES, (c + 1) * LANES)
        roped = _rope_tile(y[:, sl], cos, sin, half).astype(o_ref.dtype)
        if head_major:
            for k in range(per_tile):
                o_ref[c * per_tile + k] = roped[:, k * 2 * half:(k + 1) * 2 * half]
        else:
            o_ref[:, sl] = roped


def _proj_rope(h, w, cos, sin, half, tm, tn, head_major=False):
    n, d = h.shape
    nc = w.shape[1]
    if head_major:
        assert tn == nc
        heads = nc // (2 * half)
        out_spec = pl.BlockSpec((heads, tm, 2 * half), lambda i, j: (0, i, 0))
        out_shape = jax.ShapeDtypeStruct((heads, n, 2 * half), BF16)
    else:
        out_spec = pl.BlockSpec((tm, tn), lambda i, j: (i, j))
        out_shape = jax.ShapeDtypeStruct((n, nc), BF16)
    return pl.pallas_call(
        functools.partial(_proj_rope_kernel, half=half, head_major=head_major),
        grid=(n // tm, nc // tn),
        in_specs=[pl.BlockSpec((tm, d), lambda i, j: (i, 0)), pl.BlockSpec((d, tn), lambda i, j: (0, j)),
                  pl.BlockSpec((tm, LANES), lambda i, j: (i, 0)), pl.BlockSpec((tm, LANES), lambda i, j: (i, 0))],
        out_specs=out_spec,
        out_shape=out_shape,
        compiler_params=_cparams(("parallel", "parallel")),
        name="proj_rope",
    )(h, w, cos, sin)


def _kv_kernel(h_ref, wc_ref, g_ref, wup_ref, cos_ref, sin_ref, ka_ref, va_ref):
    c = jnp.dot(h_ref[...], wc_ref[...], preferred_element_type=F32)
    ms = jnp.mean(c * c, axis=-1, keepdims=True)
    cn = (c * lax.rsqrt(ms + EPS) * g_ref[...]).astype(BF16)
    kv = jnp.dot(cn, wup_ref[...], preferred_element_type=F32)
    cos, sin = cos_ref[...], sin_ref[...]
    for hd in range(A_HEADS):
        sl = slice(hd * HEAD_DIM, (hd + 1) * HEAD_DIM)
        ka_ref[:, sl] = _rope_tile(kv[:, sl], cos, sin, HEAD_DIM // 2).astype(ka_ref.dtype)
    va_ref[...] = kv[:, A_WIDTH:].astype(va_ref.dtype)


def _proj_kv(h, wc, g_kv, wup, cos, sin, tm):
    n, d = h.shape
    return pl.pallas_call(
        _kv_kernel,
        grid=(n // tm,),
        in_specs=[pl.BlockSpec((tm, d), lambda i: (i, 0)),
                  pl.BlockSpec((d, A_KV_RANK), lambda i: (0, 0)),
                  pl.BlockSpec((1, A_KV_RANK), lambda i: (0, 0)),
                  pl.BlockSpec((A_KV_RANK, 2 * A_WIDTH), lambda i: (0, 0)),
                  pl.BlockSpec((tm, LANES), lambda i: (i, 0)),
                  pl.BlockSpec((tm, LANES), lambda i: (i, 0))],
        out_specs=[pl.BlockSpec((tm, A_WIDTH), lambda i: (i, 0)), pl.BlockSpec((tm, A_WIDTH), lambda i: (i, 0))],
        out_shape=[jax.ShapeDtypeStruct((n, A_WIDTH), BF16), jax.ShapeDtypeStruct((n, A_WIDTH), BF16)],
        compiler_params=_cparams(("parallel",)),
        name="proj_kv",
    )(h, wc, g_kv.reshape(1, A_KV_RANK), wup, cos, sin)


def _small_kernel(h_ref, w_ref, g_ref, bf_ref, c0_ref, cos_ref, sin_ref, o_ref, carry_ref):
    tm = h_ref.shape[0]

    @pl.when(pl.program_id(0) == 0)
    def _():
        carry_ref[...] = c0_ref[...]

    y = jnp.dot(h_ref[...], w_ref[...], preferred_element_type=F32)
    lane = lax.broadcasted_iota(jnp.int32, y.shape, 1)
    is_ki = lane < SM_WI
    ms = jnp.sum(jnp.where(is_ki, y * y, 0.0), axis=-1, keepdims=True) * (1.0 / IDX_DIM)
    ki = _rope_tile(y * lax.rsqrt(ms + EPS) * g_ref[...], cos_ref[...], sin_ref[...], IDX_DIM // 2)
    wi = y * (IDX_HEADS ** -0.5 * IDX_DIM ** -0.5)
    z = y + bf_ref[...]
    logf = -(jnp.maximum(-z, 0.0) + jnp.log(1.0 + jnp.exp(-jnp.abs(z))))
    hi = logf.astype(BF16)
    r1 = logf - hi.astype(F32)
    mid = r1.astype(BF16)
    lo = (r1 - mid.astype(F32)).astype(BF16)
    row = lax.broadcasted_iota(jnp.int32, (tm, tm), 0)
    col = lax.broadcasted_iota(jnp.int32, (tm, tm), 1)
    tri = (col <= row).astype(BF16)
    csum = (jnp.dot(tri, hi, preferred_element_type=F32) + jnp.dot(tri, mid, preferred_element_type=F32)
            + jnp.dot(tri, lo, preferred_element_type=F32)) + carry_ref[0:1, :]
    carry_ref[...] = jnp.broadcast_to(csum[tm - 1:tm, :], carry_ref.shape)
    o_ref[...] = jnp.where(is_ki, ki, jnp.where(lane < SM_FB, wi, csum))


def _proj_small(h, w, g_idx_pad, bf_pad, c0, cos, sin, tm):
    n, d = h.shape
    return pl.pallas_call(
        _small_kernel,
        grid=(n // tm,),
        in_specs=[pl.BlockSpec((tm, d), lambda i: (i, 0)),
                  pl.BlockSpec((d, LANES), lambda i: (0, 0)),
                  pl.BlockSpec((1, LANES), lambda i: (0, 0)),
                  pl.BlockSpec((1, LANES), lambda i: (0, 0)),
                  pl.BlockSpec((8, LANES), lambda i: (0, 0)),
                  pl.BlockSpec((tm, LANES), lambda i: (i, 0)),
                  pl.BlockSpec((tm, LANES), lambda i: (i, 0))],
        out_specs=pl.BlockSpec((tm, LANES), lambda i: (i, 0)),
        out_shape=jax.ShapeDtypeStruct((n, LANES), F32),
        scratch_shapes=[pltpu.VMEM((8, LANES), F32)],
        compiler_params=_cparams(("arbitrary",)),
        name="proj_small",
    )(h, w, g_idx_pad, bf_pad, c0, cos, sin)


COUNT_ROWS = 64


def _key_to_float(key):
    bits = key ^ ((key >> 31) & jnp.int32(0x7FFFFFFF))
    return lax.bitcast_convert_type(bits, F32)


def _select_kernel(qi_ref, kir_ref, kim_ref, wt_ref, or_ref, om_ref, s_ref, *, tq, topk):
    i = pl.program_id(0)
    n_real = or_ref.shape[1]
    n_chunks = n_real // tq
    krow = lax.broadcasted_iota(jnp.int32, (tq, tq), 0)
    qcol = lax.broadcasted_iota(jnp.int32, (tq, tq), 1)
    mrow = lax.broadcasted_iota(jnp.int32, (META_PAD, tq), 0)
    wt = wt_ref[...]

    def scores(kc):
        s = jnp.zeros((kc.shape[0], tq), F32)
        for h in range(IDX_HEADS):
            d = lax.dot_general(kc, qi_ref[h], NT_DIMS, preferred_element_type=F32)
            s = s + wt[h:h + 1, :] * jnp.maximum(d, 0.0)
        return s

    def causal(c):
        return krow + (c - i) * tq <= qcol

    def score_chunk(c, _):
        off = pl.multiple_of(c * tq, tq)
        s_ref[pl.ds(off, tq), :] = jnp.where(causal(c), scores(kir_ref[pl.ds(off, tq), :]), NEG)
        return 0

    lax.fori_loop(0, i + 1, score_chunk, 0)
    s_ref[n_real:, :] = jnp.where(mrow < N_META, scores(kim_ref[...]), NEG)

    def count(cmp, thr):
        def body(c, acc):
            off = c * tq
            for r in range(tq // COUNT_ROWS):
                blk = s_ref[pl.ds(pl.multiple_of(off + r * COUNT_ROWS, COUNT_ROWS), COUNT_ROWS), :]
                acc = acc + jnp.where(cmp(blk, thr), 1.0, 0.0)
            return acc

        acc = lax.fori_loop(0, i + 1, body, jnp.zeros((COUNT_ROWS, tq), F32))
        for r in range(META_PAD // COUNT_ROWS):
            blk = s_ref[n_real + r * COUNT_ROWS:n_real + (r + 1) * COUNT_ROWS, :]
            acc = acc + jnp.where(cmp(blk, thr), 1.0, 0.0)
        return jnp.sum(acc, axis=0, keepdims=True)

    kf = jnp.float32(topk)

    def bit_step(b, carry):
        t_key, cnt_t = carry
        cand = t_key + lax.shift_left(jnp.int32(1), 31 - b)
        cnt = count(lambda s, t: s >= t, _key_to_float(cand))
        take = cnt >= kf
        return jnp.where(take, cand, t_key), jnp.where(take, cnt, cnt_t)

    t0 = jnp.full((1, tq), jnp.iinfo(jnp.int32).min, jnp.int32)
    c0 = jnp.broadcast_to(((i + 1) * tq + META_PAD).astype(F32), (1, tq))
    t_key, cnt_t = lax.fori_loop(0, 32, bit_step, (t0, c0))
    thr = _key_to_float(t_key)

    def fill_chunk(c, _):
        or_ref[:, pl.ds(pl.multiple_of(c * tq, tq), tq)] = jnp.full((tq, tq), NEG, or_ref.dtype)
        return 0

    lax.fori_loop(i + 1, n_chunks, fill_chunk, 0)

    ambiguous = jnp.logical_and(cnt_t > kf, thr > NEG)
    any_amb = jnp.max(jnp.where(ambiguous, 1.0, 0.0)) > 0.0

    @pl.when(jnp.logical_not(any_amb))
    def _():
        def out_chunk(c, _):
            off = pl.multiple_of(c * tq, tq)
            sel = jnp.where(s_ref[pl.ds(off, tq), :] >= thr, 0.0, NEG)
            or_ref[:, pl.ds(off, tq)] = jnp.where(causal(c), sel, NEG).T.astype(or_ref.dtype)
            return 0

        lax.fori_loop(0, i + 1, out_chunk, 0)
        sel = jnp.where(s_ref[n_real:, :] >= thr, 0.0, NEG)
        om_ref[...] = jnp.where(mrow < N_META, sel, NEG).T.astype(om_ref.dtype)

    @pl.when(any_amb)
    def _():
        cnt_gt = count(lambda s, t: s > t, thr)
        quota = jnp.where(ambiguous, kf - cnt_gt, jnp.float32(3e38))

        def tie_select(s, valid, running):
            rows = s.shape[0]
            r = lax.broadcasted_iota(jnp.int32, (rows, rows), 0)
            c_ = lax.broadcasted_iota(jnp.int32, (rows, rows), 1)
            lower = (c_ <= r).astype(BF16)
            eq = jnp.where(s == thr, 1.0, 0.0)
            rank = jnp.dot(lower, eq.astype(BF16), preferred_element_type=F32) + running
            keep = jnp.where(s > thr, 0.0, jnp.where(jnp.logical_and(eq > 0.0, rank <= quota), 0.0, NEG))
            return jnp.where(valid, keep, NEG), running + jnp.sum(eq, axis=0, keepdims=True)

        bias_m, running = tie_select(s_ref[n_real:, :], mrow < N_META, jnp.zeros((1, tq), F32))
        om_ref[...] = bias_m.T.astype(om_ref.dtype)

        def out_chunk(c, running):
            off = pl.multiple_of(c * tq, tq)
            bias, running = tie_select(s_ref[pl.ds(off, tq), :], causal(c), running)
            or_ref[:, pl.ds(off, tq)] = bias.T.astype(or_ref.dtype)
            return running

        lax.fori_loop(0, i + 1, out_chunk, running)


def _select_bias(qi_hm, ki_real, ki_meta, wt, topk, tq):
    _, t, _ = qi_hm.shape
    return pl.pallas_call(
        functools.partial(_select_kernel, tq=tq, topk=topk),
        grid=(t // tq,),
        in_specs=[pl.BlockSpec((IDX_HEADS, tq, IDX_DIM), lambda i: (0, i, 0)),
                  pl.BlockSpec((t, IDX_DIM), lambda i: (0, 0)),
                  pl.BlockSpec((META_PAD, IDX_DIM), lambda i: (0, 0)),
                  pl.BlockSpec((IDX_HEADS, tq), lambda i: (0, i))],
        out_specs=[pl.BlockSpec((tq, t), lambda i: (i, 0)), pl.BlockSpec((tq, META_PAD), lambda i: (i, 0))],
        out_shape=[jax.ShapeDtypeStruct((t, t), BF16), jax.ShapeDtypeStruct((t, META_PAD), BF16)],
        scratch_shapes=[pltpu.VMEM((t + META_PAD, tq), F32)],
        compiler_params=_cparams(("parallel",)),
        name="select_bias",
    )(qi_hm, ki_real, ki_meta, wt)


def _attn_kernel(*refs, t, fox):
    if fox:
        (qi_ref, kj_ref, q_ref, k_ref, v_ref, km_ref, vm_ref, cq_ref, ck_ref, ckm_ref,
         o_ref, m_ref, l_ref, acc_ref, r_ref) = refs
    else:
        (qi_ref, kj_ref, q_ref, k_ref, v_ref, km_ref, vm_ref, b_ref, bm_ref,
         o_ref, m_ref, l_ref, acc_ref) = refs
    step = pl.program_id(0)
    i = qi_ref[step]
    j = kj_ref[step]
    nheads = q_ref.shape[1] // HEAD_DIM
    c1 = HEAD_DIM ** -0.5 * LOG2E

    def block(k_blk_ref, v_blk_ref, bias_fn):
        width = k_blk_ref.shape[0]
        for h in range(nheads):
            sl = slice(h * HEAD_DIM, (h + 1) * HEAD_DIM)
            s = lax.dot_general(q_ref[:, sl], k_blk_ref[:, sl], NT_DIMS, preferred_element_type=F32)
            u = bias_fn(s * c1, h)
            m_prev = m_ref[h]
            mx = jnp.max(u, axis=1, keepdims=True)
            if fox:
                r = r_ref[h]
                m_next = jnp.maximum(m_prev, mx + r)
                sub = m_next - r
            else:
                m_next = jnp.maximum(m_prev, mx)
                sub = m_next
            alpha = jnp.exp2(m_prev - m_next)
            ps = [jnp.exp2(u[:, c * LANES:(c + 1) * LANES] - sub) for c in range(width // LANES)]
            psum = ps[0]
            for pc in ps[1:]:
                psum = psum + pc
            p = ps[0] if len(ps) == 1 else jnp.concatenate(ps, axis=1)
            l_ref[h] = alpha * l_ref[h] + psum
            m_ref[h] = m_next
            acc_ref[h] = alpha * acc_ref[h] + jnp.dot(p.astype(BF16), v_blk_ref[:, sl],
                                                      preferred_element_type=F32)

    @pl.when(j == 0)
    def _():
        m_ref[...] = jnp.full(m_ref.shape, NEG, F32)
        l_ref[...] = jnp.zeros(l_ref.shape, F32)
        acc_ref[...] = jnp.zeros(acc_ref.shape, F32)
        mcol = lax.broadcasted_iota(jnp.int32, (t, META_PAD), 1)
        if fox:
            for h in range(nheads):
                r_ref[h] = jnp.broadcast_to(cq_ref[:, h:h + 1] * LOG2E, (t, LANES))

            def bias_fn(u, h):
                return jnp.where(mcol < N_META, u - ckm_ref[h:h + 1, :] * LOG2E, NEG)
        else:
            def bias_fn(u, h):
                return u + bm_ref[...].astype(F32)
        block(km_ref, vm_ref, bias_fn)

    if fox:
        @pl.when(j < i)
        def _():
            block(k_ref, v_ref, lambda u, h: u - ck_ref[h:h + 1, :] * LOG2E)

        @pl.when(j == i)
        def _():
            row = lax.broadcasted_iota(jnp.int32, (t, t), 0)
            col = lax.broadcasted_iota(jnp.int32, (t, t), 1)
            block(k_ref, v_ref, lambda u, h: jnp.where(col <= row, u - ck_ref[h:h + 1, :] * LOG2E, NEG))
    else:
        block(k_ref, v_ref, lambda u, h: u + b_ref[...].astype(F32))

    @pl.when(j == i)
    def _():
        for h in range(nheads):
            sl = slice(h * HEAD_DIM, (h + 1) * HEAD_DIM)
            o_ref[:, sl] = (acc_ref[h] / jnp.sum(l_ref[h], axis=1, keepdims=True)).astype(o_ref.dtype)


def _attention(qkv, meta_kv, extra, *, fox, t_blk):
    q_arr, q_col, k_arr, k_col, v_arr, v_col = qkv
    km_arr, km_col, vm_arr, vm_col = meta_kv
    t = q_arr.shape[0]
    w = A_WIDTH
    nheads = w // HEAD_DIM
    nq = t // t_blk
    qi = jnp.asarray([i for i in range(nq) for _ in range(i + 1)], jnp.int32)
    kj = jnp.asarray([j for i in range(nq) for j in range(i + 1)], jnp.int32)

    in_specs = [pl.BlockSpec((t_blk, w), lambda s, qi, kj: (qi[s], q_col)),
                pl.BlockSpec((t_blk, w), lambda s, qi, kj: (kj[s], k_col)),
                pl.BlockSpec((t_blk, w), lambda s, qi, kj: (kj[s], v_col)),
                pl.BlockSpec((META_PAD, w), lambda s, qi, kj: (0, km_col)),
                pl.BlockSpec((META_PAD, w), lambda s, qi, kj: (0, vm_col))]
    scratch = [pltpu.VMEM((nheads, t_blk, LANES), F32), pltpu.VMEM((nheads, t_blk, LANES), F32),
               pltpu.VMEM((nheads, t_blk, HEAD_DIM), F32)]
    args = (q_arr, k_arr, v_arr, km_arr, vm_arr)
    if fox:
        cq, ct_real, ct_meta = extra
        in_specs += [pl.BlockSpec((t_blk, nheads), lambda s, qi, kj: (qi[s], 0)),
                     pl.BlockSpec((nheads, t_blk), lambda s, qi, kj: (0, kj[s])),
                     pl.BlockSpec((nheads, META_PAD), lambda s, qi, kj: (0, 0))]
        args += (cq, ct_real, ct_meta)
        scratch.append(pltpu.VMEM((nheads, t_blk, LANES), F32))
    else:
        bias_real, bias_meta = extra
        in_specs += [pl.BlockSpec((t_blk, t_blk), lambda s, qi, kj: (qi[s], kj[s])),
                     pl.BlockSpec((t_blk, META_PAD), lambda s, qi, kj: (qi[s], 0))]
        args += (bias_real, bias_meta)
    return pl.pallas_call(
        functools.partial(_attn_kernel, t=t_blk, fox=fox),
        grid_spec=pltpu.PrefetchScalarGridSpec(
            num_scalar_prefetch=2,
            grid=(int(qi.shape[0]),),
            in_specs=in_specs,
            out_specs=pl.BlockSpec((t_blk, w), lambda s, qi, kj: (qi[s], 0)),
            scratch_shapes=scratch),
        out_shape=jax.ShapeDtypeStruct((t, w), BF16),
        compiler_params=_cparams(("arbitrary",)),
        name="attn_fox" if fox else "attn_dsa",
    )(qi, kj, *args)


def _route(logits):
    lane = lax.broadcasted_iota(jnp.int32, logits.shape, 1).astype(F32)
    big = jnp.float32(LANES)
    ninf = -jnp.inf
    gl = jnp.where(lane < N_GROUPS, logits, ninf)
    gmax = jnp.max(gl, axis=1, keepdims=True)
    g_sel = jnp.min(jnp.where(gl == gmax, lane, big), axis=1, keepdims=True)
    p_group = 1.0 / jnp.sum(jnp.exp(gl - gmax), axis=1, keepdims=True)
    e_lo = RT_E0 + g_sel * EXPERTS_PER_GROUP
    in_grp = jnp.logical_and(lane >= e_lo, lane < e_lo + EXPERTS_PER_GROUP)
    el = jnp.where(in_grp, logits, ninf)
    emax = jnp.max(el, axis=1, keepdims=True)
    ex = jnp.exp(el - emax)
    prob = ex / jnp.sum(ex, axis=1, keepdims=True)
    prob = jnp.where(in_grp, prob, -1.0)
    p1 = jnp.max(prob, axis=1, keepdims=True)
    i1 = jnp.min(jnp.where(prob == p1, lane, big), axis=1, keepdims=True)
    rest = jnp.where(lane == i1, -1.0, prob)
    p2 = jnp.max(rest, axis=1, keepdims=True)
    i2 = jnp.min(jnp.where(rest == p2, lane, big), axis=1, keepdims=True)
    tot = p1 + p2
    w1 = p_group * (p1 / tot)
    w2 = p_group * (p2 / tot)
    return jnp.where(lane == i1, w1, jnp.where(lane == i2, w2, 0.0)), g_sel


def _merge_kernel(h_ref, ya_ref, yb_ref, wga_ref, wgb_ref, wa_ref, wb_ref, wo_ref, x_ref, g_ref,
                  wrh_ref, wrl_ref, br_ref, h1_ref, xr_ref, plan_ref, acc_ref, cnt_ref):
    n = pl.program_id(1)

    @pl.when(jnp.logical_and(pl.program_id(0) == 0, n == 0))
    def _():
        cnt_ref[...] = jnp.zeros(cnt_ref.shape, F32)

    @pl.when(n == 0)
    def _():
        acc_ref[...] = jnp.zeros(acc_ref.shape, F32)

    h = h_ref[...]
    ga = jnp.dot(h, wga_ref[...], preferred_element_type=F32)
    gb = jnp.dot(h, wgb_ref[...], preferred_element_type=F32)
    pa = jnp.dot(ya_ref[...], wa_ref[...], preferred_element_type=F32)
    pb = jnp.dot(yb_ref[...], wb_ref[...], preferred_element_type=F32)
    merged = jax.nn.sigmoid(ga) * pa + jax.nn.sigmoid(gb) * pb
    acc_ref[...] += jnp.dot(merged.astype(BF16), wo_ref[...], preferred_element_type=F32)

    @pl.when(n == pl.num_programs(1) - 1)
    def _():
        h1 = x_ref[...] + acc_ref[...]
        h1_ref[...] = h1
        ms = jnp.mean(h1 * h1, axis=-1, keepdims=True)
        t = h1 * lax.rsqrt(ms + EPS) * g_ref[...]
        t_hi = t.astype(BF16)
        t_lo = (t - t_hi.astype(F32)).astype(BF16)
        logits = (jnp.dot(t_hi, wrh_ref[...], preferred_element_type=F32)
                  + jnp.dot(t_hi, wrl_ref[...], preferred_element_type=F32)
                  + jnp.dot(t_lo, wrh_ref[...], preferred_element_type=F32)) + br_ref[...]
        gates, g_sel = _route(logits)
        d = t.shape[1]
        xr_ref[:, :d] = t
        xr_ref[:, d:] = gates
        tm = t.shape[0]
        lane = lax.broadcasted_iota(jnp.int32, gates.shape, 1).astype(F32)
        onehot = jnp.where(lane == g_sel, 1.0, 0.0).astype(BF16)
        rr = lax.broadcasted_iota(jnp.int32, (tm, tm), 0)
        cc = lax.broadcasted_iota(jnp.int32, (tm, tm), 1)
        cum = jnp.dot((cc <= rr).astype(BF16), onehot, preferred_element_type=F32) + cnt_ref[0:1, :]
        cnt_ref[...] = jnp.broadcast_to(cum[tm - 1:tm, :], cnt_ref.shape)
        plan_ref[...] = jnp.where(lane < N_GROUPS, cum, jnp.where(lane == N_GROUPS, g_sel, 0.0))


def _merge(h, ya, yb, wga, wgb, wa, wb, wo, x, g_ffn, wr_hi, wr_lo, br, tm, tn):
    t, d = h.shape
    row = lambda i, n: (i, 0)
    return pl.pallas_call(
        _merge_kernel,
        grid=(t // tm, d // tn),
        in_specs=[pl.BlockSpec((tm, d), row),
                  pl.BlockSpec((tm, A_WIDTH), row),
                  pl.BlockSpec((tm, B_WIDTH), row),
                  pl.BlockSpec((d, tn), lambda i, n: (0, n)),
                  pl.BlockSpec((d, tn), lambda i, n: (0, n)),
                  pl.BlockSpec((A_WIDTH, tn), lambda i, n: (0, n)),
                  pl.BlockSpec((B_WIDTH, tn), lambda i, n: (0, n)),
                  pl.BlockSpec((tn, d), lambda i, n: (n, 0)),
                  pl.BlockSpec((tm, d), row),
                  pl.BlockSpec((1, d), lambda i, n: (0, 0)),
                  pl.BlockSpec((d, LANES), lambda i, n: (0, 0)),
                  pl.BlockSpec((d, LANES), lambda i, n: (0, 0)),
                  pl.BlockSpec((1, LANES), lambda i, n: (0, 0))],
        out_specs=[pl.BlockSpec((tm, d), row), pl.BlockSpec((tm, d + LANES), row),
                   pl.BlockSpec((tm, LANES), row)],
        out_shape=[jax.ShapeDtypeStruct((t, d), F32), jax.ShapeDtypeStruct((t, d + LANES), F32),
                   jax.ShapeDtypeStruct((t, LANES), F32)],
        scratch_shapes=[pltpu.VMEM((tm, d), F32), pltpu.VMEM((8, LANES), F32)],
        compiler_params=_cparams(("arbitrary", "arbitrary")),
        name="merge_out_router",
    )(h, ya, yb, wga, wgb, wa, wb, wo, x, g_ffn.reshape(1, d), wr_hi, wr_lo, br)


def _row_copy(src_ref, src_row, dst_ref, dst_row, sem):
    return pltpu.make_async_copy(src_ref.at[pl.ds(src_row, 1), :], dst_ref.at[pl.ds(dst_row, 1), :], sem)


def _dispatch_kernel(pos_ref, x_ref, init_ref, xs_ref, sem):
    del init_ref
    i = pl.program_id(0)
    tm = x_ref.shape[0]

    def issue(r, _):
        _row_copy(x_ref, r, xs_ref, pos_ref[i * tm + r], sem).start()
        return 0

    lax.fori_loop(0, tm, issue, 0)

    def drain(r, _):
        _row_copy(x_ref, r, xs_ref, pos_ref[i * tm + r], sem).wait()
        return 0

    lax.fori_loop(0, tm, drain, 0)


def _dispatch(pos, xr, n_slots, tm):
    t, w = xr.shape
    return pl.pallas_call(
        _dispatch_kernel,
        grid_spec=pltpu.PrefetchScalarGridSpec(
            num_scalar_prefetch=1,
            grid=(t // tm,),
            in_specs=[pl.BlockSpec((tm, w), lambda i, pos: (i, 0)),
                      pl.BlockSpec(memory_space=pl.ANY)],
            out_specs=pl.BlockSpec(memory_space=pl.ANY),
            scratch_shapes=[pltpu.SemaphoreType.DMA]),
        out_shape=jax.ShapeDtypeStruct((n_slots, w), xr.dtype),
        input_output_aliases={2: 0},
        compiler_params=_cparams(("arbitrary",)),
        name="moe_dispatch",
    )(pos, xr, jnp.zeros((n_slots, w), xr.dtype))


def _group_ffn_kernel(tg_ref, tv_ref, x_ref, wg_ref, wu_ref, wd_ref, o_ref, acc_ref, xb_ref):
    j = pl.program_id(0)
    k = pl.program_id(1)
    d = o_ref.shape[1]

    @pl.when(k == 0)
    def _():
        acc_ref[...] = jnp.zeros(acc_ref.shape, F32)
        xb_ref[...] = x_ref[:, :d].astype(BF16)

    @pl.when(tv_ref[j] > 0)
    def _():
        gates = x_ref[:, d:]
        lane = lax.broadcasted_iota(jnp.int32, gates.shape, 1)
        e_lane = RT_E0 + tg_ref[j] * EXPERTS_PER_GROUP + k
        g_e = jnp.sum(jnp.where(lane == e_lane, gates, 0.0), axis=1, keepdims=True)
        xb = xb_ref[...]
        hg = jnp.dot(xb, wg_ref[0], preferred_element_type=F32)
        hu = jnp.dot(xb, wu_ref[0], preferred_element_type=F32)
        act = jax.nn.silu(hg) * hu * g_e
        acc_ref[...] += jnp.dot(act.astype(BF16), wd_ref[0], preferred_element_type=F32)

    @pl.when(k == pl.num_programs(1) - 1)
    def _():
        o_ref[...] = acc_ref[...]


def _group_ffn(tile_group, tile_valid, xs, wg, wu, wd, ts):
    n_slots, w = xs.shape
    d = w - LANES
    f = wg.shape[2]
    expert = lambda j, k, tg, tv: (tg[j] * EXPERTS_PER_GROUP + k, 0, 0)
    return pl.pallas_call(
        _group_ffn_kernel,
        grid_spec=pltpu.PrefetchScalarGridSpec(
            num_scalar_prefetch=2,
            grid=(n_slots // ts, EXPERTS_PER_GROUP),
            in_specs=[pl.BlockSpec((ts, w), lambda j, k, tg, tv: (j, 0)),
                      pl.BlockSpec((1, d, f), expert),
                      pl.BlockSpec((1, d, f), expert),
                      pl.BlockSpec((1, f, d), expert)],
            out_specs=pl.BlockSpec((ts, d), lambda j, k, tg, tv: (j, 0)),
            scratch_shapes=[pltpu.VMEM((ts, d), F32), pltpu.VMEM((ts, d), BF16)]),
        out_shape=jax.ShapeDtypeStruct((n_slots, d), F32),
        compiler_params=_cparams(("arbitrary", "arbitrary")),
        name="moe_group_ffn",
    )(tile_group, tile_valid, xs, wg, wu, wd)


def _combine_kernel(pos_ref, ys_ref, h1_ref, gf_ref, o_ref, ybuf_ref, sem):
    i = pl.program_id(0)
    tm = h1_ref.shape[0]

    def issue(r, _):
        _row_copy(ys_ref, pos_ref[i * tm + r], ybuf_ref, r, sem).start()
        return 0

    lax.fori_loop(0, tm, issue, 0)

    def drain(r, _):
        _row_copy(ys_ref, pos_ref[i * tm + r], ybuf_ref, r, sem).wait()
        return 0

    lax.fori_loop(0, tm, drain, 0)
    h2 = h1_ref[...] + ybuf_ref[...]
    ms = jnp.mean(h2 * h2, axis=-1, keepdims=True)
    o_ref[...] = h2 * lax.rsqrt(ms + EPS) * gf_ref[...]


def _combine(pos, ys, h1, g_final, tm):
    t, d = h1.shape
    return pl.pallas_call(
        _combine_kernel,
        grid_spec=pltpu.PrefetchScalarGridSpec(
            num_scalar_prefetch=1,
            grid=(t // tm,),
            in_specs=[pl.BlockSpec(memory_space=pl.ANY),
                      pl.BlockSpec((tm, d), lambda i, pos: (i, 0)),
                      pl.BlockSpec((1, d), lambda i, pos: (0, 0))],
            out_specs=pl.BlockSpec((tm, d), lambda i, pos: (i, 0)),
            scratch_shapes=[pltpu.VMEM((tm, d), F32), pltpu.SemaphoreType.DMA]),
        out_shape=jax.ShapeDtypeStruct((t, d), F32),
        compiler_params=_cparams(("arbitrary",)),
        name="moe_combine_final",
    )(pos, ys, h1, g_final.reshape(1, d))


def _group_plan(plan, ts):
    t = plan.shape[0]
    g_sel = plan[:, N_GROUPS].astype(jnp.int32)
    cum_sel = jnp.take_along_axis(plan[:, :N_GROUPS], g_sel[:, None], axis=1)[:, 0]
    counts = plan[t - 1, :N_GROUPS].astype(jnp.int32)
    padded = (counts + ts - 1) // ts * ts
    ends = jnp.cumsum(padded)
    starts = ends - padded
    pos = starts[g_sel] + cum_sel.astype(jnp.int32) - 1
    n_tiles = (t + N_GROUPS * (ts - 1)) // ts
    tile_start = jnp.arange(n_tiles, dtype=jnp.int32) * ts
    tile_group = jnp.minimum(jnp.sum(tile_start[:, None] >= ends[None, :], axis=1), N_GROUPS - 1)
    tile_valid = (tile_start < ends[N_GROUPS - 1]).astype(jnp.int32)
    return pos, tile_group.astype(jnp.int32), tile_valid, n_tiles * ts


def _pick(n, pref):
    return pref if n % pref == 0 else n


def _pad_rows(a, rows):
    return jnp.pad(a, ((0, rows - a.shape[0]), (0, 0)))


def _layer(x, meta, g_mix, w_in, g_kv, w_kv_up, g_idx_k, b_f, w_branch_a, w_branch_b, w_out,
           g_ffn, w_group, b_group, w_expert, b_expert, w_gate_e, w_up_e, w_down_e, g_final):
    t, d = x.shape
    n_all = t + N_META
    topk = min(TOPK_MAX, n_all // 4)

    o = [0]
    for wdt in (A_WIDTH, A_KV_RANK, IDX_HEADS * IDX_DIM, IDX_DIM, IDX_HEADS, B_WIDTH, B_WIDTH, B_WIDTH,
                B_HEADS, D_MODEL, D_MODEL):
        o.append(o[-1] + wdt)
    wb = w_in.astype(BF16)
    w_qa, w_ckv, w_qi = wb[:, o[0]:o[1]], wb[:, o[1]:o[2]], wb[:, o[2]:o[3]]
    w_small = jnp.concatenate(
        [wb[:, o[3]:o[5]], wb[:, o[8]:o[9]], jnp.zeros((d, LANES - (o[5] - o[3]) - B_HEADS), BF16)], axis=1)
    w_b = wb[:, o[5]:o[8]]
    w_ga, w_gb = wb[:, o[9]:o[10]], wb[:, o[10]:o[11]]
    w_up_bf = w_kv_up.astype(BF16)
    g_idx_pad = jnp.zeros((1, LANES), F32).at[0, SM_KI:SM_KI + IDX_DIM].set(g_idx_k)
    bf_pad = jnp.zeros((1, LANES), F32).at[0, SM_FB:SM_FB + B_HEADS].set(b_f)

    pos_meta = jnp.arange(N_META)
    pos_real = jnp.arange(N_META, n_all)
    cos64_m, sin64_m = _rope_tables(pos_meta, HEAD_DIM // 2)
    cos64_r, sin64_r = _rope_tables(pos_real, HEAD_DIM // 2)
    cos32_m, sin32_m = _rope_tables(pos_meta, IDX_DIM // 2)
    cos32_r, sin32_r = _rope_tables(pos_real, IDX_DIM // 2)

    tm = _pick(t, 512)

    hm = _rmsnorm(meta, g_mix, N_META)
    ka_m, va_m = _proj_kv(hm, w_ckv, g_kv, w_up_bf, cos64_m, sin64_m, N_META)
    small_m = _proj_small(hm, w_small, g_idx_pad, bf_pad, jnp.zeros((8, LANES), F32), cos32_m, sin32_m, N_META)
    b_m = _proj_plain(hm, w_b, N_META, 1024)
    c_carry = jnp.broadcast_to(small_m[N_META - 1:N_META, :], (8, LANES))

    h = _rmsnorm(x, g_mix, tm)
    qa = _proj_rope(h, w_qa, cos64_r, sin64_r, HEAD_DIM // 2, tm, 1024)
    ka_r, va_r = _proj_kv(h, w_ckv, g_kv, w_up_bf, cos64_r, sin64_r, tm)
    qi_hm = _proj_rope(h, w_qi, cos32_r, sin32_r, IDX_DIM // 2, tm, IDX_HEADS * IDX_DIM, head_major=True)
    small_r = _proj_small(h, w_small, g_idx_pad, bf_pad, c_carry, cos32_r, sin32_r, tm)
    b_r = _proj_plain(h, w_b, tm, 1024)

    ki_real = small_r[:, SM_KI:SM_KI + IDX_DIM].astype(BF16)
    ki_meta = _pad_rows(small_m[:, SM_KI:SM_KI + IDX_DIM], META_PAD).astype(BF16)
    wt = small_r[:, SM_WI:SM_WI + IDX_HEADS].T
    cq = small_r[:, SM_FB:SM_FB + B_HEADS]
    ct_meta = _pad_rows(small_m[:, SM_FB:SM_FB + B_HEADS], META_PAD).T

    bias_real, bias_meta = _select_bias(qi_hm, ki_real, ki_meta, wt, topk, _pick(t, 256))
    ya = _attention((qa, 0, ka_r, 0, va_r, 0),
                    (_pad_rows(ka_m, META_PAD), 0, _pad_rows(va_m, META_PAD), 0),
                    (bias_real, bias_meta), fox=False, t_blk=_pick(t, 512))
    b_m_pad = _pad_rows(b_m, META_PAD)
    yb = _attention((b_r, 0, b_r, 1, b_r, 2), (b_m_pad, 1, b_m_pad, 2),
                    (cq, cq.T, ct_meta), fox=True, t_blk=_pick(t, 512))

    w_r = jnp.concatenate([w_group, w_expert, jnp.zeros((d, LANES - N_GROUPS - N_EXPERTS), F32)], axis=1)
    wr_hi = w_r.astype(BF16)
    wr_lo = (w_r - wr_hi.astype(F32)).astype(BF16)
    b_r_pad = jnp.concatenate([b_group, b_expert, jnp.zeros((LANES - N_GROUPS - N_EXPERTS,), F32)]).reshape(1, LANES)
    h1, xr, plan = _merge(h, ya, yb, w_ga, w_gb, w_branch_a.astype(BF16), w_branch_b.astype(BF16),
                          w_out.astype(BF16), x, g_ffn, wr_hi, wr_lo, b_r_pad, tm, 256)

    ts = _pick(t, 512)
    pos, tile_group, tile_valid, n_slots = _group_plan(plan, ts)
    xs = _dispatch(pos, xr, n_slots, _pick(t, 256))
    ys = _group_ffn(tile_group, tile_valid, xs, w_gate_e.astype(BF16), w_up_e.astype(BF16),
                    w_down_e.astype(BF16), ts)
    return _combine(pos, ys, h1, g_final, _pick(t, 256))


def kernel(x, meta_tokens, g_mix, w_in, g_kv, w_kv_up, g_idx_k, b_f, w_branch_a, w_branch_b, w_out, g_ffn,
           w_group, b_group, w_expert, b_expert, w_gate_e, w_up_e, w_down_e, g_final):
    assert x.shape[0] == 1 and g_mix.shape[0] == 1, "single batch, single layer"
    out = _layer(x[0], meta_tokens.astype(x.dtype), g_mix[0], w_in[0], g_kv[0], w_kv_up[0], g_idx_k[0], b_f[0],
                 w_branch_a[0], w_branch_b[0], w_out[0], g_ffn[0], w_group[0], b_group[0], w_expert[0],
                 b_expert[0], w_gate_e[0], w_up_e[0], w_down_e[0], g_final)
    return out[None]
```

```python
import functools

import jax
import jax.numpy as jnp
from jax import lax
from jax.experimental import pallas as pl
from jax.experimental.pallas import tpu as pltpu

F32 = jnp.float32
BF16 = jnp.bfloat16

D_MODEL = 2048
N_META = 16
HEAD_DIM = 128
ROPE_THETA = 10000.0
EPS = 1e-6
NEG = -1e30
A_HEADS = 8
A_WIDTH = A_HEADS * HEAD_DIM
A_KV_RANK = 512
IDX_HEADS = 16
IDX_DIM = 64
TOPK_MAX = 256
B_HEADS = 8
B_WIDTH = B_HEADS * HEAD_DIM
N_GROUPS = 4
EXPERTS_PER_GROUP = 4
N_EXPERTS = N_GROUPS * EXPERTS_PER_GROUP
D_FF_EXPERT = 512

LANES = 128
META_PAD = LANES
VMEM_LIMIT = 56 * 1024 * 1024
LOG2E = 1.4426950408889634
NT_DIMS = (((1,), (1,)), ((), ()))

SM_KI = 0
SM_WI = IDX_DIM
SM_FB = IDX_DIM + IDX_HEADS
RT_E0 = N_GROUPS


def _cparams(sem):
    return pltpu.CompilerParams(dimension_semantics=sem, vmem_limit_bytes=VMEM_LIMIT)


def _rms_kernel(x_ref, g_ref, o_ref):
    x = x_ref[...]
    ms = jnp.mean(x * x, axis=-1, keepdims=True)
    o_ref[...] = (x * lax.rsqrt(ms + EPS) * g_ref[...]).astype(o_ref.dtype)


def _rmsnorm(x, g, tm):
    n, d = x.shape
    return pl.pallas_call(
        _rms_kernel,
        grid=(n // tm,),
        in_specs=[pl.BlockSpec((tm, d), lambda i: (i, 0)), pl.BlockSpec((1, d), lambda i: (0, 0))],
        out_specs=pl.BlockSpec((tm, d), lambda i: (i, 0)),
        out_shape=jax.ShapeDtypeStruct((n, d), BF16),
        compiler_params=_cparams(("parallel",)),
        name="rmsnorm",
    )(x, g.reshape(1, d))


def _rope_tile(y, cos, sin_signed, half):
    if 2 * half == LANES:
        partner = pltpu.roll(y, half, axis=1)
    else:
        lane = lax.broadcasted_iota(jnp.int32, y.shape, 1)
        first = (lane % (2 * half)) < half
        partner = jnp.where(first, pltpu.roll(y, LANES - half, axis=1), pltpu.roll(y, half, axis=1))
    return y * cos + partner * sin_signed


def _rope_tables(pos, half):
    inv_freq = ROPE_THETA ** (-jnp.arange(half, dtype=F32) / half)
    ang = pos.astype(F32)[:, None] * inv_freq[None, :]
    cos, sin = jnp.cos(ang), jnp.sin(ang)
    reps = LANES // (2 * half)
    cos_t = jnp.tile(jnp.concatenate([cos, cos], axis=1), (1, reps))
    sin_t = jnp.tile(jnp.concatenate([-sin, sin], axis=1), (1, reps))
    return cos_t, sin_t


def _proj_plain_kernel(h_ref, w_ref, o_ref, *, first_tile_scale):
    y = jnp.dot(h_ref[...], w_ref[...], preferred_element_type=F32)
    scale = jnp.where(pl.program_id(1) == 0, jnp.float32(first_tile_scale), jnp.float32(1.0))
    o_ref[...] = (y * scale).astype(o_ref.dtype)


def _proj_plain(h, w, tm, tn, first_tile_scale=1.0):
    n, d = h.shape
    nc = w.shape[1]
    return pl.pallas_call(
        functools.partial(_proj_plain_kernel, first_tile_scale=first_tile_scale),
        grid=(n // tm, nc // tn),
        in_specs=[pl.BlockSpec((tm, d), lambda i, j: (i, 0)), pl.BlockSpec((d, tn), lambda i, j: (0, j))],
        out_specs=pl.BlockSpec((tm, tn), lambda i, j: (i, j)),
        out_shape=jax.ShapeDtypeStruct((n, nc), BF16),
        compiler_params=_cparams(("parallel", "parallel")),
        name="proj_plain",
    )(h, w)


def _proj_rope_kernel(h_ref, w_ref, cos_ref, sin_ref, o_ref, *, half, head_major, out_scale):
    y = jnp.dot(h_ref[...], w_ref[...], preferred_element_type=F32)
    cos, sin = cos_ref[...], sin_ref[...]
    per_tile = LANES // (2 * half)
    for c in range(y.shape[1] // LANES):
        sl = slice(c * LANES, (c + 1) * LANES)
        roped = (_rope_tile(y[:, sl], cos, sin, half) * out_scale).astype(o_ref.dtype)
        if head_major:
            for k in range(per_tile):
                o_ref[c * per_tile + k] = roped[:, k * 2 * half:(k + 1) * 2 * half]
        else:
            o_ref[:, sl] = roped


def _proj_rope(h, w, cos, sin, half, tm, tn, head_major=False, out_scale=1.0):
    n, d = h.shape
    nc = w.shape[1]
    if head_major:
        assert tn == nc
        heads = nc // (2 * half)
        out_spec = pl.BlockSpec((heads, tm, 2 * half), lambda i, j: (0, i, 0))
        out_shape = jax.ShapeDtypeStruct((heads, n, 2 * half), BF16)
    else:
        out_spec = pl.BlockSpec((tm, tn), lambda i, j: (i, j))
        out_shape = jax.ShapeDtypeStruct((n, nc), BF16)
    return pl.pallas_call(
        functools.partial(_proj_rope_kernel, half=half, head_major=head_major, out_scale=out_scale),
        grid=(n // tm, nc // tn),
        in_specs=[pl.BlockSpec((tm, d), lambda i, j: (i, 0)), pl.BlockSpec((d, tn), lambda i, j: (0, j)),
                  pl.BlockSpec((tm, LANES), lambda i, j: (i, 0)), pl.BlockSpec((tm, LANES), lambda i, j: (i, 0))],
        out_specs=out_spec,
        out_shape=out_shape,
        compiler_params=_cparams(("parallel", "parallel")),
        name="proj_rope",
    )(h, w, cos, sin)


def _kv_kernel(h_ref, wc_ref, g_ref, wup_ref, cos_ref, sin_ref, ka_ref, va_ref):
    c = jnp.dot(h_ref[...], wc_ref[...], preferred_element_type=F32)
    ms = jnp.mean(c * c, axis=-1, keepdims=True)
    cn = (c * lax.rsqrt(ms + EPS) * g_ref[...]).astype(BF16)
    kv = jnp.dot(cn, wup_ref[...], preferred_element_type=F32)
    cos, sin = cos_ref[...], sin_ref[...]
    for hd in range(A_HEADS):
        sl = slice(hd * HEAD_DIM, (hd + 1) * HEAD_DIM)
        ka_ref[:, sl] = _rope_tile(kv[:, sl], cos, sin, HEAD_DIM // 2).astype(ka_ref.dtype)
    va_ref[...] = kv[:, A_WIDTH:].astype(va_ref.dtype)


def _proj_kv(h, wc, g_kv, wup, cos, sin, tm):
    n, d = h.shape
    return pl.pallas_call(
        _kv_kernel,
        grid=(n // tm,),
        in_specs=[pl.BlockSpec((tm, d), lambda i: (i, 0)),
                  pl.BlockSpec((d, A_KV_RANK), lambda i: (0, 0)),
                  pl.BlockSpec((1, A_KV_RANK), lambda i: (0, 0)),
                  pl.BlockSpec((A_KV_RANK, 2 * A_WIDTH), lambda i: (0, 0)),
                  pl.BlockSpec((tm, LANES), lambda i: (i, 0)),
                  pl.BlockSpec((tm, LANES), lambda i: (i, 0))],
        out_specs=[pl.BlockSpec((tm, A_WIDTH), lambda i: (i, 0)), pl.BlockSpec((tm, A_WIDTH), lambda i: (i, 0))],
        out_shape=[jax.ShapeDtypeStruct((n, A_WIDTH), BF16), jax.ShapeDtypeStruct((n, A_WIDTH), BF16)],
        compiler_params=_cparams(("parallel",)),
        name="proj_kv",
    )(h, wc, g_kv.reshape(1, A_KV_RANK), wup, cos, sin)


def _small_kernel(h_ref, w_ref, g_ref, bf_ref, c0_ref, cos_ref, sin_ref, o_ref, carry_ref):
    tm = h_ref.shape[0]

    @pl.when(pl.program_id(0) == 0)
    def _():
        carry_ref[...] = c0_ref[...]

    y = jnp.dot(h_ref[...], w_ref[...], preferred_element_type=F32)
    lane = lax.broadcasted_iota(jnp.int32, y.shape, 1)
    is_ki = lane < SM_WI
    ms = jnp.sum(jnp.where(is_ki, y * y, 0.0), axis=-1, keepdims=True) * (1.0 / IDX_DIM)
    ki = _rope_tile(y * lax.rsqrt(ms + EPS) * g_ref[...], cos_ref[...], sin_ref[...], IDX_DIM // 2)
    wi = y * (IDX_HEADS ** -0.5 * IDX_DIM ** -0.5)
    z = y + bf_ref[...]
    logf = -(jnp.maximum(-z, 0.0) + jnp.log(1.0 + jnp.exp(-jnp.abs(z))))
    hi = logf.astype(BF16)
    r1 = logf - hi.astype(F32)
    mid = r1.astype(BF16)
    lo = (r1 - mid.astype(F32)).astype(BF16)
    row = lax.broadcasted_iota(jnp.int32, (tm, tm), 0)
    col = lax.broadcasted_iota(jnp.int32, (tm, tm), 1)
    tri = (col <= row).astype(BF16)
    csum = (jnp.dot(tri, hi, preferred_element_type=F32) + jnp.dot(tri, mid, preferred_element_type=F32)
            + jnp.dot(tri, lo, preferred_element_type=F32)) + carry_ref[0:1, :]
    carry_ref[...] = jnp.broadcast_to(csum[tm - 1:tm, :], carry_ref.shape)
    o_ref[...] = jnp.where(is_ki, ki, jnp.where(lane < SM_FB, wi, csum))


def _proj_small(h, w, g_idx_pad, bf_pad, c0, cos, sin, tm):
    n, d = h.shape
    return pl.pallas_call(
        _small_kernel,
        grid=(n // tm,),
        in_specs=[pl.BlockSpec((tm, d), lambda i: (i, 0)),
                  pl.BlockSpec((d, LANES), lambda i: (0, 0)),
                  pl.BlockSpec((1, LANES), lambda i: (0, 0)),
                  pl.BlockSpec((1, LANES), lambda i: (0, 0)),
                  pl.BlockSpec((8, LANES), lambda i: (0, 0)),
                  pl.BlockSpec((tm, LANES), lambda i: (i, 0)),
                  pl.BlockSpec((tm, LANES), lambda i: (i, 0))],
        out_specs=pl.BlockSpec((tm, LANES), lambda i: (i, 0)),
        out_shape=jax.ShapeDtypeStruct((n, LANES), F32),
        scratch_shapes=[pltpu.VMEM((8, LANES), F32)],
        compiler_params=_cparams(("arbitrary",)),
        name="proj_small",
    )(h, w, g_idx_pad, bf_pad, c0, cos, sin)


COUNT_ROWS = 64


def _key_to_float(key):
    bits = key ^ ((key >> 31) & jnp.int32(0x7FFFFFFF))
    return lax.bitcast_convert_type(bits, F32)


def _select_kernel(qi_ref, kir_ref, kim_ref, wt_ref, or_ref, om_ref, s_ref, *, tq, topk):
    i = pl.program_id(0)
    n_real = or_ref.shape[1]
    n_chunks = n_real // tq
    krow = lax.broadcasted_iota(jnp.int32, (tq, tq), 0)
    qcol = lax.broadcasted_iota(jnp.int32, (tq, tq), 1)
    mrow = lax.broadcasted_iota(jnp.int32, (META_PAD, tq), 0)
    wt = wt_ref[...]

    def scores(kc):
        s = jnp.zeros((kc.shape[0], tq), F32)
        for h in range(IDX_HEADS):
            d = lax.dot_general(kc, qi_ref[h], NT_DIMS, preferred_element_type=F32)
            s = s + wt[h:h + 1, :] * jnp.maximum(d, 0.0)
        return s

    def causal(c):
        return krow + (c - i) * tq <= qcol

    def score_chunk(c, _):
        off = pl.multiple_of(c * tq, tq)
        s_ref[pl.ds(off, tq), :] = jnp.where(causal(c), scores(kir_ref[pl.ds(off, tq), :]), NEG)
        return 0

    lax.fori_loop(0, i + 1, score_chunk, 0)
    s_ref[n_real:, :] = jnp.where(mrow < N_META, scores(kim_ref[...]), NEG)

    def count(cmp, thr):
        def body(c, acc):
            off = c * tq
            for r in range(tq // COUNT_ROWS):
                blk = s_ref[pl.ds(pl.multiple_of(off + r * COUNT_ROWS, COUNT_ROWS), COUNT_ROWS), :]
                acc = acc + jnp.where(cmp(blk, thr), 1.0, 0.0)
            return acc

        acc = lax.fori_loop(0, i + 1, body, jnp.zeros((COUNT_ROWS, tq), F32))
        for r in range(META_PAD // COUNT_ROWS):
            blk = s_ref[n_real + r * COUNT_ROWS:n_real + (r + 1) * COUNT_ROWS, :]
            acc = acc + jnp.where(cmp(blk, thr), 1.0, 0.0)
        return jnp.sum(acc, axis=0, keepdims=True)

    kf = jnp.float32(topk)

    def bit_step(b, carry):
        t_key, cnt_t = carry
        cand = t_key + lax.shift_left(jnp.int32(1), 31 - b)
        cnt = count(lambda s, t: s >= t, _key_to_float(cand))
        take = cnt >= kf
        return jnp.where(take, cand, t_key), jnp.where(take, cnt, cnt_t)

    t0 = jnp.full((1, tq), jnp.iinfo(jnp.int32).min, jnp.int32)
    c0 = jnp.broadcast_to(((i + 1) * tq + META_PAD).astype(F32), (1, tq))
    t_key, cnt_t = lax.fori_loop(0, 32, bit_step, (t0, c0))
    thr = _key_to_float(t_key)

    def fill_chunk(c, _):
        or_ref[:, pl.ds(pl.multiple_of(c * tq, tq), tq)] = jnp.full((tq, tq), NEG, or_ref.dtype)
        return 0

    lax.fori_loop(i + 1, n_chunks, fill_chunk, 0)

    ambiguous = jnp.logical_and(cnt_t > kf, thr > NEG)
    any_amb = jnp.max(jnp.where(ambiguous, 1.0, 0.0)) > 0.0

    @pl.when(jnp.logical_not(any_amb))
    def _():
        def out_chunk(c, _):
            off = pl.multiple_of(c * tq, tq)
            sel = jnp.where(s_ref[pl.ds(off, tq), :] >= thr, 0.0, NEG)
            or_ref[:, pl.ds(off, tq)] = jnp.where(causal(c), sel, NEG).T.astype(or_ref.dtype)
            return 0

        lax.fori_loop(0, i + 1, out_chunk, 0)
        sel = jnp.where(s_ref[n_real:, :] >= thr, 0.0, NEG)
        om_ref[...] = jnp.where(mrow < N_META, sel, NEG).T.astype(om_ref.dtype)

    @pl.when(any_amb)
    def _():
        cnt_gt = count(lambda s, t: s > t, thr)
        quota = jnp.where(ambiguous, kf - cnt_gt, jnp.float32(3e38))

        def tie_select(s, valid, running):
            rows = s.shape[0]
            r = lax.broadcasted_iota(jnp.int32, (rows, rows), 0)
            c_ = lax.broadcasted_iota(jnp.int32, (rows, rows), 1)
            lower = (c_ <= r).astype(BF16)
            eq = jnp.where(s == thr, 1.0, 0.0)
            rank = jnp.dot(lower, eq.astype(BF16), preferred_element_type=F32) + running
            keep = jnp.where(s > thr, 0.0, jnp.where(jnp.logical_and(eq > 0.0, rank <= quota), 0.0, NEG))
            return jnp.where(valid, keep, NEG), running + jnp.sum(eq, axis=0, keepdims=True)

        bias_m, running = tie_select(s_ref[n_real:, :], mrow < N_META, jnp.zeros((1, tq), F32))
        om_ref[...] = bias_m.T.astype(om_ref.dtype)

        def out_chunk(c, running):
            off = pl.multiple_of(c * tq, tq)
            bias, running = tie_select(s_ref[pl.ds(off, tq), :], causal(c), running)
            or_ref[:, pl.ds(off, tq)] = bias.T.astype(or_ref.dtype)
            return running

        lax.fori_loop(0, i + 1, out_chunk, running)


def _select_bias(qi_hm, ki_real, ki_meta, wt, topk, tq):
    _, t, _ = qi_hm.shape
    return pl.pallas_call(
        functools.partial(_select_kernel, tq=tq, topk=topk),
        grid=(t // tq,),
        in_specs=[pl.BlockSpec((IDX_HEADS, tq, IDX_DIM), lambda i: (0, i, 0)),
                  pl.BlockSpec((t, IDX_DIM), lambda i: (0, 0)),
                  pl.BlockSpec((META_PAD, IDX_DIM), lambda i: (0, 0)),
                  pl.BlockSpec((IDX_HEADS, tq), lambda i: (0, i))],
        out_specs=[pl.BlockSpec((tq, t), lambda i: (i, 0)), pl.BlockSpec((tq, META_PAD), lambda i: (i, 0))],
        out_shape=[jax.ShapeDtypeStruct((t, t), BF16), jax.ShapeDtypeStruct((t, META_PAD), BF16)],
        scratch_shapes=[pltpu.VMEM((t + META_PAD, tq), F32)],
        compiler_params=_cparams(("parallel",)),
        name="select_bias",
    )(qi_hm, ki_real, ki_meta, wt)


def _attn_kernel(*refs, t, fox):
    if fox:
        (qi_ref, kj_ref, q_ref, k_ref, v_ref, km_ref, vm_ref, cq_ref, ck_ref, ckm_ref,
         o_ref, m_ref, l_ref, acc_ref, r_ref) = refs
    else:
        (qi_ref, kj_ref, q_ref, k_ref, v_ref, km_ref, vm_ref, b_ref, bm_ref,
         o_ref, m_ref, l_ref, acc_ref) = refs
    step = pl.program_id(0)
    i = qi_ref[step]
    j = kj_ref[step]
    nheads = q_ref.shape[1] // HEAD_DIM

    def block(k_blk_ref, v_blk_ref, bias_fn):
        width = k_blk_ref.shape[0]
        for h in range(nheads):
            sl = slice(h * HEAD_DIM, (h + 1) * HEAD_DIM)
            s = lax.dot_general(q_ref[:, sl], k_blk_ref[:, sl], NT_DIMS, preferred_element_type=F32)
            u = bias_fn(s, h)
            m_prev = m_ref[h]
            mx = jnp.max(u, axis=1, keepdims=True)
            if fox:
                r = r_ref[h]
                m_next = jnp.maximum(m_prev, mx + r)
                sub = m_next - r
            else:
                m_next = jnp.maximum(m_prev, mx)
                sub = m_next
            alpha = jnp.exp2(m_prev - m_next)
            ps = [jnp.exp2(u[:, c * LANES:(c + 1) * LANES] - sub) for c in range(width // LANES)]
            psum = ps[0]
            for pc in ps[1:]:
                psum = psum + pc
            p = ps[0] if len(ps) == 1 else jnp.concatenate(ps, axis=1)
            l_ref[h] = alpha * l_ref[h] + psum
            m_ref[h] = m_next
            acc_ref[h] = alpha * acc_ref[h] + jnp.dot(p.astype(BF16), v_blk_ref[:, sl],
                                                      preferred_element_type=F32)

    @pl.when(j == 0)
    def _():
        m_ref[...] = jnp.full(m_ref.shape, NEG, F32)
        l_ref[...] = jnp.zeros(l_ref.shape, F32)
        acc_ref[...] = jnp.zeros(acc_ref.shape, F32)
        mcol = lax.broadcasted_iota(jnp.int32, (t, META_PAD), 1)
        if fox:
            for h in range(nheads):
                r_ref[h] = jnp.broadcast_to(cq_ref[:, h:h + 1] * LOG2E, (t, LANES))

            def bias_fn(u, h):
                return jnp.where(mcol < N_META, u - ckm_ref[h:h + 1, :] * LOG2E, NEG)
        else:
            bias_m = bm_ref[...].astype(F32)

            def bias_fn(u, h):
                return u + bias_m
        block(km_ref, vm_ref, bias_fn)

    if fox:
        @pl.when(j < i)
        def _():
            block(k_ref, v_ref, lambda u, h: u - ck_ref[h:h + 1, :] * LOG2E)

        @pl.when(j == i)
        def _():
            row = lax.broadcasted_iota(jnp.int32, (t, t), 0)
            col = lax.broadcasted_iota(jnp.int32, (t, t), 1)
            block(k_ref, v_ref, lambda u, h: jnp.where(col <= row, u - ck_ref[h:h + 1, :] * LOG2E, NEG))
    else:
        bias = b_ref[...].astype(F32)
        block(k_ref, v_ref, lambda u, h: u + bias)

    @pl.when(j == i)
    def _():
        for h in range(nheads):
            sl = slice(h * HEAD_DIM, (h + 1) * HEAD_DIM)
            o_ref[:, sl] = (acc_ref[h] / jnp.sum(l_ref[h], axis=1, keepdims=True)).astype(o_ref.dtype)


def _attention(qkv, meta_kv, extra, *, fox, t_blk):
    q_arr, q_col, k_arr, k_col, v_arr, v_col = qkv
    km_arr, km_col, vm_arr, vm_col = meta_kv
    t = q_arr.shape[0]
    w = A_WIDTH
    nheads = w // HEAD_DIM
    nq = t // t_blk
    qi = jnp.asarray([i for i in range(nq) for _ in range(i + 1)], jnp.int32)
    kj = jnp.asarray([j for i in range(nq) for j in range(i + 1)], jnp.int32)

    in_specs = [pl.BlockSpec((t_blk, w), lambda s, qi, kj: (qi[s], q_col)),
                pl.BlockSpec((t_blk, w), lambda s, qi, kj: (kj[s], k_col)),
                pl.BlockSpec((t_blk, w), lambda s, qi, kj: (kj[s], v_col)),
                pl.BlockSpec((META_PAD, w), lambda s, qi, kj: (0, km_col)),
                pl.BlockSpec((META_PAD, w), lambda s, qi, kj: (0, vm_col))]
    scratch = [pltpu.VMEM((nheads, t_blk, LANES), F32), pltpu.VMEM((nheads, t_blk, LANES), F32),
               pltpu.VMEM((nheads, t_blk, HEAD_DIM), F32)]
    args = (q_arr, k_arr, v_arr, km_arr, vm_arr)
    if fox:
        cq, ct_real, ct_meta = extra
        in_specs += [pl.BlockSpec((t_blk, nheads), lambda s, qi, kj: (qi[s], 0)),
                     pl.BlockSpec((nheads, t_blk), lambda s, qi, kj: (0, kj[s])),
                     pl.BlockSpec((nheads, META_PAD), lambda s, qi, kj: (0, 0))]
        args += (cq, ct_real, ct_meta)
        scratch.append(pltpu.VMEM((nheads, t_blk, LANES), F32))
    else:
        bias_real, bias_meta = extra
        in_specs += [pl.BlockSpec((t_blk, t_blk), lambda s, qi, kj: (qi[s], kj[s])),
                     pl.BlockSpec((t_blk, META_PAD), lambda s, qi, kj: (qi[s], 0))]
        args += (bias_real, bias_meta)
    return pl.pallas_call(
        functools.partial(_attn_kernel, t=t_blk, fox=fox),
        grid_spec=pltpu.PrefetchScalarGridSpec(
            num_scalar_prefetch=2,
            grid=(int(qi.shape[0]),),
            in_specs=in_specs,
            out_specs=pl.BlockSpec((t_blk, w), lambda s, qi, kj: (qi[s], 0)),
            scratch_shapes=scratch),
        out_shape=jax.ShapeDtypeStruct((t, w), BF16),
        compiler_params=_cparams(("arbitrary",)),
        name="attn_fox" if fox else "attn_dsa",
    )(qi, kj, *args)


def _route(logits):
    lane = lax.broadcasted_iota(jnp.int32, logits.shape, 1).astype(F32)
    big = jnp.float32(LANES)
    ninf = -jnp.inf
    gl = jnp.where(lane < N_GROUPS, logits, ninf)
    gmax = jnp.max(gl, axis=1, keepdims=True)
    g_sel = jnp.min(jnp.where(gl == gmax, lane, big), axis=1, keepdims=True)
    p_group = 1.0 / jnp.sum(jnp.exp(gl - gmax), axis=1, keepdims=True)
    e_lo = RT_E0 + g_sel * EXPERTS_PER_GROUP
    in_grp = jnp.logical_and(lane >= e_lo, lane < e_lo + EXPERTS_PER_GROUP)
    el = jnp.where(in_grp, logits, ninf)
    emax = jnp.max(el, axis=1, keepdims=True)
    ex = jnp.exp(el - emax)
    prob = ex / jnp.sum(ex, axis=1, keepdims=True)
    prob = jnp.where(in_grp, prob, -1.0)
    p1 = jnp.max(prob, axis=1, keepdims=True)
    i1 = jnp.min(jnp.where(prob == p1, lane, big), axis=1, keepdims=True)
    rest = jnp.where(lane == i1, -1.0, prob)
    p2 = jnp.max(rest, axis=1, keepdims=True)
    i2 = jnp.min(jnp.where(rest == p2, lane, big), axis=1, keepdims=True)
    tot = p1 + p2
    w1 = p_group * (p1 / tot)
    w2 = p_group * (p2 / tot)
    return jnp.where(lane == i1, w1, jnp.where(lane == i2, w2, 0.0)), g_sel


def _merge_kernel(h_ref, ya_ref, yb_ref, wga_ref, wgb_ref, wa_ref, wb_ref, wo_ref, x_ref, g_ref,
                  wrh_ref, wrl_ref, br_ref, h1_ref, xr_ref, plan_ref, acc_ref, cnt_ref):
    n = pl.program_id(1)

    @pl.when(jnp.logical_and(pl.program_id(0) == 0, n == 0))
    def _():
        cnt_ref[...] = jnp.zeros(cnt_ref.shape, F32)

    @pl.when(n == 0)
    def _():
        acc_ref[...] = jnp.zeros(acc_ref.shape, F32)

    h = h_ref[...]
    ga = jnp.dot(h, wga_ref[...], preferred_element_type=F32)
    gb = jnp.dot(h, wgb_ref[...], preferred_element_type=F32)
    pa = jnp.dot(ya_ref[...], wa_ref[...], preferred_element_type=F32)
    pb = jnp.dot(yb_ref[...], wb_ref[...], preferred_element_type=F32)
    merged = jax.nn.sigmoid(ga) * pa + jax.nn.sigmoid(gb) * pb
    acc_ref[...] += jnp.dot(merged.astype(BF16), wo_ref[...], preferred_element_type=F32)

    @pl.when(n == pl.num_programs(1) - 1)
    def _():
        h1 = x_ref[...] + acc_ref[...]
        h1_ref[...] = h1
        ms = jnp.mean(h1 * h1, axis=-1, keepdims=True)
        t = h1 * lax.rsqrt(ms + EPS) * g_ref[...]
        t_hi = t.astype(BF16)
        t_lo = (t - t_hi.astype(F32)).astype(BF16)
        logits = (jnp.dot(t_hi, wrh_ref[...], preferred_element_type=F32)
                  + jnp.dot(t_hi, wrl_ref[...], preferred_element_type=F32)
                  + jnp.dot(t_lo, wrh_ref[...], preferred_element_type=F32)) + br_ref[...]
        gates, g_sel = _route(logits)
        d = t.shape[1]
        xr_ref[:, :d] = t
        xr_ref[:, d:] = gates
        tm = t.shape[0]
        lane = lax.broadcasted_iota(jnp.int32, gates.shape, 1).astype(F32)
        onehot = jnp.where(lane == g_sel, 1.0, 0.0).astype(BF16)
        rr = lax.broadcasted_iota(jnp.int32, (tm, tm), 0)
        cc = lax.broadcasted_iota(jnp.int32, (tm, tm), 1)
        cum = jnp.dot((cc <= rr).astype(BF16), onehot, preferred_element_type=F32) + cnt_ref[0:1, :]
        cnt_ref[...] = jnp.broadcast_to(cum[tm - 1:tm, :], cnt_ref.shape)
        plan_ref[...] = jnp.where(lane < N_GROUPS, cum, jnp.where(lane == N_GROUPS, g_sel, 0.0))


def _merge(h, ya, yb, wga, wgb, wa, wb, wo, x, g_ffn, wr_hi, wr_lo, br, tm, tn):
    t, d = h.shape
    row = lambda i, n: (i, 0)
    return pl.pallas_call(
        _merge_kernel,
        grid=(t // tm, d // tn),
        in_specs=[pl.BlockSpec((tm, d), row),
                  pl.BlockSpec((tm, A_WIDTH), row),
                  pl.BlockSpec((tm, B_WIDTH), row),
                  pl.BlockSpec((d, tn), lambda i, n: (0, n)),
                  pl.BlockSpec((d, tn), lambda i, n: (0, n)),
                  pl.BlockSpec((A_WIDTH, tn), lambda i, n: (0, n)),
                  pl.BlockSpec((B_WIDTH, tn), lambda i, n: (0, n)),
                  pl.BlockSpec((tn, d), lambda i, n: (n, 0)),
                  pl.BlockSpec((tm, d), row),
                  pl.BlockSpec((1, d), lambda i, n: (0, 0)),
                  pl.BlockSpec((d, LANES), lambda i, n: (0, 0)),
                  pl.BlockSpec((d, LANES), lambda i, n: (0, 0)),
                  pl.BlockSpec((1, LANES), lambda i, n: (0, 0))],
        out_specs=[pl.BlockSpec((tm, d), row), pl.BlockSpec((tm, d + LANES), row),
                   pl.BlockSpec((tm, LANES), row)],
        out_shape=[jax.ShapeDtypeStruct((t, d), F32), jax.ShapeDtypeStruct((t, d + LANES), F32),
                   jax.ShapeDtypeStruct((t, LANES), F32)],
        scratch_shapes=[pltpu.VMEM((tm, d), F32), pltpu.VMEM((8, LANES), F32)],
        compiler_params=_cparams(("arbitrary", "arbitrary")),
        name="merge_out_router",
    )(h, ya, yb, wga, wgb, wa, wb, wo, x, g_ffn.reshape(1, d), wr_hi, wr_lo, br)


def _row_copy(src_ref, src_row, dst_ref, dst_row, sem):
    return pltpu.make_async_copy(src_ref.at[pl.ds(src_row, 1), :], dst_ref.at[pl.ds(dst_row, 1), :], sem)


def _dispatch_kernel(pos_ref, x_ref, init_ref, xs_ref, sem):
    del init_ref
    i = pl.program_id(0)
    tm = x_ref.shape[0]

    def issue(r, _):
        _row_copy(x_ref, r, xs_ref, pos_ref[i * tm + r], sem).start()
        return 0

    lax.fori_loop(0, tm, issue, 0)

    def drain(r, _):
        _row_copy(x_ref, r, xs_ref, pos_ref[i * tm + r], sem).wait()
        return 0

    lax.fori_loop(0, tm, drain, 0)


def _dispatch(pos, xr, n_slots, tm):
    t, w = xr.shape
    return pl.pallas_call(
        _dispatch_kernel,
        grid_spec=pltpu.PrefetchScalarGridSpec(
            num_scalar_prefetch=1,
            grid=(t // tm,),
            in_specs=[pl.BlockSpec((tm, w), lambda i, pos: (i, 0)),
                      pl.BlockSpec(memory_space=pl.ANY)],
            out_specs=pl.BlockSpec(memory_space=pl.ANY),
            scratch_shapes=[pltpu.SemaphoreType.DMA]),
        out_shape=jax.ShapeDtypeStruct((n_slots, w), xr.dtype),
        input_output_aliases={2: 0},
        compiler_params=_cparams(("arbitrary",)),
        name="moe_dispatch",
    )(pos, xr, jnp.zeros((n_slots, w), xr.dtype))


def _group_ffn_kernel(tg_ref, tv_ref, x_ref, wg_ref, wu_ref, wd_ref, o_ref, acc_ref, xb_ref):
    j = pl.program_id(0)
    k = pl.program_id(1)
    d = o_ref.shape[1]

    @pl.when(k == 0)
    def _():
        acc_ref[...] = jnp.zeros(acc_ref.shape, F32)
        xb_ref[...] = x_ref[:, :d].astype(BF16)

    @pl.when(tv_ref[j] > 0)
    def _():
        gates = x_ref[:, d:]
        lane = lax.broadcasted_iota(jnp.int32, gates.shape, 1)
        e_lane = RT_E0 + tg_ref[j] * EXPERTS_PER_GROUP + k
        g_e = jnp.sum(jnp.where(lane == e_lane, gates, 0.0), axis=1, keepdims=True)
        xb = xb_ref[...]
        hg = jnp.dot(xb, wg_ref[0], preferred_element_type=F32)
        hu = jnp.dot(xb, wu_ref[0], preferred_element_type=F32)
        act = jax.nn.silu(hg) * hu * g_e
        acc_ref[...] += jnp.dot(act.astype(BF16), wd_ref[0], preferred_element_type=F32)

    @pl.when(k == pl.num_programs(1) - 1)
    def _():
        o_ref[...] = acc_ref[...]


def _group_ffn(tile_group, tile_valid, xs, wg, wu, wd, ts):
    n_slots, w = xs.shape
    d = w - LANES
    f = wg.shape[2]
    expert = lambda j, k, tg, tv: (tg[j] * EXPERTS_PER_GROUP + k, 0, 0)
    return pl.pallas_call(
        _group_ffn_kernel,
        grid_spec=pltpu.PrefetchScalarGridSpec(
            num_scalar_prefetch=2,
            grid=(n_slots // ts, EXPERTS_PER_GROUP),
            in_specs=[pl.BlockSpec((ts, w), lambda j, k, tg, tv: (j, 0)),
                      pl.BlockSpec((1, d, f), expert),
                      pl.BlockSpec((1, d, f), expert),
                      pl.BlockSpec((1, f, d), expert)],
            out_specs=pl.BlockSpec((ts, d), lambda j, k, tg, tv: (j, 0)),
            scratch_shapes=[pltpu.VMEM((ts, d), F32), pltpu.VMEM((ts, d), BF16)]),
        out_shape=jax.ShapeDtypeStruct((n_slots, d), F32),
        compiler_params=_cparams(("arbitrary", "arbitrary")),
        name="moe_group_ffn",
    )(tile_group, tile_valid, xs, wg, wu, wd)


def _combine_kernel(pos_ref, ys_ref, h1_ref, gf_ref, o_ref, ybuf_ref, sem):
    i = pl.program_id(0)
    tm = h1_ref.shape[0]

    def issue(r, _):
        _row_copy(ys_ref, pos_ref[i * tm + r], ybuf_ref, r, sem).start()
        return 0

    lax.fori_loop(0, tm, issue, 0)

    def drain(r, _):
        _row_copy(ys_ref, pos_ref[i * tm + r], ybuf_ref, r, sem).wait()
        return 0

    lax.fori_loop(0, tm, drain, 0)
    h2 = h1_ref[...] + ybuf_ref[...]
    ms = jnp.mean(h2 * h2, axis=-1, keepdims=True)
    o_ref[...] = h2 * lax.rsqrt(ms + EPS) * gf_ref[...]


def _combine(pos, ys, h1, g_final, tm):
    t, d = h1.shape
    return pl.pallas_call(
        _combine_kernel,
        grid_spec=pltpu.PrefetchScalarGridSpec(
            num_scalar_prefetch=1,
            grid=(t // tm,),
            in_specs=[pl.BlockSpec(memory_space=pl.ANY),
                      pl.BlockSpec((tm, d), lambda i, pos: (i, 0)),
                      pl.BlockSpec((1, d), lambda i, pos: (0, 0))],
            out_specs=pl.BlockSpec((tm, d), lambda i, pos: (i, 0)),
            scratch_shapes=[pltpu.VMEM((tm, d), F32), pltpu.SemaphoreType.DMA]),
        out_shape=jax.ShapeDtypeStruct((t, d), F32),
        compiler_params=_cparams(("arbitrary",)),
        name="moe_combine_final",
    )(pos, ys, h1, g_final.reshape(1, d))


def _group_plan(plan, ts):
    t = plan.shape[0]
    g_sel = plan[:, N_GROUPS].astype(jnp.int32)
    cum_sel = jnp.take_along_axis(plan[:, :N_GROUPS], g_sel[:, None], axis=1)[:, 0]
    counts = plan[t - 1, :N_GROUPS].astype(jnp.int32)
    padded = (counts + ts - 1) // ts * ts
    ends = jnp.cumsum(padded)
    starts = ends - padded
    pos = starts[g_sel] + cum_sel.astype(jnp.int32) - 1
    n_tiles = (t + N_GROUPS * (ts - 1)) // ts
    tile_start = jnp.arange(n_tiles, dtype=jnp.int32) * ts
    tile_group = jnp.minimum(jnp.sum(tile_start[:, None] >= ends[None, :], axis=1), N_GROUPS - 1)
    tile_valid = (tile_start < ends[N_GROUPS - 1]).astype(jnp.int32)
    return pos, tile_group.astype(jnp.int32), tile_valid, n_tiles * ts


def _pick(n, pref):
    return pref if n % pref == 0 else n


def _pad_rows(a, rows):
    return jnp.pad(a, ((0, rows - a.shape[0]), (0, 0)))


def _layer(x, meta, g_mix, w_in, g_kv, w_kv_up, g_idx_k, b_f, w_branch_a, w_branch_b, w_out,
           g_ffn, w_group, b_group, w_expert, b_expert, w_gate_e, w_up_e, w_down_e, g_final):
    t, d = x.shape
    n_all = t + N_META
    topk = min(TOPK_MAX, n_all // 4)

    o = [0]
    for wdt in (A_WIDTH, A_KV_RANK, IDX_HEADS * IDX_DIM, IDX_DIM, IDX_HEADS, B_WIDTH, B_WIDTH, B_WIDTH,
                B_HEADS, D_MODEL, D_MODEL):
        o.append(o[-1] + wdt)
    def section(lo, hi):
        return w_in[:, o[lo]:o[hi]].astype(BF16)

    w_qa, w_ckv, w_qi = section(0, 1), section(1, 2), section(2, 3)
    w_small = jnp.concatenate(
        [section(3, 5), section(8, 9), jnp.zeros((d, LANES - (o[5] - o[3]) - B_HEADS), BF16)], axis=1)
    w_b = section(5, 8)
    w_ga, w_gb = section(9, 10), section(10, 11)
    w_up_bf = w_kv_up.astype(BF16)
    g_idx_pad = jnp.zeros((1, LANES), F32).at[0, SM_KI:SM_KI + IDX_DIM].set(g_idx_k)
    bf_pad = jnp.zeros((1, LANES), F32).at[0, SM_FB:SM_FB + B_HEADS].set(b_f)

    pos_meta = jnp.arange(N_META)
    pos_real = jnp.arange(N_META, n_all)
    cos64_m, sin64_m = _rope_tables(pos_meta, HEAD_DIM // 2)
    cos64_r, sin64_r = _rope_tables(pos_real, HEAD_DIM // 2)
    cos32_m, sin32_m = _rope_tables(pos_meta, IDX_DIM // 2)
    cos32_r, sin32_r = _rope_tables(pos_real, IDX_DIM // 2)

    tm = _pick(t, 512)

    hm = _rmsnorm(meta, g_mix, N_META)
    ka_m, va_m = _proj_kv(hm, w_ckv, g_kv, w_up_bf, cos64_m, sin64_m, N_META)
    small_m = _proj_small(hm, w_small, g_idx_pad, bf_pad, jnp.zeros((8, LANES), F32), cos32_m, sin32_m, N_META)
    b_m = _proj_plain(hm, w_b, N_META, 1024)
    c_carry = jnp.broadcast_to(small_m[N_META - 1:N_META, :], (8, LANES))

    h = _rmsnorm(x, g_mix, tm)
    qk_scale = HEAD_DIM ** -0.5 * LOG2E
    qa = _proj_rope(h, w_qa, cos64_r, sin64_r, HEAD_DIM // 2, tm, 1024, out_scale=qk_scale)
    ka_r, va_r = _proj_kv(h, w_ckv, g_kv, w_up_bf, cos64_r, sin64_r, tm)
    qi_hm = _proj_rope(h, w_qi, cos32_r, sin32_r, IDX_DIM // 2, tm, IDX_HEADS * IDX_DIM, head_major=True)
    small_r = _proj_small(h, w_small, g_idx_pad, bf_pad, c_carry, cos32_r, sin32_r, tm)
    b_r = _proj_plain(h, w_b, tm, B_WIDTH, first_tile_scale=qk_scale)

    ki_real = small_r[:, SM_KI:SM_KI + IDX_DIM].astype(BF16)
    ki_meta = _pad_rows(small_m[:, SM_KI:SM_KI + IDX_DIM], META_PAD).astype(BF16)
    wt = small_r[:, SM_WI:SM_WI + IDX_HEADS].T
    cq = small_r[:, SM_FB:SM_FB + B_HEADS]
    ct_meta = _pad_rows(small_m[:, SM_FB:SM_FB + B_HEADS], META_PAD).T

    bias_real, bias_meta = _select_bias(qi_hm, ki_real, ki_meta, wt, topk, _pick(t, 256))
    ya = _attention((qa, 0, ka_r, 0, va_r, 0),
                    (_pad_rows(ka_m, META_PAD), 0, _pad_rows(va_m, META_PAD), 0),
                    (bias_real, bias_meta), fox=False, t_blk=_pick(t, 512))
    b_m_pad = _pad_rows(b_m, META_PAD)
    yb = _attention((b_r, 0, b_r, 1, b_r, 2), (b_m_pad, 1, b_m_pad, 2),
                    (cq, cq.T, ct_meta), fox=True, t_blk=_pick(t, 512))

    w_r = jnp.concatenate([w_group, w_expert, jnp.zeros((d, LANES - N_GROUPS - N_EXPERTS), F32)], axis=1)
    wr_hi = w_r.astype(BF16)
    wr_lo = (w_r - wr_hi.astype(F32)).astype(BF16)
    b_r_pad = jnp.concatenate([b_group, b_expert, jnp.zeros((LANES - N_GROUPS - N_EXPERTS,), F32)]).reshape(1, LANES)
    h1, xr, plan = _merge(h, ya, yb, w_ga, w_gb, w_branch_a.astype(BF16), w_branch_b.astype(BF16),
                          w_out.astype(BF16), x, g_ffn, wr_hi, wr_lo, b_r_pad, tm, 256)

    ts = _pick(t, 512)
    pos, tile_group, tile_valid, n_slots = _group_plan(plan, ts)
    xs = _dispatch(pos, xr, n_slots, _pick(t, 256))
    ys = _group_ffn(tile_group, tile_valid, xs, w_gate_e.astype(BF16), w_up_e.astype(BF16),
                    w_down_e.astype(BF16), ts)
    return _combine(pos, ys, h1, g_final, _pick(t, 256))


def kernel(x, meta_tokens, g_mix, w_in, g_kv, w_kv_up, g_idx_k, b_f, w_branch_a, w_branch_b, w_out, g_ffn,
           w_group, b_group, w_expert, b_expert, w_gate_e, w_up_e, w_down_e, g_final):
    assert x.shape[0] == 1 and g_mix.shape[0] == 1, "single batch, single layer"
    out = _layer(x[0], meta_tokens.astype(x.dtype), g_mix[0], w_in[0], g_kv[0], w_kv_up[0], g_idx_k[0], b_f[0],
                 w_branch_a[0], w_branch_b[0], w_out[0], g_ffn[0], w_group[0], b_group[0], w_expert[0],
                 b_expert[0], w_gate_e[0], w_up_e[0], w_down_e[0], g_final)
    return out[None]
```

```python
import functools

import jax
import jax.numpy as jnp
from jax import lax
from jax.experimental import pallas as pl
from jax.experimental.pallas import tpu as pltpu

F32 = jnp.float32
BF16 = jnp.bfloat16

D_MODEL = 2048
N_META = 16
HEAD_DIM = 128
ROPE_THETA = 10000.0
EPS = 1e-6
NEG = -1e30
A_HEADS = 8
A_WIDTH = A_HEADS * HEAD_DIM
A_KV_RANK = 512
IDX_HEADS = 16
IDX_DIM = 64
TOPK_MAX = 256
B_HEADS = 8
B_WIDTH = B_HEADS * HEAD_DIM
N_GROUPS = 4
EXPERTS_PER_GROUP = 4
N_EXPERTS = N_GROUPS * EXPERTS_PER_GROUP
D_FF_EXPERT = 512

LANES = 128
META_PAD = LANES
VMEM_LIMIT = 56 * 1024 * 1024
LOG2E = 1.4426950408889634
NT_DIMS = (((1,), (1,)), ((), ()))

SM_KI = 0
SM_WI = IDX_DIM
SM_FB = IDX_DIM + IDX_HEADS
RT_E0 = N_GROUPS


def _cparams(sem):
    return pltpu.CompilerParams(dimension_semantics=sem, vmem_limit_bytes=VMEM_LIMIT)


def _rms_kernel(x_ref, g_ref, o_ref):
    x = x_ref[...]
    ms = jnp.mean(x * x, axis=-1, keepdims=True)
    o_ref[...] = (x * lax.rsqrt(ms + EPS) * g_ref[...]).astype(o_ref.dtype)


def _rmsnorm(x, g, tm):
    n, d = x.shape
    return pl.pallas_call(
        _rms_kernel,
        grid=(n // tm,),
        in_specs=[pl.BlockSpec((tm, d), lambda i: (i, 0)), pl.BlockSpec((1, d), lambda i: (0, 0))],
        out_specs=pl.BlockSpec((tm, d), lambda i: (i, 0)),
        out_shape=jax.ShapeDtypeStruct((n, d), BF16),
        compiler_params=_cparams(("parallel",)),
        name="rmsnorm",
    )(x, g.reshape(1, d))


def _rope_tile(y, cos, sin_signed, half):
    if 2 * half == LANES:
        partner = pltpu.roll(y, half, axis=1)
    else:
        lane = lax.broadcasted_iota(jnp.int32, y.shape, 1)
        first = (lane % (2 * half)) < half
        partner = jnp.where(first, pltpu.roll(y, LANES - half, axis=1), pltpu.roll(y, half, axis=1))
    return y * cos + partner * sin_signed


def _rope_tables(pos, half):
    inv_freq = ROPE_THETA ** (-jnp.arange(half, dtype=F32) / half)
    ang = pos.astype(F32)[:, None] * inv_freq[None, :]
    cos, sin = jnp.cos(ang), jnp.sin(ang)
    reps = LANES // (2 * half)
    cos_t = jnp.tile(jnp.concatenate([cos, cos], axis=1), (1, reps))
    sin_t = jnp.tile(jnp.concatenate([-sin, sin], axis=1), (1, reps))
    return cos_t, sin_t


def _proj_plain_kernel(h_ref, w_ref, o_ref, *, first_tile_scale):
    y = jnp.dot(h_ref[...], w_ref[...], preferred_element_type=F32)
    scale = jnp.where(pl.program_id(1) == 0, jnp.float32(first_tile_scale), jnp.float32(1.0))
    o_ref[...] = (y * scale).astype(o_ref.dtype)


def _proj_plain(h, w, col0, nc, tm, tn, first_tile_scale=1.0):
    n, d = h.shape
    c0 = col0 // tn
    return pl.pallas_call(
        functools.partial(_proj_plain_kernel, first_tile_scale=first_tile_scale),
        grid=(n // tm, nc // tn),
        in_specs=[pl.BlockSpec((tm, d), lambda i, j: (i, 0)), pl.BlockSpec((d, tn), lambda i, j: (0, c0 + j))],
        out_specs=pl.BlockSpec((tm, tn), lambda i, j: (i, j)),
        out_shape=jax.ShapeDtypeStruct((n, nc), BF16),
        compiler_params=_cparams(("parallel", "parallel")),
        name="proj_plain",
    )(h, w)


def _proj_rope_kernel(h_ref, w_ref, cos_ref, sin_ref, o_ref, *, half, head_major, out_scale):
    y = jnp.dot(h_ref[...], w_ref[...], preferred_element_type=F32)
    cos, sin = cos_ref[...], sin_ref[...]
    per_tile = LANES // (2 * half)
    for c in range(y.shape[1] // LANES):
        sl = slice(c * LANES, (c + 1) * LANES)
        roped = (_rope_tile(y[:, sl], cos, sin, half) * out_scale).astype(o_ref.dtype)
        if head_major:
            for k in range(per_tile):
                o_ref[c * per_tile + k] = roped[:, k * 2 * half:(k + 1) * 2 * half]
        else:
            o_ref[:, sl] = roped


def _proj_rope(h, w, col0, nc, cos, sin, half, tm, tn, head_major=False, out_scale=1.0):
    n, d = h.shape
    c0 = col0 // tn
    if head_major:
        assert tn == nc
        heads = nc // (2 * half)
        out_spec = pl.BlockSpec((heads, tm, 2 * half), lambda i, j: (0, i, 0))
        out_shape = jax.ShapeDtypeStruct((heads, n, 2 * half), BF16)
    else:
        out_spec = pl.BlockSpec((tm, tn), lambda i, j: (i, j))
        out_shape = jax.ShapeDtypeStruct((n, nc), BF16)
    return pl.pallas_call(
        functools.partial(_proj_rope_kernel, half=half, head_major=head_major, out_scale=out_scale),
        grid=(n // tm, nc // tn),
        in_specs=[pl.BlockSpec((tm, d), lambda i, j: (i, 0)), pl.BlockSpec((d, tn), lambda i, j: (0, c0 + j)),
                  pl.BlockSpec((tm, LANES), lambda i, j: (i, 0)), pl.BlockSpec((tm, LANES), lambda i, j: (i, 0))],
        out_specs=out_spec,
        out_shape=out_shape,
        compiler_params=_cparams(("parallel", "parallel")),
        name="proj_rope",
    )(h, w, cos, sin)


def _kv_kernel(h_ref, wc_ref, g_ref, wup_ref, cos_ref, sin_ref, ka_ref, va_ref):
    c = jnp.dot(h_ref[...], wc_ref[...], preferred_element_type=F32)
    ms = jnp.mean(c * c, axis=-1, keepdims=True)
    cn = (c * lax.rsqrt(ms + EPS) * g_ref[...]).astype(BF16)
    kv = jnp.dot(cn, wup_ref[...], preferred_element_type=F32)
    cos, sin = cos_ref[...], sin_ref[...]
    for hd in range(A_HEADS):
        sl = slice(hd * HEAD_DIM, (hd + 1) * HEAD_DIM)
        ka_ref[:, sl] = _rope_tile(kv[:, sl], cos, sin, HEAD_DIM // 2).astype(ka_ref.dtype)
    va_ref[...] = kv[:, A_WIDTH:].astype(va_ref.dtype)


def _proj_kv(h, wc, col0, g_kv, wup, cos, sin, tm):
    n, d = h.shape
    c0 = col0 // A_KV_RANK
    return pl.pallas_call(
        _kv_kernel,
        grid=(n // tm,),
        in_specs=[pl.BlockSpec((tm, d), lambda i: (i, 0)),
                  pl.BlockSpec((d, A_KV_RANK), lambda i: (0, c0)),
                  pl.BlockSpec((1, A_KV_RANK), lambda i: (0, 0)),
                  pl.BlockSpec((A_KV_RANK, 2 * A_WIDTH), lambda i: (0, 0)),
                  pl.BlockSpec((tm, LANES), lambda i: (i, 0)),
                  pl.BlockSpec((tm, LANES), lambda i: (i, 0))],
        out_specs=[pl.BlockSpec((tm, A_WIDTH), lambda i: (i, 0)), pl.BlockSpec((tm, A_WIDTH), lambda i: (i, 0))],
        out_shape=[jax.ShapeDtypeStruct((n, A_WIDTH), BF16), jax.ShapeDtypeStruct((n, A_WIDTH), BF16)],
        compiler_params=_cparams(("parallel",)),
        name="proj_kv",
    )(h, wc, g_kv.reshape(1, A_KV_RANK), wup, cos, sin)


def _small_kernel(h_ref, w_ref, g_ref, bf_ref, c0_ref, cos_ref, sin_ref, o_ref, carry_ref):
    tm = h_ref.shape[0]

    @pl.when(pl.program_id(0) == 0)
    def _():
        carry_ref[...] = c0_ref[...]

    y = jnp.dot(h_ref[...], w_ref[...], preferred_element_type=F32)
    lane = lax.broadcasted_iota(jnp.int32, y.shape, 1)
    is_ki = lane < SM_WI
    ms = jnp.sum(jnp.where(is_ki, y * y, 0.0), axis=-1, keepdims=True) * (1.0 / IDX_DIM)
    ki = _rope_tile(y * lax.rsqrt(ms + EPS) * g_ref[...], cos_ref[...], sin_ref[...], IDX_DIM // 2)
    wi = y * (IDX_HEADS ** -0.5 * IDX_DIM ** -0.5)
    z = y + bf_ref[...]
    logf = -(jnp.maximum(-z, 0.0) + jnp.log(1.0 + jnp.exp(-jnp.abs(z))))
    hi = logf.astype(BF16)
    r1 = logf - hi.astype(F32)
    mid = r1.astype(BF16)
    lo = (r1 - mid.astype(F32)).astype(BF16)
    row = lax.broadcasted_iota(jnp.int32, (tm, tm), 0)
    col = lax.broadcasted_iota(jnp.int32, (tm, tm), 1)
    tri = (col <= row).astype(BF16)
    csum = (jnp.dot(tri, hi, preferred_element_type=F32) + jnp.dot(tri, mid, preferred_element_type=F32)
            + jnp.dot(tri, lo, preferred_element_type=F32)) + carry_ref[0:1, :]
    carry_ref[...] = jnp.broadcast_to(csum[tm - 1:tm, :], carry_ref.shape)
    o_ref[...] = jnp.where(is_ki, ki, jnp.where(lane < SM_FB, wi, csum))


def _proj_small(h, w, col0, g_idx_pad, bf_pad, c0, cos, sin, tm):
    n, d = h.shape
    cb = col0 // LANES
    return pl.pallas_call(
        _small_kernel,
        grid=(n // tm,),
        in_specs=[pl.BlockSpec((tm, d), lambda i: (i, 0)),
                  pl.BlockSpec((d, LANES), lambda i: (0, cb)),
                  pl.BlockSpec((1, LANES), lambda i: (0, 0)),
                  pl.BlockSpec((1, LANES), lambda i: (0, 0)),
                  pl.BlockSpec((8, LANES), lambda i: (0, 0)),
                  pl.BlockSpec((tm, LANES), lambda i: (i, 0)),
                  pl.BlockSpec((tm, LANES), lambda i: (i, 0))],
        out_specs=pl.BlockSpec((tm, LANES), lambda i: (i, 0)),
        out_shape=jax.ShapeDtypeStruct((n, LANES), F32),
        scratch_shapes=[pltpu.VMEM((8, LANES), F32)],
        compiler_params=_cparams(("arbitrary",)),
        name="proj_small",
    )(h, w, g_idx_pad, bf_pad, c0, cos, sin)


COUNT_ROWS = 64


def _key_to_float(key):
    bits = key ^ ((key >> 31) & jnp.int32(0x7FFFFFFF))
    return lax.bitcast_convert_type(bits, F32)


def _select_kernel(qi_ref, kir_ref, kim_ref, wt_ref, or_ref, om_ref, s_ref, *, tq, topk):
    i = pl.program_id(0)
    n_real = or_ref.shape[1]
    n_chunks = n_real // tq
    krow = lax.broadcasted_iota(jnp.int32, (tq, tq), 0)
    qcol = lax.broadcasted_iota(jnp.int32, (tq, tq), 1)
    mrow = lax.broadcasted_iota(jnp.int32, (META_PAD, tq), 0)
    wt = wt_ref[...]

    def scores(kc):
        s = jnp.zeros((kc.shape[0], tq), F32)
        for h in range(IDX_HEADS):
            d = lax.dot_general(kc, qi_ref[h], NT_DIMS, preferred_element_type=F32)
            s = s + wt[h:h + 1, :] * jnp.maximum(d, 0.0)
        return s

    def causal(c):
        return krow + (c - i) * tq <= qcol

    def score_chunk(c, _):
        off = pl.multiple_of(c * tq, tq)
        s_ref[pl.ds(off, tq), :] = jnp.where(causal(c), scores(kir_ref[pl.ds(off, tq), :]), NEG)
        return 0

    lax.fori_loop(0, i + 1, score_chunk, 0)
    s_ref[n_real:, :] = jnp.where(mrow < N_META, scores(kim_ref[...]), NEG)

    def count(cmp, thr):
        def body(c, acc):
            off = c * tq
            for r in range(tq // COUNT_ROWS):
                blk = s_ref[pl.ds(pl.multiple_of(off + r * COUNT_ROWS, COUNT_ROWS), COUNT_ROWS), :]
                acc = acc + jnp.where(cmp(blk, thr), 1.0, 0.0)
            return acc

        acc = lax.fori_loop(0, i + 1, body, jnp.zeros((COUNT_ROWS, tq), F32))
        for r in range(META_PAD // COUNT_ROWS):
            blk = s_ref[n_real + r * COUNT_ROWS:n_real + (r + 1) * COUNT_ROWS, :]
            acc = acc + jnp.where(cmp(blk, thr), 1.0, 0.0)
        return jnp.sum(acc, axis=0, keepdims=True)

    kf = jnp.float32(topk)

    def bit_step(b, carry):
        t_key, cnt_t = carry
        cand = t_key + lax.shift_left(jnp.int32(1), 31 - b)
        cnt = count(lambda s, t: s >= t, _key_to_float(cand))
        take = cnt >= kf
        return jnp.where(take, cand, t_key), jnp.where(take, cnt, cnt_t)

    t0 = jnp.full((1, tq), jnp.iinfo(jnp.int32).min, jnp.int32)
    c0 = jnp.broadcast_to(((i + 1) * tq + META_PAD).astype(F32), (1, tq))
    t_key, cnt_t = lax.fori_loop(0, 32, bit_step, (t0, c0))
    thr = _key_to_float(t_key)

    def fill_chunk(c, _):
        or_ref[:, pl.ds(pl.multiple_of(c * tq, tq), tq)] = jnp.full((tq, tq), NEG, or_ref.dtype)
        return 0

    lax.fori_loop(i + 1, n_chunks, fill_chunk, 0)

    ambiguous = jnp.logical_and(cnt_t > kf, thr > NEG)
    any_amb = jnp.max(jnp.where(ambiguous, 1.0, 0.0)) > 0.0

    @pl.when(jnp.logical_not(any_amb))
    def _():
        def out_chunk(c, _):
            off = pl.multiple_of(c * tq, tq)
            sel = jnp.where(s_ref[pl.ds(off, tq), :] >= thr, 0.0, NEG)
            or_ref[:, pl.ds(off, tq)] = jnp.where(causal(c), sel, NEG).T.astype(or_ref.dtype)
            return 0

        lax.fori_loop(0, i + 1, out_chunk, 0)
        sel = jnp.where(s_ref[n_real:, :] >= thr, 0.0, NEG)
        om_ref[...] = jnp.where(mrow < N_META, sel, NEG).T.astype(om_ref.dtype)

    @pl.when(any_amb)
    def _():
        cnt_gt = count(lambda s, t: s > t, thr)
        quota = jnp.where(ambiguous, kf - cnt_gt, jnp.float32(3e38))

        def tie_select(s, valid, running):
            rows = s.shape[0]
            r = lax.broadcasted_iota(jnp.int32, (rows, rows), 0)
            c_ = lax.broadcasted_iota(jnp.int32, (rows, rows), 1)
            lower = (c_ <= r).astype(BF16)
            eq = jnp.where(s == thr, 1.0, 0.0)
            rank = jnp.dot(lower, eq.astype(BF16), preferred_element_type=F32) + running
            keep = jnp.where(s > thr, 0.0, jnp.where(jnp.logical_and(eq > 0.0, rank <= quota), 0.0, NEG))
            return jnp.where(valid, keep, NEG), running + jnp.sum(eq, axis=0, keepdims=True)

        bias_m, running = tie_select(s_ref[n_real:, :], mrow < N_META, jnp.zeros((1, tq), F32))
        om_ref[...] = bias_m.T.astype(om_ref.dtype)

        def out_chunk(c, running):
            off = pl.multiple_of(c * tq, tq)
            bias, running = tie_select(s_ref[pl.ds(off, tq), :], causal(c), running)
            or_ref[:, pl.ds(off, tq)] = bias.T.astype(or_ref.dtype)
            return running

        lax.fori_loop(0, i + 1, out_chunk, running)


def _select_bias(qi_hm, ki_real, ki_meta, wt, topk, tq):
    _, t, _ = qi_hm.shape
    return pl.pallas_call(
        functools.partial(_select_kernel, tq=tq, topk=topk),
        grid=(t // tq,),
        in_specs=[pl.BlockSpec((IDX_HEADS, tq, IDX_DIM), lambda i: (0, i, 0)),
                  pl.BlockSpec((t, IDX_DIM), lambda i: (0, 0)),
                  pl.BlockSpec((META_PAD, IDX_DIM), lambda i: (0, 0)),
                  pl.BlockSpec((IDX_HEADS, tq), lambda i: (0, i))],
        out_specs=[pl.BlockSpec((tq, t), lambda i: (i, 0)), pl.BlockSpec((tq, META_PAD), lambda i: (i, 0))],
        out_shape=[jax.ShapeDtypeStruct((t, t), BF16), jax.ShapeDtypeStruct((t, META_PAD), BF16)],
        scratch_shapes=[pltpu.VMEM((t + META_PAD, tq), F32)],
        compiler_params=_cparams(("parallel",)),
        name="select_bias",
    )(qi_hm, ki_real, ki_meta, wt)


def _attn_kernel(*refs, t, fox):
    if fox:
        (qi_ref, kj_ref, q_ref, k_ref, v_ref, km_ref, vm_ref, cq_ref, ck_ref, ckm_ref,
         o_ref, m_ref, l_ref, acc_ref, r_ref) = refs
        s_ref = None
    else:
        (qi_ref, kj_ref, q_ref, k_ref, v_ref, km_ref, vm_ref, b_ref, bm_ref,
         o_ref, m_ref, l_ref, acc_ref, s_ref) = refs
    stage_logits = s_ref is not None
    step = pl.program_id(0)
    i = qi_ref[step]
    j = kj_ref[step]
    nheads = q_ref.shape[1] // HEAD_DIM

    def block(k_blk_ref, v_blk_ref, bias_fn):
        width = k_blk_ref.shape[0]

        def logits(h):
            sl = slice(h * HEAD_DIM, (h + 1) * HEAD_DIM)
            return lax.dot_general(q_ref[:, sl], k_blk_ref[:, sl], NT_DIMS, preferred_element_type=F32)

        if stage_logits:
            for h in range(nheads):
                s_ref[h, :, :width] = logits(h)
        s_next = None if stage_logits else logits(0)
        for h in range(nheads):
            sl = slice(h * HEAD_DIM, (h + 1) * HEAD_DIM)
            if stage_logits:
                s = s_ref[h, :, :width]
            else:
                s, s_next = s_next, (logits(h + 1) if h + 1 < nheads else None)
            u = bias_fn(s, h)
            m_prev = m_ref[h]
            mx = jnp.max(u, axis=1, keepdims=True)
            if fox:
                r = r_ref[h]
                m_next = jnp.maximum(m_prev, mx + r)
                sub = m_next - r
            else:
                m_next = jnp.maximum(m_prev, mx)
                sub = m_next
            alpha = jnp.exp2(m_prev - m_next)
            ps = [jnp.exp2(u[:, c * LANES:(c + 1) * LANES] - sub) for c in range(width // LANES)]
            psum = ps[0]
            for pc in ps[1:]:
                psum = psum + pc
            p = ps[0] if len(ps) == 1 else jnp.concatenate(ps, axis=1)
            l_ref[h] = alpha * l_ref[h] + psum
            m_ref[h] = m_next
            acc_ref[h] = alpha * acc_ref[h] + jnp.dot(p.astype(BF16), v_blk_ref[:, sl],
                                                      preferred_element_type=F32)

    @pl.when(j == 0)
    def _():
        m_ref[...] = jnp.full(m_ref.shape, NEG, F32)
        l_ref[...] = jnp.zeros(l_ref.shape, F32)
        acc_ref[...] = jnp.zeros(acc_ref.shape, F32)
        mcol = lax.broadcasted_iota(jnp.int32, (t, META_PAD), 1)
        if fox:
            for h in range(nheads):
                r_ref[h] = jnp.broadcast_to(cq_ref[:, h:h + 1] * LOG2E, (t, LANES))

            def bias_fn(u, h):
                return jnp.where(mcol < N_META, u - ckm_ref[h:h + 1, :] * LOG2E, NEG)
        else:
            bias_m = bm_ref[...].astype(F32)

            def bias_fn(u, h):
                return u + bias_m
        block(km_ref, vm_ref, bias_fn)

    if fox:
        @pl.when(j < i)
        def _():
            block(k_ref, v_ref, lambda u, h: u - ck_ref[h:h + 1, :] * LOG2E)

        @pl.when(j == i)
        def _():
            row = lax.broadcasted_iota(jnp.int32, (t, t), 0)
            col = lax.broadcasted_iota(jnp.int32, (t, t), 1)
            block(k_ref, v_ref, lambda u, h: jnp.where(col <= row, u - ck_ref[h:h + 1, :] * LOG2E, NEG))
    else:
        bias = b_ref[...].astype(F32)
        block(k_ref, v_ref, lambda u, h: u + bias)

    @pl.when(j == i)
    def _():
        for h in range(nheads):
            sl = slice(h * HEAD_DIM, (h + 1) * HEAD_DIM)
            o_ref[:, sl] = (acc_ref[h] / jnp.sum(l_ref[h], axis=1, keepdims=True)).astype(o_ref.dtype)


def _attention(qkv, meta_kv, extra, *, fox, t_blk):
    q_arr, q_col, k_arr, k_col, v_arr, v_col = qkv
    km_arr, km_col, vm_arr, vm_col = meta_kv
    t = q_arr.shape[0]
    w = A_WIDTH
    nheads = w // HEAD_DIM
    nq = t // t_blk
    qi = jnp.asarray([i for i in range(nq) for _ in range(i + 1)], jnp.int32)
    kj = jnp.asarray([j for i in range(nq) for j in range(i + 1)], jnp.int32)

    in_specs = [pl.BlockSpec((t_blk, w), lambda s, qi, kj: (qi[s], q_col)),
                pl.BlockSpec((t_blk, w), lambda s, qi, kj: (kj[s], k_col)),
                pl.BlockSpec((t_blk, w), lambda s, qi, kj: (kj[s], v_col)),
                pl.BlockSpec((META_PAD, w), lambda s, qi, kj: (0, km_col)),
                pl.BlockSpec((META_PAD, w), lambda s, qi, kj: (0, vm_col))]
    scratch = [pltpu.VMEM((nheads, t_blk, LANES), F32), pltpu.VMEM((nheads, t_blk, LANES), F32),
               pltpu.VMEM((nheads, t_blk, HEAD_DIM), F32)]
    args = (q_arr, k_arr, v_arr, km_arr, vm_arr)
    if fox:
        cq, ct_real, ct_meta = extra
        in_specs += [pl.BlockSpec((t_blk, nheads), lambda s, qi, kj: (qi[s], 0)),
                     pl.BlockSpec((nheads, t_blk), lambda s, qi, kj: (0, kj[s])),
                     pl.BlockSpec((nheads, META_PAD), lambda s, qi, kj: (0, 0))]
        args += (cq, ct_real, ct_meta)
        scratch.append(pltpu.VMEM((nheads, t_blk, LANES), F32))
    else:
        bias_real, bias_meta = extra
        in_specs += [pl.BlockSpec((t_blk, t_blk), lambda s, qi, kj: (qi[s], kj[s])),
                     pl.BlockSpec((t_blk, META_PAD), lambda s, qi, kj: (qi[s], 0))]
        args += (bias_real, bias_meta)
        scratch.append(pltpu.VMEM((nheads, t_blk, t_blk), F32))
    return pl.pallas_call(
        functools.partial(_attn_kernel, t=t_blk, fox=fox),
        grid_spec=pltpu.PrefetchScalarGridSpec(
            num_scalar_prefetch=2,
            grid=(int(qi.shape[0]),),
            in_specs=in_specs,
            out_specs=pl.BlockSpec((t_blk, w), lambda s, qi, kj: (qi[s], 0)),
            scratch_shapes=scratch),
        out_shape=jax.ShapeDtypeStruct((t, w), BF16),
        compiler_params=_cparams(("arbitrary",)),
        name="attn_fox" if fox else "attn_dsa",
    )(qi, kj, *args)


def _route(logits):
    lane = lax.broadcasted_iota(jnp.int32, logits.shape, 1).astype(F32)
    big = jnp.float32(LANES)
    ninf = -jnp.inf
    gl = jnp.where(lane < N_GROUPS, logits, ninf)
    gmax = jnp.max(gl, axis=1, keepdims=True)
    g_sel = jnp.min(jnp.where(gl == gmax, lane, big), axis=1, keepdims=True)
    p_group = 1.0 / jnp.sum(jnp.exp(gl - gmax), axis=1, keepdims=True)
    e_lo = RT_E0 + g_sel * EXPERTS_PER_GROUP
    in_grp = jnp.logical_and(lane >= e_lo, lane < e_lo + EXPERTS_PER_GROUP)
    el = jnp.where(in_grp, logits, ninf)
    emax = jnp.max(el, axis=1, keepdims=True)
    ex = jnp.exp(el - emax)
    prob = ex / jnp.sum(ex, axis=1, keepdims=True)
    prob = jnp.where(in_grp, prob, -1.0)
    p1 = jnp.max(prob, axis=1, keepdims=True)
    i1 = jnp.min(jnp.where(prob == p1, lane, big), axis=1, keepdims=True)
    rest = jnp.where(lane == i1, -1.0, prob)
    p2 = jnp.max(rest, axis=1, keepdims=True)
    i2 = jnp.min(jnp.where(rest == p2, lane, big), axis=1, keepdims=True)
    tot = p1 + p2
    w1 = p_group * (p1 / tot)
    w2 = p_group * (p2 / tot)
    return jnp.where(lane == i1, w1, jnp.where(lane == i2, w2, 0.0)), g_sel


def _merge_kernel(h_ref, ya_ref, yb_ref, wga_ref, wgb_ref, wa_ref, wb_ref, wo_ref, x_ref, g_ref,
                  wrh_ref, wrl_ref, br_ref, h1_ref, xr_ref, plan_ref, acc_ref, cnt_ref):
    n = pl.program_id(1)

    @pl.when(jnp.logical_and(pl.program_id(0) == 0, n == 0))
    def _():
        cnt_ref[...] = jnp.zeros(cnt_ref.shape, F32)

    @pl.when(n == 0)
    def _():
        acc_ref[...] = jnp.zeros(acc_ref.shape, F32)

    h = h_ref[...]
    ga = jnp.dot(h, wga_ref[...], preferred_element_type=F32)
    gb = jnp.dot(h, wgb_ref[...], preferred_element_type=F32)
    pa = jnp.dot(ya_ref[...], wa_ref[...], preferred_element_type=F32)
    pb = jnp.dot(yb_ref[...], wb_ref[...], preferred_element_type=F32)
    merged = jax.nn.sigmoid(ga) * pa + jax.nn.sigmoid(gb) * pb
    acc_ref[...] += jnp.dot(merged.astype(BF16), wo_ref[...], preferred_element_type=F32)

    @pl.when(n == pl.num_programs(1) - 1)
    def _():
        h1 = x_ref[...] + acc_ref[...]
        h1_ref[...] = h1
        ms = jnp.mean(h1 * h1, axis=-1, keepdims=True)
        t = h1 * lax.rsqrt(ms + EPS) * g_ref[...]
        t_hi = t.astype(BF16)
        t_lo = (t - t_hi.astype(F32)).astype(BF16)
        logits = (jnp.dot(t_hi, wrh_ref[...], preferred_element_type=F32)
                  + jnp.dot(t_hi, wrl_ref[...], preferred_element_type=F32)
                  + jnp.dot(t_lo, wrh_ref[...], preferred_element_type=F32)) + br_ref[...]
        gates, g_sel = _route(logits)
        d = t.shape[1]
        xr_ref[:, :d] = t
        xr_ref[:, d:] = gates
        tm = t.shape[0]
        lane = lax.broadcasted_iota(jnp.int32, gates.shape, 1).astype(F32)
        onehot = jnp.where(lane == g_sel, 1.0, 0.0).astype(BF16)
        rr = lax.broadcasted_iota(jnp.int32, (tm, tm), 0)
        cc = lax.broadcasted_iota(jnp.int32, (tm, tm), 1)
        cum = jnp.dot((cc <= rr).astype(BF16), onehot, preferred_element_type=F32) + cnt_ref[0:1, :]
        cnt_ref[...] = jnp.broadcast_to(cum[tm - 1:tm, :], cnt_ref.shape)
        plan_ref[...] = jnp.where(lane < N_GROUPS, cum, jnp.where(lane == N_GROUPS, g_sel, 0.0))


def _merge(h, ya, yb, w_all, ga_col0, gb_col0, wa, wb, wo, x, g_ffn, wr_hi, wr_lo, br, tm, tn):
    t, d = h.shape
    row = lambda i, n: (i, 0)
    ga0, gb0 = ga_col0 // tn, gb_col0 // tn
    wga = wgb = w_all
    return pl.pallas_call(
        _merge_kernel,
        grid=(t // tm, d // tn),
        in_specs=[pl.BlockSpec((tm, d), row),
                  pl.BlockSpec((tm, A_WIDTH), row),
                  pl.BlockSpec((tm, B_WIDTH), row),
                  pl.BlockSpec((d, tn), lambda i, n: (0, ga0 + n)),
                  pl.BlockSpec((d, tn), lambda i, n: (0, gb0 + n)),
                  pl.BlockSpec((A_WIDTH, tn), lambda i, n: (0, n)),
                  pl.BlockSpec((B_WIDTH, tn), lambda i, n: (0, n)),
                  pl.BlockSpec((tn, d), lambda i, n: (n, 0)),
                  pl.BlockSpec((tm, d), row),
                  pl.BlockSpec((1, d), lambda i, n: (0, 0)),
                  pl.BlockSpec((d, LANES), lambda i, n: (0, 0)),
                  pl.BlockSpec((d, LANES), lambda i, n: (0, 0)),
                  pl.BlockSpec((1, LANES), lambda i, n: (0, 0))],
        out_specs=[pl.BlockSpec((tm, d), row), pl.BlockSpec((tm, d + LANES), row),
                   pl.BlockSpec((tm, LANES), row)],
        out_shape=[jax.ShapeDtypeStruct((t, d), F32), jax.ShapeDtypeStruct((t, d + LANES), F32),
                   jax.ShapeDtypeStruct((t, LANES), F32)],
        scratch_shapes=[pltpu.VMEM((tm, d), F32), pltpu.VMEM((8, LANES), F32)],
        compiler_params=_cparams(("arbitrary", "arbitrary")),
        name="merge_out_router",
    )(h, ya, yb, wga, wgb, wa, wb, wo, x, g_ffn.reshape(1, d), wr_hi, wr_lo, br)


def _row_copy(src_ref, src_row, dst_ref, dst_row, sem):
    return pltpu.make_async_copy(src_ref.at[pl.ds(src_row, 1), :], dst_ref.at[pl.ds(dst_row, 1), :], sem)


def _dispatch_kernel(pos_ref, x_ref, init_ref, xs_ref, sem):
    del init_ref
    i = pl.program_id(0)
    tm = x_ref.shape[0]

    def issue(r, _):
        _row_copy(x_ref, r, xs_ref, pos_ref[i * tm + r], sem).start()
        return 0

    lax.fori_loop(0, tm, issue, 0)

    def drain(r, _):
        _row_copy(x_ref, r, xs_ref, pos_ref[i * tm + r], sem).wait()
        return 0

    lax.fori_loop(0, tm, drain, 0)


def _dispatch(pos, xr, n_slots, tm):
    t, w = xr.shape
    return pl.pallas_call(
        _dispatch_kernel,
        grid_spec=pltpu.PrefetchScalarGridSpec(
            num_scalar_prefetch=1,
            grid=(t // tm,),
            in_specs=[pl.BlockSpec((tm, w), lambda i, pos: (i, 0)),
                      pl.BlockSpec(memory_space=pl.ANY)],
            out_specs=pl.BlockSpec(memory_space=pl.ANY),
            scratch_shapes=[pltpu.SemaphoreType.DMA]),
        out_shape=jax.ShapeDtypeStruct((n_slots, w), xr.dtype),
        input_output_aliases={2: 0},
        compiler_params=_cparams(("arbitrary",)),
        name="moe_dispatch",
    )(pos, xr, jnp.zeros((n_slots, w), xr.dtype))


def _group_ffn_kernel(tg_ref, tv_ref, x_ref, wg_ref, wu_ref, wd_ref, o_ref, acc_ref, xb_ref):
    j = pl.program_id(0)
    k = pl.program_id(1)
    d = o_ref.shape[1]

    @pl.when(k == 0)
    def _():
        acc_ref[...] = jnp.zeros(acc_ref.shape, F32)
        xb_ref[...] = x_ref[:, :d].astype(BF16)

    @pl.when(tv_ref[j] > 0)
    def _():
        gates = x_ref[:, d:]
        lane = lax.broadcasted_iota(jnp.int32, gates.shape, 1)
        e_lane = RT_E0 + tg_ref[j] * EXPERTS_PER_GROUP + k
        g_e = jnp.sum(jnp.where(lane == e_lane, gates, 0.0), axis=1, keepdims=True)
        xb = xb_ref[...]
        hg = jnp.dot(xb, wg_ref[0], preferred_element_type=F32)
        hu = jnp.dot(xb, wu_ref[0], preferred_element_type=F32)
        act = jax.nn.silu(hg) * hu * g_e
        acc_ref[...] += jnp.dot(act.astype(BF16), wd_ref[0], preferred_element_type=F32)

    @pl.when(k == pl.num_programs(1) - 1)
    def _():
        o_ref[...] = acc_ref[...]


def _group_ffn(tile_group, tile_valid, xs, wg, wu, wd, ts):
    n_slots, w = xs.shape
    d = w - LANES
    f = wg.shape[2]
    expert = lambda j, k, tg, tv: (tg[j] * EXPERTS_PER_GROUP + k, 0, 0)
    return pl.pallas_call(
        _group_ffn_kernel,
        grid_spec=pltpu.PrefetchScalarGridSpec(
            num_scalar_prefetch=2,
            grid=(n_slots // ts, EXPERTS_PER_GROUP),
            in_specs=[pl.BlockSpec((ts, w), lambda j, k, tg, tv: (j, 0)),
                      pl.BlockSpec((1, d, f), expert),
                      pl.BlockSpec((1, d, f), expert),
                      pl.BlockSpec((1, f, d), expert)],
            out_specs=pl.BlockSpec((ts, d), lambda j, k, tg, tv: (j, 0)),
            scratch_shapes=[pltpu.VMEM((ts, d), F32), pltpu.VMEM((ts, d), BF16)]),
        out_shape=jax.ShapeDtypeStruct((n_slots, d), F32),
        compiler_params=_cparams(("arbitrary", "arbitrary")),
        name="moe_group_ffn",
    )(tile_group, tile_valid, xs, wg, wu, wd)


def _combine_kernel(pos_ref, ys_ref, h1_ref, gf_ref, o_ref, ybuf_ref, sem):
    i = pl.program_id(0)
    tm = h1_ref.shape[0]

    def issue(r, _):
        _row_copy(ys_ref, pos_ref[i * tm + r], ybuf_ref, r, sem).start()
        return 0

    lax.fori_loop(0, tm, issue, 0)

    def drain(r, _):
        _row_copy(ys_ref, pos_ref[i * tm + r], ybuf_ref, r, sem).wait()
        return 0

    lax.fori_loop(0, tm, drain, 0)
    h2 = h1_ref[...] + ybuf_ref[...]
    ms = jnp.mean(h2 * h2, axis=-1, keepdims=True)
    o_ref[...] = h2 * lax.rsqrt(ms + EPS) * gf_ref[...]


def _combine(pos, ys, h1, g_final, tm):
    t, d = h1.shape
    return pl.pallas_call(
        _combine_kernel,
        grid_spec=pltpu.PrefetchScalarGridSpec(
            num_scalar_prefetch=1,
            grid=(t // tm,),
            in_specs=[pl.BlockSpec(memory_space=pl.ANY),
                      pl.BlockSpec((tm, d), lambda i, pos: (i, 0)),
                      pl.BlockSpec((1, d), lambda i, pos: (0, 0))],
            out_specs=pl.BlockSpec((tm, d), lambda i, pos: (i, 0)),
            scratch_shapes=[pltpu.VMEM((tm, d), F32), pltpu.SemaphoreType.DMA]),
        out_shape=jax.ShapeDtypeStruct((t, d), F32),
        compiler_params=_cparams(("arbitrary",)),
        name="moe_combine_final",
    )(pos, ys, h1, g_final.reshape(1, d))


def _group_plan(plan, ts):
    t = plan.shape[0]
    g_sel = plan[:, N_GROUPS].astype(jnp.int32)
    cum_sel = jnp.take_along_axis(plan[:, :N_GROUPS], g_sel[:, None], axis=1)[:, 0]
    counts = plan[t - 1, :N_GROUPS].astype(jnp.int32)
    padded = (counts + ts - 1) // ts * ts
    ends = jnp.cumsum(padded)
    starts = ends - padded
    pos = starts[g_sel] + cum_sel.astype(jnp.int32) - 1
    n_tiles = (t + N_GROUPS * (ts - 1)) // ts
    tile_start = jnp.arange(n_tiles, dtype=jnp.int32) * ts
    tile_group = jnp.minimum(jnp.sum(tile_start[:, None] >= ends[None, :], axis=1), N_GROUPS - 1)
    tile_valid = (tile_start < ends[N_GROUPS - 1]).astype(jnp.int32)
    return pos, tile_group.astype(jnp.int32), tile_valid, n_tiles * ts


def _pick(n, pref):
    return pref if n % pref == 0 else n


def _pad_rows(a, rows):
    return jnp.pad(a, ((0, rows - a.shape[0]), (0, 0)))


def _layer(x, meta, g_mix, w_in, g_kv, w_kv_up, g_idx_k, b_f, w_branch_a, w_branch_b, w_out,
           g_ffn, w_group, b_group, w_expert, b_expert, w_gate_e, w_up_e, w_down_e, g_final):
    t, d = x.shape
    n_all = t + N_META
    topk = min(TOPK_MAX, n_all // 4)

    o = [0]
    for wdt in (A_WIDTH, A_KV_RANK, IDX_HEADS * IDX_DIM, IDX_DIM, IDX_HEADS, B_WIDTH, B_WIDTH, B_WIDTH,
                B_HEADS, D_MODEL, D_MODEL):
        o.append(o[-1] + wdt)
    def section(lo, hi):
        return w_in[:, o[lo]:o[hi]]

    small_pad = jnp.zeros((d, LANES - (o[5] - o[3]) - B_HEADS), F32)
    w_all = jnp.concatenate([section(0, 1), section(2, 3), section(5, 8), section(9, 10), section(10, 11),
                             section(1, 2), section(3, 5), section(8, 9), small_pad], axis=1).astype(BF16)
    col_qa = 0
    col_qi = col_qa + A_WIDTH
    col_b = col_qi + IDX_HEADS * IDX_DIM
    col_ga = col_b + 3 * B_WIDTH
    col_gb = col_ga + D_MODEL
    col_ckv = col_gb + D_MODEL
    col_small = col_ckv + A_KV_RANK
    w_up_bf = w_kv_up.astype(BF16)
    g_idx_pad = jnp.zeros((1, LANES), F32).at[0, SM_KI:SM_KI + IDX_DIM].set(g_idx_k)
    bf_pad = jnp.zeros((1, LANES), F32).at[0, SM_FB:SM_FB + B_HEADS].set(b_f)

    pos_meta = jnp.arange(N_META)
    pos_real = jnp.arange(N_META, n_all)
    cos64_m, sin64_m = _rope_tables(pos_meta, HEAD_DIM // 2)
    cos64_r, sin64_r = _rope_tables(pos_real, HEAD_DIM // 2)
    cos32_m, sin32_m = _rope_tables(pos_meta, IDX_DIM // 2)
    cos32_r, sin32_r = _rope_tables(pos_real, IDX_DIM // 2)

    tm = _pick(t, 512)

    hm = _rmsnorm(meta, g_mix, N_META)
    ka_m, va_m = _proj_kv(hm, w_all, col_ckv, g_kv, w_up_bf, cos64_m, sin64_m, N_META)
    small_m = _proj_small(hm, w_all, col_small, g_idx_pad, bf_pad, jnp.zeros((8, LANES), F32), cos32_m, sin32_m,
                          N_META)
    b_m = _proj_plain(hm, w_all, col_b, 3 * B_WIDTH, N_META, B_WIDTH)
    c_carry = jnp.broadcast_to(small_m[N_META - 1:N_META, :], (8, LANES))

    h = _rmsnorm(x, g_mix, tm)
    qk_scale = HEAD_DIM ** -0.5 * LOG2E
    qa = _proj_rope(h, w_all, col_qa, A_WIDTH, cos64_r, sin64_r, HEAD_DIM // 2, tm, A_WIDTH, out_scale=qk_scale)
    ka_r, va_r = _proj_kv(h, w_all, col_ckv, g_kv, w_up_bf, cos64_r, sin64_r, tm)
    qi_hm = _proj_rope(h, w_all, col_qi, IDX_HEADS * IDX_DIM, cos32_r, sin32_r, IDX_DIM // 2, tm,
                       IDX_HEADS * IDX_DIM, head_major=True)
    small_r = _proj_small(h, w_all, col_small, g_idx_pad, bf_pad, c_carry, cos32_r, sin32_r, tm)
    b_r = _proj_plain(h, w_all, col_b, 3 * B_WIDTH, tm, B_WIDTH, first_tile_scale=qk_scale)

    ki_real = small_r[:, SM_KI:SM_KI + IDX_DIM].astype(BF16)
    ki_meta = _pad_rows(small_m[:, SM_KI:SM_KI + IDX_DIM], META_PAD).astype(BF16)
    wt = small_r[:, SM_WI:SM_WI + IDX_HEADS].T
    cq = small_r[:, SM_FB:SM_FB + B_HEADS]
    ct_meta = _pad_rows(small_m[:, SM_FB:SM_FB + B_HEADS], META_PAD).T

    bias_real, bias_meta = _select_bias(qi_hm, ki_real, ki_meta, wt, topk, _pick(t, 256))
    ya = _attention((qa, 0, ka_r, 0, va_r, 0),
                    (_pad_rows(ka_m, META_PAD), 0, _pad_rows(va_m, META_PAD), 0),
                    (bias_real, bias_meta), fox=False, t_blk=_pick(t, 512))
    b_m_pad = _pad_rows(b_m, META_PAD)
    yb = _attention((b_r, 0, b_r, 1, b_r, 2), (b_m_pad, 1, b_m_pad, 2),
                    (cq, cq.T, ct_meta), fox=True, t_blk=_pick(t, 512))

    w_r = jnp.concatenate([w_group, w_expert, jnp.zeros((d, LANES - N_GROUPS - N_EXPERTS), F32)], axis=1)
    wr_hi = w_r.astype(BF16)
    wr_lo = (w_r - wr_hi.astype(F32)).astype(BF16)
    b_r_pad = jnp.concatenate([b_group, b_expert, jnp.zeros((LANES - N_GROUPS - N_EXPERTS,), F32)]).reshape(1, LANES)
    h1, xr, plan = _merge(h, ya, yb, w_all, col_ga, col_gb, w_branch_a.astype(BF16), w_branch_b.astype(BF16),
                          w_out.astype(BF16), x, g_ffn, wr_hi, wr_lo, b_r_pad, tm, 256)

    ts = _pick(t, 512)
    pos, tile_group, tile_valid, n_slots = _group_plan(plan, ts)
    xs = _dispatch(pos, xr, n_slots, _pick(t, 256))
    ys = _group_ffn(tile_group, tile_valid, xs, w_gate_e.astype(BF16), w_up_e.astype(BF16),
                    w_down_e.astype(BF16), ts)
    return _combine(pos, ys, h1, g_final, _pick(t, 256))


def kernel(x, meta_tokens, g_mix, w_in, g_kv, w_kv_up, g_idx_k, b_f, w_branch_a, w_branch_b, w_out, g_ffn,
           w_group, b_group, w_expert, b_expert, w_gate_e, w_up_e, w_down_e, g_final):
    assert x.shape[0] == 1 and g_mix.shape[0] == 1, "single batch, single layer"
    out = _layer(x[0], meta_tokens.astype(x.dtype), g_mix[0], w_in[0], g_kv[0], w_kv_up[0], g_idx_k[0], b_f[0],
                 w_branch_a[0], w_branch_b[0], w_out[0], g_ffn[0], w_group[0], b_group[0], w_expert[0],
                 b_expert[0], w_gate_e[0], w_up_e[0], w_down_e[0], g_final)
    return out[None]
```

```python
import functools

import jax
import jax.numpy as jnp
from jax import lax
from jax.experimental import pallas as pl
from jax.experimental.pallas import tpu as pltpu

F32 = jnp.float32
BF16 = jnp.bfloat16

D_MODEL = 2048
N_META = 16
HEAD_DIM = 128
ROPE_THETA = 10000.0
EPS = 1e-6
NEG = -1e30
A_HEADS = 8
A_WIDTH = A_HEADS * HEAD_DIM
A_KV_RANK = 512
IDX_HEADS = 16
IDX_DIM = 64
TOPK_MAX = 256
B_HEADS = 8
B_WIDTH = B_HEADS * HEAD_DIM
N_GROUPS = 4
EXPERTS_PER_GROUP = 4
N_EXPERTS = N_GROUPS * EXPERTS_PER_GROUP
D_FF_EXPERT = 512

LANES = 128
META_PAD = LANES
VMEM_LIMIT = 56 * 1024 * 1024
LOG2E = 1.4426950408889634
NT_DIMS = (((1,), (1,)), ((), ()))

SM_KI = 0
SM_WI = IDX_DIM
SM_FB = IDX_DIM + IDX_HEADS
RT_E0 = N_GROUPS


def _cparams(sem):
    return pltpu.CompilerParams(dimension_semantics=sem, vmem_limit_bytes=VMEM_LIMIT)


def _rms_kernel(x_ref, g_ref, o_ref):
    x = x_ref[...]
    ms = jnp.mean(x * x, axis=-1, keepdims=True)
    o_ref[...] = (x * lax.rsqrt(ms + EPS) * g_ref[...]).astype(o_ref.dtype)


def _rmsnorm(x, g, tm):
    n, d = x.shape
    return pl.pallas_call(
        _rms_kernel,
        grid=(n // tm,),
        in_specs=[pl.BlockSpec((tm, d), lambda i: (i, 0)), pl.BlockSpec((1, d), lambda i: (0, 0))],
        out_specs=pl.BlockSpec((tm, d), lambda i: (i, 0)),
        out_shape=jax.ShapeDtypeStruct((n, d), BF16),
        compiler_params=_cparams(("parallel",)),
        name="rmsnorm",
    )(x, g.reshape(1, d))


def _rope_tile(y, cos, sin_signed, half):
    if 2 * half == LANES:
        partner = pltpu.roll(y, half, axis=1)
    else:
        lane = lax.broadcasted_iota(jnp.int32, y.shape, 1)
        first = (lane % (2 * half)) < half
        partner = jnp.where(first, pltpu.roll(y, LANES - half, axis=1), pltpu.roll(y, half, axis=1))
    return y * cos + partner * sin_signed


def _rope_tables(pos, half):
    inv_freq = ROPE_THETA ** (-jnp.arange(half, dtype=F32) / half)
    ang = pos.astype(F32)[:, None] * inv_freq[None, :]
    cos, sin = jnp.cos(ang), jnp.sin(ang)
    reps = LANES // (2 * half)
    cos_t = jnp.tile(jnp.concatenate([cos, cos], axis=1), (1, reps))
    sin_t = jnp.tile(jnp.concatenate([-sin, sin], axis=1), (1, reps))
    return cos_t, sin_t


def _pack_kernel(x_ref, tail_ref, o_ref, *, plan, tail_tile):
    lane = lax.broadcasted_iota(jnp.int32, (x_ref.shape[0], LANES), 1)

    def src(a):
        return tail_ref[...] if a == tail_tile else x_ref[:, a * LANES:(a + 1) * LANES]

    for j, pieces in enumerate(plan):
        out = jnp.zeros(lane.shape, F32)
        for a, shift, lo, hi in pieces:
            v = pltpu.roll(src(a), shift, axis=1) if shift else src(a)
            out = v if (lo, hi) == (0, LANES) else jnp.where(jnp.logical_and(lane >= lo, lane < hi), v, out)
        o_ref[:, j * LANES:(j + 1) * LANES] = out.astype(o_ref.dtype)


def _pack_w_in(w, sections, small_pieces, rows_blk=256):
    d, ncols = w.shape
    plan = []
    for lo, hi in sections:
        assert (hi - lo) % LANES == 0
        for c in range(lo, hi, LANES):
            a, r = divmod(c, LANES)
            plan.append([(a, 0, 0, LANES)] if r == 0 else
                        [(a, LANES - r, 0, LANES - r), (a + 1, LANES - r, LANES - r, LANES)])
    pieces, dst = [], 0
    for lo, hi in small_pieces:
        a, r = divmod(lo, LANES)
        assert r + (hi - lo) <= LANES, "a small piece must not straddle a source tile"
        pieces.append((a, (dst - r) % LANES, dst, dst + hi - lo))
        dst += hi - lo
    plan.append(pieces)
    tail_tile = ncols // LANES if ncols % LANES else -1
    tail = jnp.pad(w[:, (ncols // LANES) * LANES:], ((0, 0), (0, (-ncols) % LANES))) if ncols % LANES else \
        jnp.zeros((d, LANES), w.dtype)
    n_out = len(plan) * LANES
    return pl.pallas_call(
        functools.partial(_pack_kernel, plan=plan, tail_tile=tail_tile),
        grid=(d // rows_blk,),
        in_specs=[pl.BlockSpec((rows_blk, ncols), lambda i: (i, 0)),
                  pl.BlockSpec((rows_blk, LANES), lambda i: (i, 0))],
        out_specs=pl.BlockSpec((rows_blk, n_out), lambda i: (i, 0)),
        out_shape=jax.ShapeDtypeStruct((d, n_out), BF16),
        compiler_params=_cparams(("parallel",)),
        name="pack_w_in",
    )(w, tail)


def _proj_plain_kernel(h_ref, w_ref, o_ref, *, first_tile_scale):
    y = jnp.dot(h_ref[...], w_ref[...], preferred_element_type=F32)
    scale = jnp.where(pl.program_id(1) == 0, jnp.float32(first_tile_scale), jnp.float32(1.0))
    o_ref[...] = (y * scale).astype(o_ref.dtype)


def _proj_plain(h, w, col0, nc, tm, tn, first_tile_scale=1.0):
    n, d = h.shape
    c0 = col0 // tn
    return pl.pallas_call(
        functools.partial(_proj_plain_kernel, first_tile_scale=first_tile_scale),
        grid=(n // tm, nc // tn),
        in_specs=[pl.BlockSpec((tm, d), lambda i, j: (i, 0)), pl.BlockSpec((d, tn), lambda i, j: (0, c0 + j))],
        out_specs=pl.BlockSpec((tm, tn), lambda i, j: (i, j)),
        out_shape=jax.ShapeDtypeStruct((n, nc), BF16),
        compiler_params=_cparams(("parallel", "parallel")),
        name="proj_plain",
    )(h, w)


def _proj_rope_kernel(h_ref, w_ref, cos_ref, sin_ref, o_ref, *, half, head_major, out_scale):
    y = jnp.dot(h_ref[...], w_ref[...], preferred_element_type=F32)
    cos, sin = cos_ref[...], sin_ref[...]
    per_tile = LANES // (2 * half)
    for c in range(y.shape[1] // LANES):
        sl = slice(c * LANES, (c + 1) * LANES)
        roped = (_rope_tile(y[:, sl], cos, sin, half) * out_scale).astype(o_ref.dtype)
        if head_major:
            for k in range(per_tile):
                o_ref[c * per_tile + k] = roped[:, k * 2 * half:(k + 1) * 2 * half]
        else:
            o_ref[:, sl] = roped


def _proj_rope(h, w, col0, nc, cos, sin, half, tm, tn, head_major=False, out_scale=1.0):
    n, d = h.shape
    c0 = col0 // tn
    if head_major:
        assert tn == nc
        heads = nc // (2 * half)
        out_spec = pl.BlockSpec((heads, tm, 2 * half), lambda i, j: (0, i, 0))
        out_shape = jax.ShapeDtypeStruct((heads, n, 2 * half), BF16)
    else:
        out_spec = pl.BlockSpec((tm, tn), lambda i, j: (i, j))
        out_shape = jax.ShapeDtypeStruct((n, nc), BF16)
    return pl.pallas_call(
        functools.partial(_proj_rope_kernel, half=half, head_major=head_major, out_scale=out_scale),
        grid=(n // tm, nc // tn),
        in_specs=[pl.BlockSpec((tm, d), lambda i, j: (i, 0)), pl.BlockSpec((d, tn), lambda i, j: (0, c0 + j)),
                  pl.BlockSpec((tm, LANES), lambda i, j: (i, 0)), pl.BlockSpec((tm, LANES), lambda i, j: (i, 0))],
        out_specs=out_spec,
        out_shape=out_shape,
        compiler_params=_cparams(("parallel", "parallel")),
        name="proj_rope",
    )(h, w, cos, sin)


def _kv_kernel(h_ref, wc_ref, g_ref, wup_ref, cos_ref, sin_ref, ka_ref, va_ref):
    c = jnp.dot(h_ref[...], wc_ref[...], preferred_element_type=F32)
    ms = jnp.mean(c * c, axis=-1, keepdims=True)
    cn = (c * lax.rsqrt(ms + EPS) * g_ref[...]).astype(BF16)
    kv = jnp.dot(cn, wup_ref[...], preferred_element_type=F32)
    cos, sin = cos_ref[...], sin_ref[...]
    for hd in range(A_HEADS):
        sl = slice(hd * HEAD_DIM, (hd + 1) * HEAD_DIM)
        ka_ref[:, sl] = _rope_tile(kv[:, sl], cos, sin, HEAD_DIM // 2).astype(ka_ref.dtype)
    va_ref[...] = kv[:, A_WIDTH:].astype(va_ref.dtype)


def _proj_kv(h, wc, col0, g_kv, wup, cos, sin, tm):
    n, d = h.shape
    c0 = col0 // A_KV_RANK
    return pl.pallas_call(
        _kv_kernel,
        grid=(n // tm,),
        in_specs=[pl.BlockSpec((tm, d), lambda i: (i, 0)),
                  pl.BlockSpec((d, A_KV_RANK), lambda i: (0, c0)),
                  pl.BlockSpec((1, A_KV_RANK), lambda i: (0, 0)),
                  pl.BlockSpec((A_KV_RANK, 2 * A_WIDTH), lambda i: (0, 0)),
                  pl.BlockSpec((tm, LANES), lambda i: (i, 0)),
                  pl.BlockSpec((tm, LANES), lambda i: (i, 0))],
        out_specs=[pl.BlockSpec((tm, A_WIDTH), lambda i: (i, 0)), pl.BlockSpec((tm, A_WIDTH), lambda i: (i, 0))],
        out_shape=[jax.ShapeDtypeStruct((n, A_WIDTH), BF16), jax.ShapeDtypeStruct((n, A_WIDTH), BF16)],
        compiler_params=_cparams(("parallel",)),
        name="proj_kv",
    )(h, wc, g_kv.reshape(1, A_KV_RANK), wup, cos, sin)


def _small_kernel(h_ref, w_ref, g_ref, bf_ref, c0_ref, cos_ref, sin_ref, o_ref, carry_ref):
    tm = h_ref.shape[0]

    @pl.when(pl.program_id(0) == 0)
    def _():
        carry_ref[...] = c0_ref[...]

    y = jnp.dot(h_ref[...], w_ref[...], preferred_element_type=F32)
    lane = lax.broadcasted_iota(jnp.int32, y.shape, 1)
    is_ki = lane < SM_WI
    ms = jnp.sum(jnp.where(is_ki, y * y, 0.0), axis=-1, keepdims=True) * (1.0 / IDX_DIM)
    ki = _rope_tile(y * lax.rsqrt(ms + EPS) * g_ref[...], cos_ref[...], sin_ref[...], IDX_DIM // 2)
    wi = y * (IDX_HEADS ** -0.5 * IDX_DIM ** -0.5)
    z = y + bf_ref[...]
    logf = -(jnp.maximum(-z, 0.0) + jnp.log(1.0 + jnp.exp(-jnp.abs(z))))
    hi = logf.astype(BF16)
    r1 = logf - hi.astype(F32)
    mid = r1.astype(BF16)
    lo = (r1 - mid.astype(F32)).astype(BF16)
    row = lax.broadcasted_iota(jnp.int32, (tm, tm), 0)
    col = lax.broadcasted_iota(jnp.int32, (tm, tm), 1)
    tri = (col <= row).astype(BF16)
    csum = (jnp.dot(tri, hi, preferred_element_type=F32) + jnp.dot(tri, mid, preferred_element_type=F32)
            + jnp.dot(tri, lo, preferred_element_type=F32)) + carry_ref[0:1, :]
    carry_ref[...] = jnp.broadcast_to(csum[tm - 1:tm, :], carry_ref.shape)
    o_ref[...] = jnp.where(is_ki, ki, jnp.where(lane < SM_FB, wi, csum))


def _proj_small(h, w, col0, g_idx_pad, bf_pad, c0, cos, sin, tm):
    n, d = h.shape
    cb = col0 // LANES
    return pl.pallas_call(
        _small_kernel,
        grid=(n // tm,),
        in_specs=[pl.BlockSpec((tm, d), lambda i: (i, 0)),
                  pl.BlockSpec((d, LANES), lambda i: (0, cb)),
                  pl.BlockSpec((1, LANES), lambda i: (0, 0)),
                  pl.BlockSpec((1, LANES), lambda i: (0, 0)),
                  pl.BlockSpec((8, LANES), lambda i: (0, 0)),
                  pl.BlockSpec((tm, LANES), lambda i: (i, 0)),
                  pl.BlockSpec((tm, LANES), lambda i: (i, 0))],
        out_specs=pl.BlockSpec((tm, LANES), lambda i: (i, 0)),
        out_shape=jax.ShapeDtypeStruct((n, LANES), F32),
        scratch_shapes=[pltpu.VMEM((8, LANES), F32)],
        compiler_params=_cparams(("arbitrary",)),
        name="proj_small",
    )(h, w, g_idx_pad, bf_pad, c0, cos, sin)


COUNT_ROWS = 64


def _key_to_float(key):
    bits = key ^ ((key >> 31) & jnp.int32(0x7FFFFFFF))
    return lax.bitcast_convert_type(bits, F32)


def _select_kernel(qi_ref, kir_ref, kim_ref, wt_ref, or_ref, om_ref, s_ref, *, tq, topk):
    i = pl.program_id(0)
    n_real = or_ref.shape[1]
    n_chunks = n_real // tq
    krow = lax.broadcasted_iota(jnp.int32, (tq, tq), 0)
    qcol = lax.broadcasted_iota(jnp.int32, (tq, tq), 1)
    mrow = lax.broadcasted_iota(jnp.int32, (META_PAD, tq), 0)
    wt = wt_ref[...]

    def scores(kc):
        s = jnp.zeros((kc.shape[0], tq), F32)
        for h in range(IDX_HEADS):
            d = lax.dot_general(kc, qi_ref[h], NT_DIMS, preferred_element_type=F32)
            s = s + wt[h:h + 1, :] * jnp.maximum(d, 0.0)
        return s

    def causal(c):
        return krow + (c - i) * tq <= qcol

    def score_chunk(c, _):
        off = pl.multiple_of(c * tq, tq)
        s_ref[pl.ds(off, tq), :] = jnp.where(causal(c), scores(kir_ref[pl.ds(off, tq), :]), NEG)
        return 0

    lax.fori_loop(0, i + 1, score_chunk, 0)
    s_ref[n_real:, :] = jnp.where(mrow < N_META, scores(kim_ref[...]), NEG)

    def count(cmp, thr):
        def body(c, acc):
            off = c * tq
            for r in range(tq // COUNT_ROWS):
                blk = s_ref[pl.ds(pl.multiple_of(off + r * COUNT_ROWS, COUNT_ROWS), COUNT_ROWS), :]
                acc = acc + jnp.where(cmp(blk, thr), 1.0, 0.0)
            return acc

        acc = lax.fori_loop(0, i + 1, body, jnp.zeros((COUNT_ROWS, tq), F32))
        for r in range(META_PAD // COUNT_ROWS):
            blk = s_ref[n_real + r * COUNT_ROWS:n_real + (r + 1) * COUNT_ROWS, :]
            acc = acc + jnp.where(cmp(blk, thr), 1.0, 0.0)
        return jnp.sum(acc, axis=0, keepdims=True)

    kf = jnp.float32(topk)

    def bit_step(b, carry):
        t_key, cnt_t = carry
        cand = t_key + lax.shift_left(jnp.int32(1), 31 - b)
        cnt = count(lambda s, t: s >= t, _key_to_float(cand))
        take = cnt >= kf
        return jnp.where(take, cand, t_key), jnp.where(take, cnt, cnt_t)

    t0 = jnp.full((1, tq), jnp.iinfo(jnp.int32).min, jnp.int32)
    c0 = jnp.broadcast_to(((i + 1) * tq + META_PAD).astype(F32), (1, tq))
    t_key, cnt_t = lax.fori_loop(0, 32, bit_step, (t0, c0))
    thr = _key_to_float(t_key)

    def fill_chunk(c, _):
        or_ref[:, pl.ds(pl.multiple_of(c * tq, tq), tq)] = jnp.full((tq, tq), NEG, or_ref.dtype)
        return 0

    lax.fori_loop(i + 1, n_chunks, fill_chunk, 0)

    ambiguous = jnp.logical_and(cnt_t > kf, thr > NEG)
    any_amb = jnp.max(jnp.where(ambiguous, 1.0, 0.0)) > 0.0

    @pl.when(jnp.logical_not(any_amb))
    def _():
        def out_chunk(c, _):
            off = pl.multiple_of(c * tq, tq)
            sel = jnp.where(s_ref[pl.ds(off, tq), :] >= thr, 0.0, NEG)
            or_ref[:, pl.ds(off, tq)] = jnp.where(causal(c), sel, NEG).T.astype(or_ref.dtype)
            return 0

        lax.fori_loop(0, i + 1, out_chunk, 0)
        sel = jnp.where(s_ref[n_real:, :] >= thr, 0.0, NEG)
        om_ref[...] = jnp.where(mrow < N_META, sel, NEG).T.astype(om_ref.dtype)

    @pl.when(any_amb)
    def _():
        cnt_gt = count(lambda s, t: s > t, thr)
        quota = jnp.where(ambiguous, kf - cnt_gt, jnp.float32(3e38))

        def tie_select(s, valid, running):
            rows = s.shape[0]
            r = lax.broadcasted_iota(jnp.int32, (rows, rows), 0)
            c_ = lax.broadcasted_iota(jnp.int32, (rows, rows), 1)
            lower = (c_ <= r).astype(BF16)
            eq = jnp.where(s == thr, 1.0, 0.0)
            rank = jnp.dot(lower, eq.astype(BF16), preferred_element_type=F32) + running
            keep = jnp.where(s > thr, 0.0, jnp.where(jnp.logical_and(eq > 0.0, rank <= quota), 0.0, NEG))
            return jnp.where(valid, keep, NEG), running + jnp.sum(eq, axis=0, keepdims=True)

        bias_m, running = tie_select(s_ref[n_real:, :], mrow < N_META, jnp.zeros((1, tq), F32))
        om_ref[...] = bias_m.T.astype(om_ref.dtype)

        def out_chunk(c, running):
            off = pl.multiple_of(c * tq, tq)
            bias, running = tie_select(s_ref[pl.ds(off, tq), :], causal(c), running)
            or_ref[:, pl.ds(off, tq)] = bias.T.astype(or_ref.dtype)
            return running

        lax.fori_loop(0, i + 1, out_chunk, running)


def _select_bias(qi_hm, ki_real, ki_meta, wt, topk, tq):
    _, t, _ = qi_hm.shape
    return pl.pallas_call(
        functools.partial(_select_kernel, tq=tq, topk=topk),
        grid=(t // tq,),
        in_specs=[pl.BlockSpec((IDX_HEADS, tq, IDX_DIM), lambda i: (0, i, 0)),
                  pl.BlockSpec((t, IDX_DIM), lambda i: (0, 0)),
                  pl.BlockSpec((META_PAD, IDX_DIM), lambda i: (0, 0)),
                  pl.BlockSpec((IDX_HEADS, tq), lambda i: (0, i))],
        out_specs=[pl.BlockSpec((tq, t), lambda i: (i, 0)), pl.BlockSpec((tq, META_PAD), lambda i: (i, 0))],
        out_shape=[jax.ShapeDtypeStruct((t, t), BF16), jax.ShapeDtypeStruct((t, META_PAD), BF16)],
        scratch_shapes=[pltpu.VMEM((t + META_PAD, tq), F32)],
        compiler_params=_cparams(("parallel",)),
        name="select_bias",
    )(qi_hm, ki_real, ki_meta, wt)


def _attn_kernel(*refs, t, fox):
    if fox:
        (qi_ref, kj_ref, q_ref, k_ref, v_ref, km_ref, vm_ref, cq_ref, ck_ref, ckm_ref,
         o_ref, m_ref, l_ref, acc_ref, r_ref) = refs
        s_ref = None
    else:
        (qi_ref, kj_ref, q_ref, k_ref, v_ref, km_ref, vm_ref, b_ref, bm_ref,
         o_ref, m_ref, l_ref, acc_ref, s_ref) = refs
    stage_logits = s_ref is not None
    step = pl.program_id(0)
    i = qi_ref[step]
    j = kj_ref[step]
    nheads = q_ref.shape[1] // HEAD_DIM

    def block(k_blk_ref, v_blk_ref, bias_fn):
        width = k_blk_ref.shape[0]

        def logits(h):
            sl = slice(h * HEAD_DIM, (h + 1) * HEAD_DIM)
            return lax.dot_general(q_ref[:, sl], k_blk_ref[:, sl], NT_DIMS, preferred_element_type=F32)

        if stage_logits:
            for h in range(nheads):
                s_ref[h, :, :width] = logits(h)
        s_next = None if stage_logits else logits(0)
        for h in range(nheads):
            sl = slice(h * HEAD_DIM, (h + 1) * HEAD_DIM)
            if stage_logits:
                s = s_ref[h, :, :width]
            else:
                s, s_next = s_next, (logits(h + 1) if h + 1 < nheads else None)
            u = bias_fn(s, h)
            m_prev = m_ref[h]
            mx = jnp.max(u, axis=1, keepdims=True)
            if fox:
                r = r_ref[h]
                m_next = jnp.maximum(m_prev, mx + r)
                sub = m_next - r
            else:
                m_next = jnp.maximum(m_prev, mx)
                sub = m_next
            alpha = jnp.exp2(m_prev - m_next)
            ps = [jnp.exp2(u[:, c * LANES:(c + 1) * LANES] - sub) for c in range(width // LANES)]
            psum = ps[0]
            for pc in ps[1:]:
                psum = psum + pc
            p = ps[0] if len(ps) == 1 else jnp.concatenate(ps, axis=1)
            l_ref[h] = alpha * l_ref[h] + psum
            m_ref[h] = m_next
            acc_ref[h] = alpha * acc_ref[h] + jnp.dot(p.astype(BF16), v_blk_ref[:, sl],
                                                      preferred_element_type=F32)

    @pl.when(j == 0)
    def _():
        m_ref[...] = jnp.full(m_ref.shape, NEG, F32)
        l_ref[...] = jnp.zeros(l_ref.shape, F32)
        acc_ref[...] = jnp.zeros(acc_ref.shape, F32)
        mcol = lax.broadcasted_iota(jnp.int32, (t, META_PAD), 1)
        if fox:
            for h in range(nheads):
                r_ref[h] = jnp.broadcast_to(cq_ref[:, h:h + 1] * LOG2E, (t, LANES))

            def bias_fn(u, h):
                return jnp.where(mcol < N_META, u - ckm_ref[h:h + 1, :] * LOG2E, NEG)
        else:
            bias_m = bm_ref[...].astype(F32)

            def bias_fn(u, h):
                return u + bias_m
        block(km_ref, vm_ref, bias_fn)

    if fox:
        @pl.when(j < i)
        def _():
            block(k_ref, v_ref, lambda u, h: u - ck_ref[h:h + 1, :] * LOG2E)

        @pl.when(j == i)
        def _():
            row = lax.broadcasted_iota(jnp.int32, (t, t), 0)
            col = lax.broadcasted_iota(jnp.int32, (t, t), 1)
            block(k_ref, v_ref, lambda u, h: jnp.where(col <= row, u - ck_ref[h:h + 1, :] * LOG2E, NEG))
    else:
        bias = b_ref[...].astype(F32)
        block(k_ref, v_ref, lambda u, h: u + bias)

    @pl.when(j == i)
    def _():
        for h in range(nheads):
            sl = slice(h * HEAD_DIM, (h + 1) * HEAD_DIM)
            o_ref[:, sl] = (acc_ref[h] / jnp.sum(l_ref[h], axis=1, keepdims=True)).astype(o_ref.dtype)


def _attention(qkv, meta_kv, extra, *, fox, t_blk):
    q_arr, q_col, k_arr, k_col, v_arr, v_col = qkv
    km_arr, km_col, vm_arr, vm_col = meta_kv
    t = q_arr.shape[0]
    w = A_WIDTH
    nheads = w // HEAD_DIM
    nq = t // t_blk
    qi = jnp.asarray([i for i in range(nq) for _ in range(i + 1)], jnp.int32)
    kj = jnp.asarray([j for i in range(nq) for j in range(i + 1)], jnp.int32)

    in_specs = [pl.BlockSpec((t_blk, w), lambda s, qi, kj: (qi[s], q_col)),
                pl.BlockSpec((t_blk, w), lambda s, qi, kj: (kj[s], k_col)),
                pl.BlockSpec((t_blk, w), lambda s, qi, kj: (kj[s], v_col)),
                pl.BlockSpec((META_PAD, w), lambda s, qi, kj: (0, km_col)),
                pl.BlockSpec((META_PAD, w), lambda s, qi, kj: (0, vm_col))]
    scratch = [pltpu.VMEM((nheads, t_blk, LANES), F32), pltpu.VMEM((nheads, t_blk, LANES), F32),
               pltpu.VMEM((nheads, t_blk, HEAD_DIM), F32)]
    args = (q_arr, k_arr, v_arr, km_arr, vm_arr)
    if fox:
        cq, ct_real, ct_meta = extra
        in_specs += [pl.BlockSpec((t_blk, nheads), lambda s, qi, kj: (qi[s], 0)),
                     pl.BlockSpec((nheads, t_blk), lambda s, qi, kj: (0, kj[s])),
                     pl.BlockSpec((nheads, META_PAD), lambda s, qi, kj: (0, 0))]
        args += (cq, ct_real, ct_meta)
        scratch.append(pltpu.VMEM((nheads, t_blk, LANES), F32))
    else:
        bias_real, bias_meta = extra
        in_specs += [pl.BlockSpec((t_blk, t_blk), lambda s, qi, kj: (qi[s], kj[s])),
                     pl.BlockSpec((t_blk, META_PAD), lambda s, qi, kj: (qi[s], 0))]
        args += (bias_real, bias_meta)
        scratch.append(pltpu.VMEM((nheads, t_blk, t_blk), F32))
    return pl.pallas_call(
        functools.partial(_attn_kernel, t=t_blk, fox=fox),
        grid_spec=pltpu.PrefetchScalarGridSpec(
            num_scalar_prefetch=2,
            grid=(int(qi.shape[0]),),
            in_specs=in_specs,
            out_specs=pl.BlockSpec((t_blk, w), lambda s, qi, kj: (qi[s], 0)),
            scratch_shapes=scratch),
        out_shape=jax.ShapeDtypeStruct((t, w), BF16),
        compiler_params=_cparams(("arbitrary",)),
        name="attn_fox" if fox else "attn_dsa",
    )(qi, kj, *args)


def _route(logits):
    lane = lax.broadcasted_iota(jnp.int32, logits.shape, 1).astype(F32)
    big = jnp.float32(LANES)
    ninf = -jnp.inf
    gl = jnp.where(lane < N_GROUPS, logits, ninf)
    gmax = jnp.max(gl, axis=1, keepdims=True)
    g_sel = jnp.min(jnp.where(gl == gmax, lane, big), axis=1, keepdims=True)
    p_group = 1.0 / jnp.sum(jnp.exp(gl - gmax), axis=1, keepdims=True)
    e_lo = RT_E0 + g_sel * EXPERTS_PER_GROUP
    in_grp = jnp.logical_and(lane >= e_lo, lane < e_lo + EXPERTS_PER_GROUP)
    el = jnp.where(in_grp, logits, ninf)
    emax = jnp.max(el, axis=1, keepdims=True)
    ex = jnp.exp(el - emax)
    prob = ex / jnp.sum(ex, axis=1, keepdims=True)
    prob = jnp.where(in_grp, prob, -1.0)
    p1 = jnp.max(prob, axis=1, keepdims=True)
    i1 = jnp.min(jnp.where(prob == p1, lane, big), axis=1, keepdims=True)
    rest = jnp.where(lane == i1, -1.0, prob)
    p2 = jnp.max(rest, axis=1, keepdims=True)
    i2 = jnp.min(jnp.where(rest == p2, lane, big), axis=1, keepdims=True)
    tot = p1 + p2
    w1 = p_group * (p1 / tot)
    w2 = p_group * (p2 / tot)
    return jnp.where(lane == i1, w1, jnp.where(lane == i2, w2, 0.0)), g_sel


def _merge_kernel(h_ref, ya_ref, yb_ref, wga_ref, wgb_ref, wa_ref, wb_ref, wo_ref, x_ref, g_ref,
                  wr_ref, br_ref, h1_ref, xr_ref, plan_ref, acc_ref, cnt_ref):
    n = pl.program_id(1)

    @pl.when(jnp.logical_and(pl.program_id(0) == 0, n == 0))
    def _():
        cnt_ref[...] = jnp.zeros(cnt_ref.shape, F32)

    @pl.when(n == 0)
    def _():
        acc_ref[...] = jnp.zeros(acc_ref.shape, F32)

    h = h_ref[...]
    ga = jnp.dot(h, wga_ref[...], preferred_element_type=F32)
    gb = jnp.dot(h, wgb_ref[...], preferred_element_type=F32)
    pa = jnp.dot(ya_ref[...], wa_ref[...], preferred_element_type=F32)
    pb = jnp.dot(yb_ref[...], wb_ref[...], preferred_element_type=F32)
    merged = jax.nn.sigmoid(ga) * pa + jax.nn.sigmoid(gb) * pb
    acc_ref[...] += jnp.dot(merged.astype(BF16), wo_ref[...], preferred_element_type=F32)

    @pl.when(n == pl.num_programs(1) - 1)
    def _():
        h1 = x_ref[...] + acc_ref[...]
        h1_ref[...] = h1
        ms = jnp.mean(h1 * h1, axis=-1, keepdims=True)
        t = h1 * lax.rsqrt(ms + EPS) * g_ref[...]
        t_hi = t.astype(BF16)
        t_lo = (t - t_hi.astype(F32)).astype(BF16)
        wr = wr_ref[...]
        y = jnp.dot(t_hi, wr, preferred_element_type=F32)
        logits = (y[:, :LANES] + y[:, LANES:]
                  + jnp.dot(t_lo, wr[:, :LANES], preferred_element_type=F32)) + br_ref[...]
        gates, g_sel = _route(logits)
        d = t.shape[1]
        xr_ref[:, :d] = t
        xr_ref[:, d:] = gates
        tm = t.shape[0]
        lane = lax.broadcasted_iota(jnp.int32, gates.shape, 1).astype(F32)
        onehot = jnp.where(lane == g_sel, 1.0, 0.0).astype(BF16)
        rr = lax.broadcasted_iota(jnp.int32, (tm, tm), 0)
        cc = lax.broadcasted_iota(jnp.int32, (tm, tm), 1)
        cum = jnp.dot((cc <= rr).astype(BF16), onehot, preferred_element_type=F32) + cnt_ref[0:1, :]
        cnt_ref[...] = jnp.broadcast_to(cum[tm - 1:tm, :], cnt_ref.shape)
        plan_ref[...] = jnp.where(lane < N_GROUPS, cum, jnp.where(lane == N_GROUPS, g_sel, 0.0))


def _merge(h, ya, yb, w_all, ga_col0, gb_col0, wa, wb, wo, x, g_ffn, wr_split, br, tm, tn):
    t, d = h.shape
    row = lambda i, n: (i, 0)
    ga0, gb0 = ga_col0 // tn, gb_col0 // tn
    wga = wgb = w_all
    return pl.pallas_call(
        _merge_kernel,
        grid=(t // tm, d // tn),
        in_specs=[pl.BlockSpec((tm, d), row),
                  pl.BlockSpec((tm, A_WIDTH), row),
                  pl.BlockSpec((tm, B_WIDTH), row),
                  pl.BlockSpec((d, tn), lambda i, n: (0, ga0 + n)),
                  pl.BlockSpec((d, tn), lambda i, n: (0, gb0 + n)),
                  pl.BlockSpec((A_WIDTH, tn), lambda i, n: (0, n)),
                  pl.BlockSpec((B_WIDTH, tn), lambda i, n: (0, n)),
                  pl.BlockSpec((tn, d), lambda i, n: (n, 0)),
                  pl.BlockSpec((tm, d), row),
                  pl.BlockSpec((1, d), lambda i, n: (0, 0)),
                  pl.BlockSpec((d, 2 * LANES), lambda i, n: (0, 0)),
                  pl.BlockSpec((1, LANES), lambda i, n: (0, 0))],
        out_specs=[pl.BlockSpec((tm, d), row), pl.BlockSpec((tm, d + LANES), row),
                   pl.BlockSpec((tm, LANES), row)],
        out_shape=[jax.ShapeDtypeStruct((t, d), F32), jax.ShapeDtypeStruct((t, d + LANES), F32),
                   jax.ShapeDtypeStruct((t, LANES), F32)],
        scratch_shapes=[pltpu.VMEM((tm, d), F32), pltpu.VMEM((8, LANES), F32)],
        compiler_params=_cparams(("arbitrary", "arbitrary")),
        name="merge_out_router",
    )(h, ya, yb, wga, wgb, wa, wb, wo, x, g_ffn.reshape(1, d), wr_split, br)


def _row_copy(src_ref, src_row, dst_ref, dst_row, sem):
    return pltpu.make_async_copy(src_ref.at[pl.ds(src_row, 1), :], dst_ref.at[pl.ds(dst_row, 1), :], sem)


def _dispatch_kernel(pos_ref, x_ref, init_ref, xs_ref, sem):
    del init_ref
    i = pl.program_id(0)
    tm = x_ref.shape[0]

    def issue(r, _):
        _row_copy(x_ref, r, xs_ref, pos_ref[i * tm + r], sem).start()
        return 0

    lax.fori_loop(0, tm, issue, 0)

    def drain(r, _):
        _row_copy(x_ref, r, xs_ref, pos_ref[i * tm + r], sem).wait()
        return 0

    lax.fori_loop(0, tm, drain, 0)


def _dispatch(pos, xr, n_slots, tm):
    t, w = xr.shape
    return pl.pallas_call(
        _dispatch_kernel,
        grid_spec=pltpu.PrefetchScalarGridSpec(
            num_scalar_prefetch=1,
            grid=(t // tm,),
            in_specs=[pl.BlockSpec((tm, w), lambda i, pos: (i, 0)),
                      pl.BlockSpec(memory_space=pl.ANY)],
            out_specs=pl.BlockSpec(memory_space=pl.ANY),
            scratch_shapes=[pltpu.SemaphoreType.DMA]),
        out_shape=jax.ShapeDtypeStruct((n_slots, w), xr.dtype),
        input_output_aliases={2: 0},
        compiler_params=_cparams(("arbitrary",)),
        name="moe_dispatch",
    )(pos, xr, jnp.zeros((n_slots, w), xr.dtype))


def _group_ffn_kernel(tg_ref, tv_ref, x_ref, wg_ref, wu_ref, wd_ref, o_ref, acc_ref, xb_ref):
    j = pl.program_id(0)
    k = pl.program_id(1)
    d = o_ref.shape[1]

    @pl.when(k == 0)
    def _():
        acc_ref[...] = jnp.zeros(acc_ref.shape, F32)
        xb_ref[...] = x_ref[:, :d].astype(BF16)

    @pl.when(tv_ref[j] > 0)
    def _():
        gates = x_ref[:, d:]
        lane = lax.broadcasted_iota(jnp.int32, gates.shape, 1)
        e_lane = RT_E0 + tg_ref[j] * EXPERTS_PER_GROUP + k
        g_e = jnp.sum(jnp.where(lane == e_lane, gates, 0.0), axis=1, keepdims=True)
        xb = xb_ref[...]
        hg = jnp.dot(xb, wg_ref[0], preferred_element_type=F32)
        hu = jnp.dot(xb, wu_ref[0], preferred_element_type=F32)
        act = jax.nn.silu(hg) * hu * g_e
        acc_ref[...] += jnp.dot(act.astype(BF16), wd_ref[0], preferred_element_type=F32)

    @pl.when(k == pl.num_programs(1) - 1)
    def _():
        o_ref[...] = acc_ref[...]


def _group_ffn(tile_group, tile_valid, xs, wg, wu, wd, ts):
    n_slots, w = xs.shape
    d = w - LANES
    f = wg.shape[2]
    expert = lambda j, k, tg, tv: (tg[j] * EXPERTS_PER_GROUP + k, 0, 0)
    return pl.pallas_call(
        _group_ffn_kernel,
        grid_spec=pltpu.PrefetchScalarGridSpec(
            num_scalar_prefetch=2,
            grid=(n_slots // ts, EXPERTS_PER_GROUP),
            in_specs=[pl.BlockSpec((ts, w), lambda j, k, tg, tv: (j, 0)),
                      pl.BlockSpec((1, d, f), expert),
                      pl.BlockSpec((1, d, f), expert),
                      pl.BlockSpec((1, f, d), expert)],
            out_specs=pl.BlockSpec((ts, d), lambda j, k, tg, tv: (j, 0)),
            scratch_shapes=[pltpu.VMEM((ts, d), F32), pltpu.VMEM((ts, d), BF16)]),
        out_shape=jax.ShapeDtypeStruct((n_slots, d), F32),
        compiler_params=_cparams(("arbitrary", "arbitrary")),
        name="moe_group_ffn",
    )(tile_group, tile_valid, xs, wg, wu, wd)


def _combine_kernel(pos_ref, ys_ref, h1_ref, gf_ref, o_ref, ybuf_ref, sem):
    i = pl.program_id(0)
    tm = h1_ref.shape[0]

    def for_tile_rows(tile, fn):
        slot = tile % 2

        def body(r, _):
            fn(_row_copy(ys_ref, pos_ref[tile * tm + r], ybuf_ref.at[slot], r, sem.at[slot]))
            return 0

        lax.fori_loop(0, tm, body, 0)

    @pl.when(i == 0)
    def _():
        for_tile_rows(i, lambda c: c.start())

    @pl.when(i + 1 < pl.num_programs(0))
    def _():
        for_tile_rows(i + 1, lambda c: c.start())

    for_tile_rows(i, lambda c: c.wait())
    h2 = h1_ref[...] + ybuf_ref[i % 2]
    ms = jnp.mean(h2 * h2, axis=-1, keepdims=True)
    o_ref[...] = h2 * lax.rsqrt(ms + EPS) * gf_ref[...]


def _combine(pos, ys, h1, g_final, tm):
    t, d = h1.shape
    return pl.pallas_call(
        _combine_kernel,
        grid_spec=pltpu.PrefetchScalarGridSpec(
            num_scalar_prefetch=1,
            grid=(t // tm,),
            in_specs=[pl.BlockSpec(memory_space=pl.ANY),
                      pl.BlockSpec((tm, d), lambda i, pos: (i, 0)),
                      pl.BlockSpec((1, d), lambda i, pos: (0, 0))],
            out_specs=pl.BlockSpec((tm, d), lambda i, pos: (i, 0)),
            scratch_shapes=[pltpu.VMEM((2, tm, d), F32), pltpu.SemaphoreType.DMA((2,))]),
        out_shape=jax.ShapeDtypeStruct((t, d), F32),
        compiler_params=_cparams(("arbitrary",)),
        name="moe_combine_final",
    )(pos, ys, h1, g_final.reshape(1, d))


def _group_plan(plan, ts):
    t = plan.shape[0]
    g_sel = plan[:, N_GROUPS].astype(jnp.int32)
    cum_sel = jnp.take_along_axis(plan[:, :N_GROUPS], g_sel[:, None], axis=1)[:, 0]
    counts = plan[t - 1, :N_GROUPS].astype(jnp.int32)
    padded = (counts + ts - 1) // ts * ts
    ends = jnp.cumsum(padded)
    starts = ends - padded
    pos = starts[g_sel] + cum_sel.astype(jnp.int32) - 1
    n_tiles = (t + N_GROUPS * (ts - 1)) // ts
    tile_start = jnp.arange(n_tiles, dtype=jnp.int32) * ts
    tile_group = jnp.minimum(jnp.sum(tile_start[:, None] >= ends[None, :], axis=1), N_GROUPS - 1)
    tile_valid = (tile_start < ends[N_GROUPS - 1]).astype(jnp.int32)
    return pos, tile_group.astype(jnp.int32), tile_valid, n_tiles * ts


def _pick(n, pref):
    return pref if n % pref == 0 else n


def _pad_rows(a, rows):
    return jnp.pad(a, ((0, rows - a.shape[0]), (0, 0)))


def _layer(x, meta, g_mix, w_in, g_kv, w_kv_up, g_idx_k, b_f, w_branch_a, w_branch_b, w_out,
           g_ffn, w_group, b_group, w_expert, b_expert, w_gate_e, w_up_e, w_down_e, g_final):
    t, d = x.shape
    n_all = t + N_META
    topk = min(TOPK_MAX, n_all // 4)

    o = [0]
    for wdt in (A_WIDTH, A_KV_RANK, IDX_HEADS * IDX_DIM, IDX_DIM, IDX_HEADS, B_WIDTH, B_WIDTH, B_WIDTH,
                B_HEADS, D_MODEL, D_MODEL):
        o.append(o[-1] + wdt)
    w_all = _pack_w_in(w_in, [(o[0], o[1]), (o[2], o[3]), (o[5], o[8]), (o[9], o[10]), (o[10], o[11]),
                              (o[1], o[2])], [(o[3], o[5]), (o[8], o[9])])
    col_qa = 0
    col_qi = col_qa + A_WIDTH
    col_b = col_qi + IDX_HEADS * IDX_DIM
    col_ga = col_b + 3 * B_WIDTH
    col_gb = col_ga + D_MODEL
    col_ckv = col_gb + D_MODEL
    col_small = col_ckv + A_KV_RANK
    w_up_bf = w_kv_up.astype(BF16)
    g_idx_pad = jnp.zeros((1, LANES), F32).at[0, SM_KI:SM_KI + IDX_DIM].set(g_idx_k)
    bf_pad = jnp.zeros((1, LANES), F32).at[0, SM_FB:SM_FB + B_HEADS].set(b_f)

    pos_meta = jnp.arange(N_META)
    pos_real = jnp.arange(N_META, n_all)
    cos64_m, sin64_m = _rope_tables(pos_meta, HEAD_DIM // 2)
    cos64_r, sin64_r = _rope_tables(pos_real, HEAD_DIM // 2)
    cos32_m, sin32_m = _rope_tables(pos_meta, IDX_DIM // 2)
    cos32_r, sin32_r = _rope_tables(pos_real, IDX_DIM // 2)

    tm = _pick(t, 512)

    hm = _rmsnorm(meta, g_mix, N_META)
    ka_m, va_m = _proj_kv(hm, w_all, col_ckv, g_kv, w_up_bf, cos64_m, sin64_m, N_META)
    small_m = _proj_small(hm, w_all, col_small, g_idx_pad, bf_pad, jnp.zeros((8, LANES), F32), cos32_m, sin32_m,
                          N_META)
    b_m = _proj_plain(hm, w_all, col_b, 3 * B_WIDTH, N_META, B_WIDTH)
    c_carry = jnp.broadcast_to(small_m[N_META - 1:N_META, :], (8, LANES))

    h = _rmsnorm(x, g_mix, tm)
    qk_scale = HEAD_DIM ** -0.5 * LOG2E
    qa = _proj_rope(h, w_all, col_qa, A_WIDTH, cos64_r, sin64_r, HEAD_DIM // 2, tm, A_WIDTH, out_scale=qk_scale)
    ka_r, va_r = _proj_kv(h, w_all, col_ckv, g_kv, w_up_bf, cos64_r, sin64_r, tm)
    qi_hm = _proj_rope(h, w_all, col_qi, IDX_HEADS * IDX_DIM, cos32_r, sin32_r, IDX_DIM // 2, tm,
                       IDX_HEADS * IDX_DIM, head_major=True)
    small_r = _proj_small(h, w_all, col_small, g_idx_pad, bf_pad, c_carry, cos32_r, sin32_r, tm)
    b_r = _proj_plain(h, w_all, col_b, 3 * B_WIDTH, tm, B_WIDTH, first_tile_scale=qk_scale)

    ki_real = small_r[:, SM_KI:SM_KI + IDX_DIM].astype(BF16)
    ki_meta = _pad_rows(small_m[:, SM_KI:SM_KI + IDX_DIM], META_PAD).astype(BF16)
    wt = small_r[:, SM_WI:SM_WI + IDX_HEADS].T
    cq = small_r[:, SM_FB:SM_FB + B_HEADS]
    ct_meta = _pad_rows(small_m[:, SM_FB:SM_FB + B_HEADS], META_PAD).T

    bias_real, bias_meta = _select_bias(qi_hm, ki_real, ki_meta, wt, topk, _pick(t, 256))
    ya = _attention((qa, 0, ka_r, 0, va_r, 0),
                    (_pad_rows(ka_m, META_PAD), 0, _pad_rows(va_m, META_PAD), 0),
                    (bias_real, bias_meta), fox=False, t_blk=_pick(t, 512))
    b_m_pad = _pad_rows(b_m, META_PAD)
    yb = _attention((b_r, 0, b_r, 1, b_r, 2), (b_m_pad, 1, b_m_pad, 2),
                    (cq, cq.T, ct_meta), fox=True, t_blk=_pick(t, 512))

    w_r = jnp.concatenate([w_group, w_expert, jnp.zeros((d, LANES - N_GROUPS - N_EXPERTS), F32)], axis=1)
    wr_hi = w_r.astype(BF16)
    wr_split = jnp.concatenate([wr_hi, (w_r - wr_hi.astype(F32)).astype(BF16)], axis=1)
    b_r_pad = jnp.concatenate([b_group, b_expert, jnp.zeros((LANES - N_GROUPS - N_EXPERTS,), F32)]).reshape(1, LANES)
    h1, xr, plan = _merge(h, ya, yb, w_all, col_ga, col_gb, w_branch_a.astype(BF16), w_branch_b.astype(BF16),
                          w_out.astype(BF16), x, g_ffn, wr_split, b_r_pad, tm, 256)

    ts = _pick(t, 512)
    pos, tile_group, tile_valid, n_slots = _group_plan(plan, ts)
    xs = _dispatch(pos, xr, n_slots, _pick(t, 256))
    ys = _group_ffn(tile_group, tile_valid, xs, w_gate_e.astype(BF16), w_up_e.astype(BF16),
                    w_down_e.astype(BF16), ts)
    return _combine(pos, ys, h1, g_final, _pick(t, 256))


def kernel(x, meta_tokens, g_mix, w_in, g_kv, w_kv_up, g_idx_k, b_f, w_branch_a, w_branch_b, w_out, g_ffn,
           w_group, b_group, w_expert, b_expert, w_gate_e, w_up_e, w_down_e, g_final):
    assert x.shape[0] == 1 and g_mix.shape[0] == 1, "single batch, single layer"
    out = _layer(x[0], meta_tokens.astype(x.dtype), g_mix[0], w_in[0], g_kv[0], w_kv_up[0], g_idx_k[0], b_f[0],
                 w_branch_a[0], w_branch_b[0], w_out[0], g_ffn[0], w_group[0], b_group[0], w_expert[0],
                 b_expert[0], w_gate_e[0], w_up_e[0], w_down_e[0], g_final)
    return out[None]
```

```python
import functools

import jax
import jax.numpy as jnp
from jax import lax
from jax.experimental import pallas as pl
from jax.experimental.pallas import tpu as pltpu

F32 = jnp.float32
BF16 = jnp.bfloat16

D_MODEL = 2048
N_META = 16
HEAD_DIM = 128
ROPE_THETA = 10000.0
EPS = 1e-6
NEG = -1e30
A_HEADS = 8
A_WIDTH = A_HEADS * HEAD_DIM
A_KV_RANK = 512
IDX_HEADS = 16
IDX_DIM = 64
TOPK_MAX = 256
B_HEADS = 8
B_WIDTH = B_HEADS * HEAD_DIM
N_GROUPS = 4
EXPERTS_PER_GROUP = 4
N_EXPERTS = N_GROUPS * EXPERTS_PER_GROUP
D_FF_EXPERT = 512

LANES = 128
META_PAD = LANES
VMEM_LIMIT = 56 * 1024 * 1024
LOG2E = 1.4426950408889634
NT_DIMS = (((1,), (1,)), ((), ()))

SM_KI = 0
SM_WI = IDX_DIM
SM_FB = IDX_DIM + IDX_HEADS
RT_E0 = N_GROUPS


def _cparams(sem):
    return pltpu.CompilerParams(dimension_semantics=sem, vmem_limit_bytes=VMEM_LIMIT)


def _rms_kernel(x_ref, g_ref, o_ref):
    x = x_ref[...]
    ms = jnp.mean(x * x, axis=-1, keepdims=True)
    o_ref[...] = (x * lax.rsqrt(ms + EPS) * g_ref[...]).astype(o_ref.dtype)


def _rmsnorm(x, g, tm):
    n, d = x.shape
    return pl.pallas_call(
        _rms_kernel,
        grid=(n // tm,),
        in_specs=[pl.BlockSpec((tm, d), lambda i: (i, 0)), pl.BlockSpec((1, d), lambda i: (0, 0))],
        out_specs=pl.BlockSpec((tm, d), lambda i: (i, 0)),
        out_shape=jax.ShapeDtypeStruct((n, d), BF16),
        compiler_params=_cparams(("parallel",)),
        name="rmsnorm",
    )(x, g.reshape(1, d))


def _rope_tile(y, cos, sin_signed, half):
    if 2 * half == LANES:
        partner = pltpu.roll(y, half, axis=1)
    else:
        lane = lax.broadcasted_iota(jnp.int32, y.shape, 1)
        first = (lane % (2 * half)) < half
        partner = jnp.where(first, pltpu.roll(y, LANES - half, axis=1), pltpu.roll(y, half, axis=1))
    return y * cos + partner * sin_signed


def _rope_tables(pos, half):
    inv_freq = ROPE_THETA ** (-jnp.arange(half, dtype=F32) / half)
    ang = pos.astype(F32)[:, None] * inv_freq[None, :]
    cos, sin = jnp.cos(ang), jnp.sin(ang)
    reps = LANES // (2 * half)
    cos_t = jnp.tile(jnp.concatenate([cos, cos], axis=1), (1, reps))
    sin_t = jnp.tile(jnp.concatenate([-sin, sin], axis=1), (1, reps))
    return cos_t, sin_t


PACK_ROWS = 256


def _pack_kernel(src_ref, x_ref, o_ref):
    del src_ref
    o_ref[...] = x_ref[...].T.astype(o_ref.dtype)


def _pack_w_in(wt, sections):
    _, d = wt.shape
    starts = []
    for lo, hi in sections:
        assert lo % 8 == 0 and (hi - lo) % PACK_ROWS == 0
        starts += list(range(lo, hi, PACK_ROWS))
    src = jnp.asarray([s // 8 for s in starts], jnp.int32)
    return pl.pallas_call(
        _pack_kernel,
        grid_spec=pltpu.PrefetchScalarGridSpec(
            num_scalar_prefetch=1,
            grid=(len(starts),),
            in_specs=[pl.BlockSpec((pl.Element(PACK_ROWS), pl.Element(d)), lambda j, src: (src[j] * 8, 0))],
            out_specs=pl.BlockSpec((d, PACK_ROWS), lambda j, src: (0, j))),
        out_shape=jax.ShapeDtypeStruct((d, len(starts) * PACK_ROWS), BF16),
        compiler_params=_cparams(("arbitrary",)),
        name="pack_w_in",
    )(src, wt)


def _proj_plain_kernel(h_ref, w_ref, o_ref, *, first_tile_scale):
    y = jnp.dot(h_ref[...], w_ref[...], preferred_element_type=F32)
    scale = jnp.where(pl.program_id(1) == 0, jnp.float32(first_tile_scale), jnp.float32(1.0))
    o_ref[...] = (y * scale).astype(o_ref.dtype)


def _proj_plain(h, w, col0, nc, tm, tn, first_tile_scale=1.0):
    n, d = h.shape
    c0 = col0 // tn
    return pl.pallas_call(
        functools.partial(_proj_plain_kernel, first_tile_scale=first_tile_scale),
        grid=(n // tm, nc // tn),
        in_specs=[pl.BlockSpec((tm, d), lambda i, j: (i, 0)), pl.BlockSpec((d, tn), lambda i, j: (0, c0 + j))],
        out_specs=pl.BlockSpec((tm, tn), lambda i, j: (i, j)),
        out_shape=jax.ShapeDtypeStruct((n, nc), BF16),
        compiler_params=_cparams(("parallel", "parallel")),
        name="proj_plain",
    )(h, w)


def _proj_rope_kernel(h_ref, w_ref, cos_ref, sin_ref, o_ref, *, half, head_major, out_scale):
    y = jnp.dot(h_ref[...], w_ref[...], preferred_element_type=F32)
    cos, sin = cos_ref[...], sin_ref[...]
    per_tile = LANES // (2 * half)
    for c in range(y.shape[1] // LANES):
        sl = slice(c * LANES, (c + 1) * LANES)
        roped = (_rope_tile(y[:, sl], cos, sin, half) * out_scale).astype(o_ref.dtype)
        if head_major:
            for k in range(per_tile):
                o_ref[c * per_tile + k] = roped[:, k * 2 * half:(k + 1) * 2 * half]
        else:
            o_ref[:, sl] = roped


def _proj_rope(h, w, col0, nc, cos, sin, half, tm, tn, head_major=False, out_scale=1.0):
    n, d = h.shape
    c0 = col0 // tn
    if head_major:
        assert tn == nc
        heads = nc // (2 * half)
        out_spec = pl.BlockSpec((heads, tm, 2 * half), lambda i, j: (0, i, 0))
        out_shape = jax.ShapeDtypeStruct((heads, n, 2 * half), BF16)
    else:
        out_spec = pl.BlockSpec((tm, tn), lambda i, j: (i, j))
        out_shape = jax.ShapeDtypeStruct((n, nc), BF16)
    return pl.pallas_call(
        functools.partial(_proj_rope_kernel, half=half, head_major=head_major, out_scale=out_scale),
        grid=(n // tm, nc // tn),
        in_specs=[pl.BlockSpec((tm, d), lambda i, j: (i, 0)), pl.BlockSpec((d, tn), lambda i, j: (0, c0 + j)),
                  pl.BlockSpec((tm, LANES), lambda i, j: (i, 0)), pl.BlockSpec((tm, LANES), lambda i, j: (i, 0))],
        out_specs=out_spec,
        out_shape=out_shape,
        compiler_params=_cparams(("parallel", "parallel")),
        name="proj_rope",
    )(h, w, cos, sin)


def _kv_kernel(h_ref, wc_ref, g_ref, wup_ref, cos_ref, sin_ref, ka_ref, va_ref):
    c = jnp.dot(h_ref[...], wc_ref[...], preferred_element_type=F32)
    ms = jnp.mean(c * c, axis=-1, keepdims=True)
    cn = (c * lax.rsqrt(ms + EPS) * g_ref[...]).astype(BF16)
    kv = jnp.dot(cn, wup_ref[...], preferred_element_type=F32)
    cos, sin = cos_ref[...], sin_ref[...]
    for hd in range(A_HEADS):
        sl = slice(hd * HEAD_DIM, (hd + 1) * HEAD_DIM)
        ka_ref[:, sl] = _rope_tile(kv[:, sl], cos, sin, HEAD_DIM // 2).astype(ka_ref.dtype)
    va_ref[...] = kv[:, A_WIDTH:].astype(va_ref.dtype)


def _proj_kv(h, wc, col0, g_kv, wup, cos, sin, tm):
    n, d = h.shape
    c0 = col0 // A_KV_RANK
    return pl.pallas_call(
        _kv_kernel,
        grid=(n // tm,),
        in_specs=[pl.BlockSpec((tm, d), lambda i: (i, 0)),
                  pl.BlockSpec((d, A_KV_RANK), lambda i: (0, c0)),
                  pl.BlockSpec((1, A_KV_RANK), lambda i: (0, 0)),
                  pl.BlockSpec((A_KV_RANK, 2 * A_WIDTH), lambda i: (0, 0)),
                  pl.BlockSpec((tm, LANES), lambda i: (i, 0)),
                  pl.BlockSpec((tm, LANES), lambda i: (i, 0))],
        out_specs=[pl.BlockSpec((tm, A_WIDTH), lambda i: (i, 0)), pl.BlockSpec((tm, A_WIDTH), lambda i: (i, 0))],
        out_shape=[jax.ShapeDtypeStruct((n, A_WIDTH), BF16), jax.ShapeDtypeStruct((n, A_WIDTH), BF16)],
        compiler_params=_cparams(("parallel",)),
        name="proj_kv",
    )(h, wc, g_kv.reshape(1, A_KV_RANK), wup, cos, sin)


def _small_kernel(h_ref, w_ref, g_ref, bf_ref, c0_ref, cos_ref, sin_ref, o_ref, carry_ref):
    tm = h_ref.shape[0]

    @pl.when(pl.program_id(0) == 0)
    def _():
        carry_ref[...] = c0_ref[...]

    y = jnp.dot(h_ref[...], w_ref[...], preferred_element_type=F32)
    lane = lax.broadcasted_iota(jnp.int32, y.shape, 1)
    is_ki = lane < SM_WI
    ms = jnp.sum(jnp.where(is_ki, y * y, 0.0), axis=-1, keepdims=True) * (1.0 / IDX_DIM)
    ki = _rope_tile(y * lax.rsqrt(ms + EPS) * g_ref[...], cos_ref[...], sin_ref[...], IDX_DIM // 2)
    wi = y * (IDX_HEADS ** -0.5 * IDX_DIM ** -0.5)
    z = y + bf_ref[...]
    logf = -(jnp.maximum(-z, 0.0) + jnp.log(1.0 + jnp.exp(-jnp.abs(z))))
    hi = logf.astype(BF16)
    r1 = logf - hi.astype(F32)
    mid = r1.astype(BF16)
    lo = (r1 - mid.astype(F32)).astype(BF16)
    row = lax.broadcasted_iota(jnp.int32, (tm, tm), 0)
    col = lax.broadcasted_iota(jnp.int32, (tm, tm), 1)
    tri = (col <= row).astype(BF16)
    csum = (jnp.dot(tri, hi, preferred_element_type=F32) + jnp.dot(tri, mid, preferred_element_type=F32)
            + jnp.dot(tri, lo, preferred_element_type=F32)) + carry_ref[0:1, :]
    carry_ref[...] = jnp.broadcast_to(csum[tm - 1:tm, :], carry_ref.shape)
    o_ref[...] = jnp.where(is_ki, ki, jnp.where(lane < SM_FB, wi, csum))


def _proj_small(h, w, col0, g_idx_pad, bf_pad, c0, cos, sin, tm):
    n, d = h.shape
    cb = col0 // LANES
    return pl.pallas_call(
        _small_kernel,
        grid=(n // tm,),
        in_specs=[pl.BlockSpec((tm, d), lambda i: (i, 0)),
                  pl.BlockSpec((d, LANES), lambda i: (0, cb)),
                  pl.BlockSpec((1, LANES), lambda i: (0, 0)),
                  pl.BlockSpec((1, LANES), lambda i: (0, 0)),
                  pl.BlockSpec((8, LANES), lambda i: (0, 0)),
                  pl.BlockSpec((tm, LANES), lambda i: (i, 0)),
                  pl.BlockSpec((tm, LANES), lambda i: (i, 0))],
        out_specs=pl.BlockSpec((tm, LANES), lambda i: (i, 0)),
        out_shape=jax.ShapeDtypeStruct((n, LANES), F32),
        scratch_shapes=[pltpu.VMEM((8, LANES), F32)],
        compiler_params=_cparams(("arbitrary",)),
        name="proj_small",
    )(h, w, g_idx_pad, bf_pad, c0, cos, sin)


COUNT_ROWS = 64


def _key_to_float(key):
    bits = key ^ ((key >> 31) & jnp.int32(0x7FFFFFFF))
    return lax.bitcast_convert_type(bits, F32)


def _select_kernel(qi_ref, kir_ref, kim_ref, wt_ref, or_ref, om_ref, s_ref, *, tq, topk):
    i = pl.program_id(0)
    n_real = or_ref.shape[1]
    n_chunks = n_real // tq
    krow = lax.broadcasted_iota(jnp.int32, (tq, tq), 0)
    qcol = lax.broadcasted_iota(jnp.int32, (tq, tq), 1)
    mrow = lax.broadcasted_iota(jnp.int32, (META_PAD, tq), 0)
    wt = wt_ref[...]

    def scores(kc):
        s = jnp.zeros((kc.shape[0], tq), F32)
        for h in range(IDX_HEADS):
            d = lax.dot_general(kc, qi_ref[h], NT_DIMS, preferred_element_type=F32)
            s = s + wt[h:h + 1, :] * jnp.maximum(d, 0.0)
        return s

    def causal(c):
        return krow + (c - i) * tq <= qcol

    def score_chunk(c, _):
        off = pl.multiple_of(c * tq, tq)
        s_ref[pl.ds(off, tq), :] = jnp.where(causal(c), scores(kir_ref[pl.ds(off, tq), :]), NEG)
        return 0

    lax.fori_loop(0, i + 1, score_chunk, 0)
    s_ref[n_real:, :] = jnp.where(mrow < N_META, scores(kim_ref[...]), NEG)

    def count(cmp, thr):
        def body(c, acc):
            off = c * tq
            for r in range(tq // COUNT_ROWS):
                blk = s_ref[pl.ds(pl.multiple_of(off + r * COUNT_ROWS, COUNT_ROWS), COUNT_ROWS), :]
                acc = acc + jnp.where(cmp(blk, thr), 1.0, 0.0)
            return acc

        acc = lax.fori_loop(0, i + 1, body, jnp.zeros((COUNT_ROWS, tq), F32))
        for r in range(META_PAD // COUNT_ROWS):
            blk = s_ref[n_real + r * COUNT_ROWS:n_real + (r + 1) * COUNT_ROWS, :]
            acc = acc + jnp.where(cmp(blk, thr), 1.0, 0.0)
        return jnp.sum(acc, axis=0, keepdims=True)

    kf = jnp.float32(topk)

    def bit_step(b, carry):
        t_key, cnt_t = carry
        cand = t_key + lax.shift_left(jnp.int32(1), 31 - b)
        cnt = count(lambda s, t: s >= t, _key_to_float(cand))
        take = cnt >= kf
        return jnp.where(take, cand, t_key), jnp.where(take, cnt, cnt_t)

    t0 = jnp.full((1, tq), jnp.iinfo(jnp.int32).min, jnp.int32)
    c0 = jnp.broadcast_to(((i + 1) * tq + META_PAD).astype(F32), (1, tq))
    t_key, cnt_t = lax.fori_loop(0, 32, bit_step, (t0, c0))
    thr = _key_to_float(t_key)

    def fill_chunk(c, _):
        or_ref[:, pl.ds(pl.multiple_of(c * tq, tq), tq)] = jnp.full((tq, tq), NEG, or_ref.dtype)
        return 0

    lax.fori_loop(i + 1, n_chunks, fill_chunk, 0)

    ambiguous = jnp.logical_and(cnt_t > kf, thr > NEG)
    any_amb = jnp.max(jnp.where(ambiguous, 1.0, 0.0)) > 0.0

    @pl.when(jnp.logical_not(any_amb))
    def _():
        def out_chunk(c, _):
            off = pl.multiple_of(c * tq, tq)
            sel = jnp.where(s_ref[pl.ds(off, tq), :] >= thr, 0.0, NEG)
            or_ref[:, pl.ds(off, tq)] = jnp.where(causal(c), sel, NEG).T.astype(or_ref.dtype)
            return 0

        lax.fori_loop(0, i + 1, out_chunk, 0)
        sel = jnp.where(s_ref[n_real:, :] >= thr, 0.0, NEG)
        om_ref[...] = jnp.where(mrow < N_META, sel, NEG).T.astype(om_ref.dtype)

    @pl.when(any_amb)
    def _():
        cnt_gt = count(lambda s, t: s > t, thr)
        quota = jnp.where(ambiguous, kf - cnt_gt, jnp.float32(3e38))

        def tie_select(s, valid, running):
            rows = s.shape[0]
            r = lax.broadcasted_iota(jnp.int32, (rows, rows), 0)
            c_ = lax.broadcasted_iota(jnp.int32, (rows, rows), 1)
            lower = (c_ <= r).astype(BF16)
            eq = jnp.where(s == thr, 1.0, 0.0)
            rank = jnp.dot(lower, eq.astype(BF16), preferred_element_type=F32) + running
            keep = jnp.where(s > thr, 0.0, jnp.where(jnp.logical_and(eq > 0.0, rank <= quota), 0.0, NEG))
            return jnp.where(valid, keep, NEG), running + jnp.sum(eq, axis=0, keepdims=True)

        bias_m, running = tie_select(s_ref[n_real:, :], mrow < N_META, jnp.zeros((1, tq), F32))
        om_ref[...] = bias_m.T.astype(om_ref.dtype)

        def out_chunk(c, running):
            off = pl.multiple_of(c * tq, tq)
            bias, running = tie_select(s_ref[pl.ds(off, tq), :], causal(c), running)
            or_ref[:, pl.ds(off, tq)] = bias.T.astype(or_ref.dtype)
            return running

        lax.fori_loop(0, i + 1, out_chunk, running)


def _select_bias(qi_hm, ki_real, ki_meta, wt, topk, tq):
    _, t, _ = qi_hm.shape
    return pl.pallas_call(
        functools.partial(_select_kernel, tq=tq, topk=topk),
        grid=(t // tq,),
        in_specs=[pl.BlockSpec((IDX_HEADS, tq, IDX_DIM), lambda i: (0, i, 0)),
                  pl.BlockSpec((t, IDX_DIM), lambda i: (0, 0)),
                  pl.BlockSpec((META_PAD, IDX_DIM), lambda i: (0, 0)),
                  pl.BlockSpec((IDX_HEADS, tq), lambda i: (0, i))],
        out_specs=[pl.BlockSpec((tq, t), lambda i: (i, 0)), pl.BlockSpec((tq, META_PAD), lambda i: (i, 0))],
        out_shape=[jax.ShapeDtypeStruct((t, t), BF16), jax.ShapeDtypeStruct((t, META_PAD), BF16)],
        scratch_shapes=[pltpu.VMEM((t + META_PAD, tq), F32)],
        compiler_params=_cparams(("parallel",)),
        name="select_bias",
    )(qi_hm, ki_real, ki_meta, wt)


def _attn_kernel(*refs, t, fox):
    if fox:
        (qi_ref, kj_ref, q_ref, k_ref, v_ref, km_ref, vm_ref, cq_ref, ck_ref, ckm_ref,
         o_ref, m_ref, l_ref, acc_ref, r_ref) = refs
        s_ref = None
    else:
        (qi_ref, kj_ref, q_ref, k_ref, v_ref, km_ref, vm_ref, b_ref, bm_ref,
         o_ref, m_ref, l_ref, acc_ref, s_ref) = refs
    stage_logits = s_ref is not None
    step = pl.program_id(0)
    i = qi_ref[step]
    j = kj_ref[step]
    nheads = q_ref.shape[1] // HEAD_DIM

    def block(k_blk_ref, v_blk_ref, bias_fn):
        width = k_blk_ref.shape[0]

        def logits(h):
            sl = slice(h * HEAD_DIM, (h + 1) * HEAD_DIM)
            return lax.dot_general(q_ref[:, sl], k_blk_ref[:, sl], NT_DIMS, preferred_element_type=F32)

        if stage_logits:
            for h in range(nheads):
                s_ref[h, :, :width] = logits(h)
        s_next = None if stage_logits else logits(0)
        for h in range(nheads):
            sl = slice(h * HEAD_DIM, (h + 1) * HEAD_DIM)
            if stage_logits:
                s = s_ref[h, :, :width]
            else:
                s, s_next = s_next, (logits(h + 1) if h + 1 < nheads else None)
            u = bias_fn(s, h)
            m_prev = m_ref[h]
            mx = jnp.max(u, axis=1, keepdims=True)
            if fox:
                r = r_ref[h]
                m_next = jnp.maximum(m_prev, mx + r)
                sub = m_next - r
            else:
                m_next = jnp.maximum(m_prev, mx)
                sub = m_next
            alpha = jnp.exp2(m_prev - m_next)
            ps = [jnp.exp2(u[:, c * LANES:(c + 1) * LANES] - sub) for c in range(width // LANES)]
            psum = ps[0]
            for pc in ps[1:]:
                psum = psum + pc
            p = ps[0] if len(ps) == 1 else jnp.concatenate(ps, axis=1)
            l_ref[h] = alpha * l_ref[h] + psum
            m_ref[h] = m_next
            acc_ref[h] = alpha * acc_ref[h] + jnp.dot(p.astype(BF16), v_blk_ref[:, sl],
                                                      preferred_element_type=F32)

    @pl.when(j == 0)
    def _():
        m_ref[...] = jnp.full(m_ref.shape, NEG, F32)
        l_ref[...] = jnp.zeros(l_ref.shape, F32)
        acc_ref[...] = jnp.zeros(acc_ref.shape, F32)
        mcol = lax.broadcasted_iota(jnp.int32, (t, META_PAD), 1)
        if fox:
            for h in range(nheads):
                r_ref[h] = jnp.broadcast_to(cq_ref[:, h:h + 1] * LOG2E, (t, LANES))

            def bias_fn(u, h):
                return jnp.where(mcol < N_META, u - ckm_ref[h:h + 1, :] * LOG2E, NEG)
        else:
            bias_m = bm_ref[...].astype(F32)

            def bias_fn(u, h):
                return u + bias_m
        block(km_ref, vm_ref, bias_fn)

    if fox:
        @pl.when(j < i)
        def _():
            block(k_ref, v_ref, lambda u, h: u - ck_ref[h:h + 1, :] * LOG2E)

        @pl.when(j == i)
        def _():
            row = lax.broadcasted_iota(jnp.int32, (t, t), 0)
            col = lax.broadcasted_iota(jnp.int32, (t, t), 1)
            block(k_ref, v_ref, lambda u, h: jnp.where(col <= row, u - ck_ref[h:h + 1, :] * LOG2E, NEG))
    else:
        bias = b_ref[...].astype(F32)
        block(k_ref, v_ref, lambda u, h: u + bias)

    @pl.when(j == i)
    def _():
        for h in range(nheads):
            sl = slice(h * HEAD_DIM, (h + 1) * HEAD_DIM)
            o_ref[:, sl] = (acc_ref[h] / jnp.sum(l_ref[h], axis=1, keepdims=True)).astype(o_ref.dtype)


def _attention(qkv, meta_kv, extra, *, fox, t_blk):
    q_arr, q_col, k_arr, k_col, v_arr, v_col = qkv
    km_arr, km_col, vm_arr, vm_col = meta_kv
    t = q_arr.shape[0]
    w = A_WIDTH
    nheads = w // HEAD_DIM
    nq = t // t_blk
    qi = jnp.asarray([i for i in range(nq) for _ in range(i + 1)], jnp.int32)
    kj = jnp.asarray([j for i in range(nq) for j in range(i + 1)], jnp.int32)

    in_specs = [pl.BlockSpec((t_blk, w), lambda s, qi, kj: (qi[s], q_col)),
                pl.BlockSpec((t_blk, w), lambda s, qi, kj: (kj[s], k_col)),
                pl.BlockSpec((t_blk, w), lambda s, qi, kj: (kj[s], v_col)),
                pl.BlockSpec((META_PAD, w), lambda s, qi, kj: (0, km_col)),
                pl.BlockSpec((META_PAD, w), lambda s, qi, kj: (0, vm_col))]
    scratch = [pltpu.VMEM((nheads, t_blk, LANES), F32), pltpu.VMEM((nheads, t_blk, LANES), F32),
               pltpu.VMEM((nheads, t_blk, HEAD_DIM), F32)]
    args = (q_arr, k_arr, v_arr, km_arr, vm_arr)
    if fox:
        cq, ct_real, ct_meta = extra
        in_specs += [pl.BlockSpec((t_blk, nheads), lambda s, qi, kj: (qi[s], 0)),
                     pl.BlockSpec((nheads, t_blk), lambda s, qi, kj: (0, kj[s])),
                     pl.BlockSpec((nheads, META_PAD), lambda s, qi, kj: (0, 0))]
        args += (cq, ct_real, ct_meta)
        scratch.append(pltpu.VMEM((nheads, t_blk, LANES), F32))
    else:
        bias_real, bias_meta = extra
        in_specs += [pl.BlockSpec((t_blk, t_blk), lambda s, qi, kj: (qi[s], kj[s])),
                     pl.BlockSpec((t_blk, META_PAD), lambda s, qi, kj: (qi[s], 0))]
        args += (bias_real, bias_meta)
        scratch.append(pltpu.VMEM((nheads, t_blk, t_blk), F32))
    return pl.pallas_call(
        functools.partial(_attn_kernel, t=t_blk, fox=fox),
        grid_spec=pltpu.PrefetchScalarGridSpec(
            num_scalar_prefetch=2,
            grid=(int(qi.shape[0]),),
            in_specs=in_specs,
            out_specs=pl.BlockSpec((t_blk, w), lambda s, qi, kj: (qi[s], 0)),
            scratch_shapes=scratch),
        out_shape=jax.ShapeDtypeStruct((t, w), BF16),
        compiler_params=_cparams(("arbitrary",)),
        name="attn_fox" if fox else "attn_dsa",
    )(qi, kj, *args)


def _route(logits):
    lane = lax.broadcasted_iota(jnp.int32, logits.shape, 1).astype(F32)
    big = jnp.float32(LANES)
    ninf = -jnp.inf
    gl = jnp.where(lane < N_GROUPS, logits, ninf)
    gmax = jnp.max(gl, axis=1, keepdims=True)
    g_sel = jnp.min(jnp.where(gl == gmax, lane, big), axis=1, keepdims=True)
    p_group = 1.0 / jnp.sum(jnp.exp(gl - gmax), axis=1, keepdims=True)
    e_lo = RT_E0 + g_sel * EXPERTS_PER_GROUP
    in_grp = jnp.logical_and(lane >= e_lo, lane < e_lo + EXPERTS_PER_GROUP)
    el = jnp.where(in_grp, logits, ninf)
    emax = jnp.max(el, axis=1, keepdims=True)
    ex = jnp.exp(el - emax)
    prob = ex / jnp.sum(ex, axis=1, keepdims=True)
    prob = jnp.where(in_grp, prob, -1.0)
    p1 = jnp.max(prob, axis=1, keepdims=True)
    i1 = jnp.min(jnp.where(prob == p1, lane, big), axis=1, keepdims=True)
    rest = jnp.where(lane == i1, -1.0, prob)
    p2 = jnp.max(rest, axis=1, keepdims=True)
    i2 = jnp.min(jnp.where(rest == p2, lane, big), axis=1, keepdims=True)
    tot = p1 + p2
    w1 = p_group * (p1 / tot)
    w2 = p_group * (p2 / tot)
    return jnp.where(lane == i1, w1, jnp.where(lane == i2, w2, 0.0)), g_sel


def _merge_kernel(h_ref, ya_ref, yb_ref, wga_ref, wgb_ref, wa_ref, wb_ref, wo_ref, x_ref, g_ref,
                  wr_ref, br_ref, h1_ref, xr_ref, plan_ref, acc_ref, cnt_ref):
    n = pl.program_id(1)

    @pl.when(jnp.logical_and(pl.program_id(0) == 0, n == 0))
    def _():
        cnt_ref[...] = jnp.zeros(cnt_ref.shape, F32)

    @pl.when(n == 0)
    def _():
        acc_ref[...] = jnp.zeros(acc_ref.shape, F32)

    h = h_ref[...]
    ga = jnp.dot(h, wga_ref[...], preferred_element_type=F32)
    gb = jnp.dot(h, wgb_ref[...], preferred_element_type=F32)
    pa = jnp.dot(ya_ref[...], wa_ref[...], preferred_element_type=F32)
    pb = jnp.dot(yb_ref[...], wb_ref[...], preferred_element_type=F32)
    merged = jax.nn.sigmoid(ga) * pa + jax.nn.sigmoid(gb) * pb
    acc_ref[...] += jnp.dot(merged.astype(BF16), wo_ref[...], preferred_element_type=F32)

    @pl.when(n == pl.num_programs(1) - 1)
    def _():
        h1 = x_ref[...] + acc_ref[...]
        h1_ref[...] = h1
        ms = jnp.mean(h1 * h1, axis=-1, keepdims=True)
        t = h1 * lax.rsqrt(ms + EPS) * g_ref[...]
        t_hi = t.astype(BF16)
        t_lo = (t - t_hi.astype(F32)).astype(BF16)
        wr = wr_ref[...]
        y = jnp.dot(t_hi, wr, preferred_element_type=F32)
        logits = (y[:, :LANES] + y[:, LANES:]
                  + jnp.dot(t_lo, wr[:, :LANES], preferred_element_type=F32)) + br_ref[...]
        gates, g_sel = _route(logits)
        d = t.shape[1]
        xr_ref[:, :d] = t
        xr_ref[:, d:] = gates
        tm = t.shape[0]
        lane = lax.broadcasted_iota(jnp.int32, gates.shape, 1).astype(F32)
        onehot = jnp.where(lane == g_sel, 1.0, 0.0).astype(BF16)
        rr = lax.broadcasted_iota(jnp.int32, (tm, tm), 0)
        cc = lax.broadcasted_iota(jnp.int32, (tm, tm), 1)
        cum = jnp.dot((cc <= rr).astype(BF16), onehot, preferred_element_type=F32) + cnt_ref[0:1, :]
        cnt_ref[...] = jnp.broadcast_to(cum[tm - 1:tm, :], cnt_ref.shape)
        plan_ref[...] = jnp.where(lane < N_GROUPS, cum, jnp.where(lane == N_GROUPS, g_sel, 0.0))


def _merge(h, ya, yb, w_all, ga_col0, gb_col0, wa, wb, wo, x, g_ffn, wr_split, br, tm, tn):
    t, d = h.shape
    row = lambda i, n: (i, 0)
    ga0, gb0 = ga_col0 // tn, gb_col0 // tn
    wga = wgb = w_all
    return pl.pallas_call(
        _merge_kernel,
        grid=(t // tm, d // tn),
        in_specs=[pl.BlockSpec((tm, d), row),
                  pl.BlockSpec((tm, A_WIDTH), row),
                  pl.BlockSpec((tm, B_WIDTH), row),
                  pl.BlockSpec((d, tn), lambda i, n: (0, ga0 + n)),
                  pl.BlockSpec((d, tn), lambda i, n: (0, gb0 + n)),
                  pl.BlockSpec((A_WIDTH, tn), lambda i, n: (0, n)),
                  pl.BlockSpec((B_WIDTH, tn), lambda i, n: (0, n)),
                  pl.BlockSpec((tn, d), lambda i, n: (n, 0)),
                  pl.BlockSpec((tm, d), row),
                  pl.BlockSpec((1, d), lambda i, n: (0, 0)),
                  pl.BlockSpec((d, 2 * LANES), lambda i, n: (0, 0)),
                  pl.BlockSpec((1, LANES), lambda i, n: (0, 0))],
        out_specs=[pl.BlockSpec((tm, d), row), pl.BlockSpec((tm, d + LANES), row),
                   pl.BlockSpec((tm, LANES), row)],
        out_shape=[jax.ShapeDtypeStruct((t, d), F32), jax.ShapeDtypeStruct((t, d + LANES), F32),
                   jax.ShapeDtypeStruct((t, LANES), F32)],
        scratch_shapes=[pltpu.VMEM((tm, d), F32), pltpu.VMEM((8, LANES), F32)],
        compiler_params=_cparams(("arbitrary", "arbitrary")),
        name="merge_out_router",
    )(h, ya, yb, wga, wgb, wa, wb, wo, x, g_ffn.reshape(1, d), wr_split, br)


def _row_copy(src_ref, src_row, dst_ref, dst_row, sem):
    return pltpu.make_async_copy(src_ref.at[pl.ds(src_row, 1), :], dst_ref.at[pl.ds(dst_row, 1), :], sem)


def _dispatch_kernel(pos_ref, x_ref, init_ref, xs_ref, sem):
    del init_ref
    i = pl.program_id(0)
    tm = x_ref.shape[0]

    def issue(r, _):
        _row_copy(x_ref, r, xs_ref, pos_ref[i * tm + r], sem).start()
        return 0

    lax.fori_loop(0, tm, issue, 0)

    def drain(r, _):
        _row_copy(x_ref, r, xs_ref, pos_ref[i * tm + r], sem).wait()
        return 0

    lax.fori_loop(0, tm, drain, 0)


def _dispatch(pos, xr, n_slots, tm):
    t, w = xr.shape
    return pl.pallas_call(
        _dispatch_kernel,
        grid_spec=pltpu.PrefetchScalarGridSpec(
            num_scalar_prefetch=1,
            grid=(t // tm,),
            in_specs=[pl.BlockSpec((tm, w), lambda i, pos: (i, 0)),
                      pl.BlockSpec(memory_space=pl.ANY)],
            out_specs=pl.BlockSpec(memory_space=pl.ANY),
            scratch_shapes=[pltpu.SemaphoreType.DMA]),
        out_shape=jax.ShapeDtypeStruct((n_slots, w), xr.dtype),
        input_output_aliases={2: 0},
        compiler_params=_cparams(("arbitrary",)),
        name="moe_dispatch",
    )(pos, xr, jnp.zeros((n_slots, w), xr.dtype))


def _group_ffn_kernel(tg_ref, tv_ref, x_ref, wg_ref, wu_ref, wd_ref, o_ref, acc_ref, xb_ref):
    j = pl.program_id(0)
    k = pl.program_id(1)
    d = o_ref.shape[1]

    @pl.when(k == 0)
    def _():
        acc_ref[...] = jnp.zeros(acc_ref.shape, F32)
        xb_ref[...] = x_ref[:, :d].astype(BF16)

    @pl.when(tv_ref[j] > 0)
    def _():
        gates = x_ref[:, d:]
        lane = lax.broadcasted_iota(jnp.int32, gates.shape, 1)
        e_lane = RT_E0 + tg_ref[j] * EXPERTS_PER_GROUP + k
        g_e = jnp.sum(jnp.where(lane == e_lane, gates, 0.0), axis=1, keepdims=True)
        xb = xb_ref[...]
        hg = jnp.dot(xb, wg_ref[0].astype(BF16), preferred_element_type=F32)
        hu = jnp.dot(xb, wu_ref[0].astype(BF16), preferred_element_type=F32)
        act = jax.nn.silu(hg) * hu * g_e
        acc_ref[...] += jnp.dot(act.astype(BF16), wd_ref[0].astype(BF16), preferred_element_type=F32)

    @pl.when(k == pl.num_programs(1) - 1)
    def _():
        o_ref[...] = acc_ref[...]


def _group_ffn(tile_group, tile_valid, xs, wg, wu, wd, ts):
    n_slots, w = xs.shape
    d = w - LANES
    f = wg.shape[2]
    expert = lambda j, k, tg, tv: (tg[j] * EXPERTS_PER_GROUP + k, 0, 0)
    return pl.pallas_call(
        _group_ffn_kernel,
        grid_spec=pltpu.PrefetchScalarGridSpec(
            num_scalar_prefetch=2,
            grid=(n_slots // ts, EXPERTS_PER_GROUP),
            in_specs=[pl.BlockSpec((ts, w), lambda j, k, tg, tv: (j, 0)),
                      pl.BlockSpec((1, d, f), expert),
                      pl.BlockSpec((1, d, f), expert),
                      pl.BlockSpec((1, f, d), expert)],
            out_specs=pl.BlockSpec((ts, d), lambda j, k, tg, tv: (j, 0)),
            scratch_shapes=[pltpu.VMEM((ts, d), F32), pltpu.VMEM((ts, d), BF16)]),
        out_shape=jax.ShapeDtypeStruct((n_slots, d), F32),
        compiler_params=_cparams(("arbitrary", "arbitrary")),
        name="moe_group_ffn",
    )(tile_group, tile_valid, xs, wg, wu, wd)


def _combine_kernel(pos_ref, ys_ref, h1_ref, gf_ref, o_ref, ybuf_ref, sem):
    i = pl.program_id(0)
    tm = h1_ref.shape[0]

    def issue(r, _):
        _row_copy(ys_ref, pos_ref[i * tm + r], ybuf_ref, r, sem).start()
        return 0

    lax.fori_loop(0, tm, issue, 0)

    def drain(r, _):
        _row_copy(ys_ref, pos_ref[i * tm + r], ybuf_ref, r, sem).wait()
        return 0

    lax.fori_loop(0, tm, drain, 0)
    h2 = h1_ref[...] + ybuf_ref[...]
    ms = jnp.mean(h2 * h2, axis=-1, keepdims=True)
    o_ref[...] = h2 * lax.rsqrt(ms + EPS) * gf_ref[...]


def _combine(pos, ys, h1, g_final, tm):
    t, d = h1.shape
    return pl.pallas_call(
        _combine_kernel,
        grid_spec=pltpu.PrefetchScalarGridSpec(
            num_scalar_prefetch=1,
            grid=(t // tm,),
            in_specs=[pl.BlockSpec(memory_space=pl.ANY),
                      pl.BlockSpec((tm, d), lambda i, pos: (i, 0)),
                      pl.BlockSpec((1, d), lambda i, pos: (0, 0))],
            out_specs=pl.BlockSpec((tm, d), lambda i, pos: (i, 0)),
            scratch_shapes=[pltpu.VMEM((tm, d), F32), pltpu.SemaphoreType.DMA]),
        out_shape=jax.ShapeDtypeStruct((t, d), F32),
        compiler_params=_cparams(("arbitrary",)),
        name="moe_combine_final",
    )(pos, ys, h1, g_final.reshape(1, d))


def _group_plan(plan, ts):
    t = plan.shape[0]
    g_sel = plan[:, N_GROUPS].astype(jnp.int32)
    cum_sel = jnp.take_along_axis(plan[:, :N_GROUPS], g_sel[:, None], axis=1)[:, 0]
    counts = plan[t - 1, :N_GROUPS].astype(jnp.int32)
    padded = (counts + ts - 1) // ts * ts
    ends = jnp.cumsum(padded)
    starts = ends - padded
    pos = starts[g_sel] + cum_sel.astype(jnp.int32) - 1
    n_tiles = (t + N_GROUPS * (ts - 1)) // ts
    tile_start = jnp.arange(n_tiles, dtype=jnp.int32) * ts
    tile_group = jnp.minimum(jnp.sum(tile_start[:, None] >= ends[None, :], axis=1), N_GROUPS - 1)
    tile_valid = (tile_start < ends[N_GROUPS - 1]).astype(jnp.int32)
    return pos, tile_group.astype(jnp.int32), tile_valid, n_tiles * ts


def _pick(n, pref):
    return pref if n % pref == 0 else n


def _pad_rows(a, rows):
    return jnp.pad(a, ((0, rows - a.shape[0]), (0, 0)))


def _layer(x, meta, g_mix, w_in, g_kv, w_kv_up, g_idx_k, b_f, w_branch_a, w_branch_b, w_out,
           g_ffn, w_group, b_group, w_expert, b_expert, w_gate_e, w_up_e, w_down_e, g_final):
    t, d = x.shape
    n_all = t + N_META
    topk = min(TOPK_MAX, n_all // 4)

    o = [0]
    for wdt in (A_WIDTH, A_KV_RANK, IDX_HEADS * IDX_DIM, IDX_DIM, IDX_HEADS, B_WIDTH, B_WIDTH, B_WIDTH,
                B_HEADS, D_MODEL, D_MODEL):
        o.append(o[-1] + wdt)
    w_in_t = w_in.T
    w_all = _pack_w_in(w_in_t, [(o[0], o[1]), (o[2], o[3]), (o[5], o[8]), (o[9], o[10]), (o[10], o[11]),
                                (o[1], o[2])])
    w_small = _pad_rows(jnp.concatenate([w_in_t[o[3]:o[5]], w_in_t[o[8]:o[9]]], axis=0), LANES).T.astype(BF16)
    col_qa = 0
    col_qi = col_qa + A_WIDTH
    col_b = col_qi + IDX_HEADS * IDX_DIM
    col_ga = col_b + 3 * B_WIDTH
    col_gb = col_ga + D_MODEL
    col_ckv = col_gb + D_MODEL
    w_up_bf = w_kv_up.astype(BF16)
    g_idx_pad = jnp.zeros((1, LANES), F32).at[0, SM_KI:SM_KI + IDX_DIM].set(g_idx_k)
    bf_pad = jnp.zeros((1, LANES), F32).at[0, SM_FB:SM_FB + B_HEADS].set(b_f)

    pos_meta = jnp.arange(N_META)
    pos_real = jnp.arange(N_META, n_all)
    cos64_m, sin64_m = _rope_tables(pos_meta, HEAD_DIM // 2)
    cos64_r, sin64_r = _rope_tables(pos_real, HEAD_DIM // 2)
    cos32_m, sin32_m = _rope_tables(pos_meta, IDX_DIM // 2)
    cos32_r, sin32_r = _rope_tables(pos_real, IDX_DIM // 2)

    tm = _pick(t, 512)

    hm = _rmsnorm(meta, g_mix, N_META)
    ka_m, va_m = _proj_kv(hm, w_all, col_ckv, g_kv, w_up_bf, cos64_m, sin64_m, N_META)
    small_m = _proj_small(hm, w_small, 0, g_idx_pad, bf_pad, jnp.zeros((8, LANES), F32), cos32_m, sin32_m,
                          N_META)
    b_m = _proj_plain(hm, w_all, col_b, 3 * B_WIDTH, N_META, B_WIDTH)
    c_carry = jnp.broadcast_to(small_m[N_META - 1:N_META, :], (8, LANES))

    h = _rmsnorm(x, g_mix, tm)
    qk_scale = HEAD_DIM ** -0.5 * LOG2E
    tmw = _pick(t, 1024)
    qa = _proj_rope(h, w_all, col_qa, A_WIDTH, cos64_r, sin64_r, HEAD_DIM // 2, tmw, A_WIDTH, out_scale=qk_scale)
    ka_r, va_r = _proj_kv(h, w_all, col_ckv, g_kv, w_up_bf, cos64_r, sin64_r, tm)
    qi_hm = _proj_rope(h, w_all, col_qi, IDX_HEADS * IDX_DIM, cos32_r, sin32_r, IDX_DIM // 2, tmw,
                       IDX_HEADS * IDX_DIM, head_major=True)
    small_r = _proj_small(h, w_small, 0, g_idx_pad, bf_pad, c_carry, cos32_r, sin32_r, tm)
    b_r = _proj_plain(h, w_all, col_b, 3 * B_WIDTH, tmw, B_WIDTH, first_tile_scale=qk_scale)

    ki_real = small_r[:, SM_KI:SM_KI + IDX_DIM].astype(BF16)
    ki_meta = _pad_rows(small_m[:, SM_KI:SM_KI + IDX_DIM], META_PAD).astype(BF16)
    wt = small_r[:, SM_WI:SM_WI + IDX_HEADS].T
    cq = small_r[:, SM_FB:SM_FB + B_HEADS]
    ct_meta = _pad_rows(small_m[:, SM_FB:SM_FB + B_HEADS], META_PAD).T

    bias_real, bias_meta = _select_bias(qi_hm, ki_real, ki_meta, wt, topk, _pick(t, 256))
    ya = _attention((qa, 0, ka_r, 0, va_r, 0),
                    (_pad_rows(ka_m, META_PAD), 0, _pad_rows(va_m, META_PAD), 0),
                    (bias_real, bias_meta), fox=False, t_blk=_pick(t, 512))
    b_m_pad = _pad_rows(b_m, META_PAD)
    yb = _attention((b_r, 0, b_r, 1, b_r, 2), (b_m_pad, 1, b_m_pad, 2),
                    (cq, cq.T, ct_meta), fox=True, t_blk=_pick(t, 512))

    w_r = jnp.concatenate([w_group, w_expert, jnp.zeros((d, LANES - N_GROUPS - N_EXPERTS), F32)], axis=1)
    wr_hi = w_r.astype(BF16)
    wr_split = jnp.concatenate([wr_hi, (w_r - wr_hi.astype(F32)).astype(BF16)], axis=1)
    b_r_pad = jnp.concatenate([b_group, b_expert, jnp.zeros((LANES - N_GROUPS - N_EXPERTS,), F32)]).reshape(1, LANES)
    h1, xr, plan = _merge(h, ya, yb, w_all, col_ga, col_gb, w_branch_a.astype(BF16), w_branch_b.astype(BF16),
                          w_out.astype(BF16), x, g_ffn, wr_split, b_r_pad, tm, 256)

    ts = _pick(t, 512)
    pos, tile_group, tile_valid, n_slots = _group_plan(plan, ts)
    xs = _dispatch(pos, xr, n_slots, _pick(t, 256))
    ys = _group_ffn(tile_group, tile_valid, xs, w_gate_e, w_up_e, w_down_e, ts)
    return _combine(pos, ys, h1, g_final, _pick(t, 256))


def kernel(x, meta_tokens, g_mix, w_in, g_kv, w_kv_up, g_idx_k, b_f, w_branch_a, w_branch_b, w_out, g_ffn,
           w_group, b_group, w_expert, b_expert, w_gate_e, w_up_e, w_down_e, g_final):
    assert x.shape[0] == 1 and g_mix.shape[0] == 1, "single batch, single layer"
    out = _layer(x[0], meta_tokens.astype(x.dtype), g_mix[0], w_in[0], g_kv[0], w_kv_up[0], g_idx_k[0], b_f[0],
                 w_branch_a[0], w_branch_b[0], w_out[0], g_ffn[0], w_group[0], b_group[0], w_expert[0],
                 b_expert[0], w_gate_e[0], w_up_e[0], w_down_e[0], g_final)
    return out[None]
```

```python
import functools

import jax
import jax.numpy as jnp
from jax import lax
from jax.experimental import pallas as pl
from jax.experimental.pallas import tpu as pltpu

F32 = jnp.float32
BF16 = jnp.bfloat16

D_MODEL = 2048
N_META = 16
HEAD_DIM = 128
ROPE_THETA = 10000.0
EPS = 1e-6
NEG = -1e30
A_HEADS = 8
A_WIDTH = A_HEADS * HEAD_DIM
A_KV_RANK = 512
IDX_HEADS = 16
IDX_DIM = 64
TOPK_MAX = 256
B_HEADS = 8
B_WIDTH = B_HEADS * HEAD_DIM
N_GROUPS = 4
EXPERTS_PER_GROUP = 4
N_EXPERTS = N_GROUPS * EXPERTS_PER_GROUP
D_FF_EXPERT = 512

LANES = 128
META_PAD = LANES
VMEM_LIMIT = 56 * 1024 * 1024
LOG2E = 1.4426950408889634
NT_DIMS = (((1,), (1,)), ((), ()))

SM_KI = 0
SM_WI = IDX_DIM
SM_FB = IDX_DIM + IDX_HEADS
RT_E0 = N_GROUPS


def _cparams(sem):
    return pltpu.CompilerParams(dimension_semantics=sem, vmem_limit_bytes=VMEM_LIMIT)


def _rms_kernel(x_ref, g_ref, o_ref):
    x = x_ref[...]
    ms = jnp.mean(x * x, axis=-1, keepdims=True)
    o_ref[...] = (x * lax.rsqrt(ms + EPS) * g_ref[...]).astype(o_ref.dtype)


def _rmsnorm(x, g, tm):
    n, d = x.shape
    return pl.pallas_call(
        _rms_kernel,
        grid=(n // tm,),
        in_specs=[pl.BlockSpec((tm, d), lambda i: (i, 0)), pl.BlockSpec((1, d), lambda i: (0, 0))],
        out_specs=pl.BlockSpec((tm, d), lambda i: (i, 0)),
        out_shape=jax.ShapeDtypeStruct((n, d), BF16),
        compiler_params=_cparams(("parallel",)),
        name="rmsnorm",
    )(x, g.reshape(1, d))


def _rope_tile(y, cos, sin_signed, half):
    if 2 * half == LANES:
        partner = pltpu.roll(y, half, axis=1)
    else:
        lane = lax.broadcasted_iota(jnp.int32, y.shape, 1)
        first = (lane % (2 * half)) < half
        partner = jnp.where(first, pltpu.roll(y, LANES - half, axis=1), pltpu.roll(y, half, axis=1))
    return y * cos + partner * sin_signed


def _rope_tables(pos, half):
    inv_freq = ROPE_THETA ** (-jnp.arange(half, dtype=F32) / half)
    ang = pos.astype(F32)[:, None] * inv_freq[None, :]
    cos, sin = jnp.cos(ang), jnp.sin(ang)
    reps = LANES // (2 * half)
    cos_t = jnp.tile(jnp.concatenate([cos, cos], axis=1), (1, reps))
    sin_t = jnp.tile(jnp.concatenate([-sin, sin], axis=1), (1, reps))
    return cos_t, sin_t


PACK_ROWS = 256


def _pack_kernel(src_ref, x_ref, o_ref):
    del src_ref
    o_ref[...] = x_ref[...].T.astype(o_ref.dtype)


def _pack_w_in(wt, sections):
    _, d = wt.shape
    starts = []
    for lo, hi in sections:
        assert lo % 8 == 0 and (hi - lo) % PACK_ROWS == 0
        starts += list(range(lo, hi, PACK_ROWS))
    src = jnp.asarray([s // 8 for s in starts], jnp.int32)
    return pl.pallas_call(
        _pack_kernel,
        grid_spec=pltpu.PrefetchScalarGridSpec(
            num_scalar_prefetch=1,
            grid=(len(starts),),
            in_specs=[pl.BlockSpec((pl.Element(PACK_ROWS), pl.Element(d)), lambda j, src: (src[j] * 8, 0))],
            out_specs=pl.BlockSpec((d, PACK_ROWS), lambda j, src: (0, j))),
        out_shape=jax.ShapeDtypeStruct((d, len(starts) * PACK_ROWS), BF16),
        compiler_params=_cparams(("arbitrary",)),
        name="pack_w_in",
    )(src, wt)


def _pack_small_kernel(a_ref, b_ref, o_ref, *, n_a, n_b):
    lane = lax.broadcasted_iota(jnp.int32, o_ref.shape, 1)
    o_ref[...] = jnp.where(lane < n_a, a_ref[...].T,
                           jnp.where(lane < n_a + n_b, b_ref[...].T, 0.0)).astype(o_ref.dtype)


def _pack_small(wt, rows_a, rows_b):
    _, d = wt.shape
    n_a, n_b = rows_a[1] - rows_a[0], rows_b[1] - rows_b[0]
    start_a, start_b = rows_a[0], rows_b[0] - n_a
    assert start_a % 8 == 0 and start_b % 8 == 0 and start_b >= 0 and n_a + n_b <= LANES
    window = lambda start: pl.BlockSpec((pl.Element(LANES), pl.Element(d)), lambda i: (start, 0))
    return pl.pallas_call(
        functools.partial(_pack_small_kernel, n_a=n_a, n_b=n_b),
        grid=(1,),
        in_specs=[window(start_a), window(start_b)],
        out_specs=pl.BlockSpec((d, LANES), lambda i: (0, 0)),
        out_shape=jax.ShapeDtypeStruct((d, LANES), BF16),
        compiler_params=_cparams(("arbitrary",)),
        name="pack_w_small",
    )(wt, wt)


def _proj_plain_kernel(h_ref, w_ref, o_ref, *, first_tile_scale):
    y = jnp.dot(h_ref[...], w_ref[...], preferred_element_type=F32)
    scale = jnp.where(pl.program_id(1) == 0, jnp.float32(first_tile_scale), jnp.float32(1.0))
    o_ref[...] = (y * scale).astype(o_ref.dtype)


def _proj_plain(h, w, col0, nc, tm, tn, first_tile_scale=1.0):
    n, d = h.shape
    c0 = col0 // tn
    return pl.pallas_call(
        functools.partial(_proj_plain_kernel, first_tile_scale=first_tile_scale),
        grid=(n // tm, nc // tn),
        in_specs=[pl.BlockSpec((tm, d), lambda i, j: (i, 0)), pl.BlockSpec((d, tn), lambda i, j: (0, c0 + j))],
        out_specs=pl.BlockSpec((tm, tn), lambda i, j: (i, j)),
        out_shape=jax.ShapeDtypeStruct((n, nc), BF16),
        compiler_params=_cparams(("parallel", "parallel")),
        name="proj_plain",
    )(h, w)


def _proj_rope_kernel(h_ref, w_ref, cos_ref, sin_ref, o_ref, *, half, head_major, out_scale):
    y = jnp.dot(h_ref[...], w_ref[...], preferred_element_type=F32)
    cos, sin = cos_ref[...], sin_ref[...]
    per_tile = LANES // (2 * half)
    for c in range(y.shape[1] // LANES):
        sl = slice(c * LANES, (c + 1) * LANES)
        roped = (_rope_tile(y[:, sl], cos, sin, half) * out_scale).astype(o_ref.dtype)
        if head_major:
            for k in range(per_tile):
                o_ref[c * per_tile + k] = roped[:, k * 2 * half:(k + 1) * 2 * half]
        else:
            o_ref[:, sl] = roped


def _proj_rope(h, w, col0, nc, cos, sin, half, tm, tn, head_major=False, out_scale=1.0):
    n, d = h.shape
    c0 = col0 // tn
    if head_major:
        assert tn == nc
        heads = nc // (2 * half)
        out_spec = pl.BlockSpec((heads, tm, 2 * half), lambda i, j: (0, i, 0))
        out_shape = jax.ShapeDtypeStruct((heads, n, 2 * half), BF16)
    else:
        out_spec = pl.BlockSpec((tm, tn), lambda i, j: (i, j))
        out_shape = jax.ShapeDtypeStruct((n, nc), BF16)
    return pl.pallas_call(
        functools.partial(_proj_rope_kernel, half=half, head_major=head_major, out_scale=out_scale),
        grid=(n // tm, nc // tn),
        in_specs=[pl.BlockSpec((tm, d), lambda i, j: (i, 0)), pl.BlockSpec((d, tn), lambda i, j: (0, c0 + j)),
                  pl.BlockSpec((tm, LANES), lambda i, j: (i, 0)), pl.BlockSpec((tm, LANES), lambda i, j: (i, 0))],
        out_specs=out_spec,
        out_shape=out_shape,
        compiler_params=_cparams(("parallel", "parallel")),
        name="proj_rope",
    )(h, w, cos, sin)


def _kv_kernel(h_ref, wc_ref, g_ref, wup_ref, cos_ref, sin_ref, ka_ref, va_ref):
    c = jnp.dot(h_ref[...], wc_ref[...], preferred_element_type=F32)
    ms = jnp.mean(c * c, axis=-1, keepdims=True)
    cn = (c * lax.rsqrt(ms + EPS) * g_ref[...]).astype(BF16)
    kv = jnp.dot(cn, wup_ref[...], preferred_element_type=F32)
    cos, sin = cos_ref[...], sin_ref[...]
    for hd in range(A_HEADS):
        sl = slice(hd * HEAD_DIM, (hd + 1) * HEAD_DIM)
        ka_ref[:, sl] = _rope_tile(kv[:, sl], cos, sin, HEAD_DIM // 2).astype(ka_ref.dtype)
    va_ref[...] = kv[:, A_WIDTH:].astype(va_ref.dtype)


def _proj_kv(h, wc, col0, g_kv, wup, cos, sin, tm):
    n, d = h.shape
    c0 = col0 // A_KV_RANK
    return pl.pallas_call(
        _kv_kernel,
        grid=(n // tm,),
        in_specs=[pl.BlockSpec((tm, d), lambda i: (i, 0)),
                  pl.BlockSpec((d, A_KV_RANK), lambda i: (0, c0)),
                  pl.BlockSpec((1, A_KV_RANK), lambda i: (0, 0)),
                  pl.BlockSpec((A_KV_RANK, 2 * A_WIDTH), lambda i: (0, 0)),
                  pl.BlockSpec((tm, LANES), lambda i: (i, 0)),
                  pl.BlockSpec((tm, LANES), lambda i: (i, 0))],
        out_specs=[pl.BlockSpec((tm, A_WIDTH), lambda i: (i, 0)), pl.BlockSpec((tm, A_WIDTH), lambda i: (i, 0))],
        out_shape=[jax.ShapeDtypeStruct((n, A_WIDTH), BF16), jax.ShapeDtypeStruct((n, A_WIDTH), BF16)],
        compiler_params=_cparams(("parallel",)),
        name="proj_kv",
    )(h, wc, g_kv.reshape(1, A_KV_RANK), wup, cos, sin)


def _small_kernel(h_ref, w_ref, g_ref, bf_ref, c0_ref, cos_ref, sin_ref, o_ref, carry_ref):
    tm = h_ref.shape[0]

    @pl.when(pl.program_id(0) == 0)
    def _():
        carry_ref[...] = c0_ref[...]

    y = jnp.dot(h_ref[...], w_ref[...], preferred_element_type=F32)
    lane = lax.broadcasted_iota(jnp.int32, y.shape, 1)
    is_ki = lane < SM_WI
    ms = jnp.sum(jnp.where(is_ki, y * y, 0.0), axis=-1, keepdims=True) * (1.0 / IDX_DIM)
    ki = _rope_tile(y * lax.rsqrt(ms + EPS) * g_ref[...], cos_ref[...], sin_ref[...], IDX_DIM // 2)
    wi = y * (IDX_HEADS ** -0.5 * IDX_DIM ** -0.5)
    z = y + bf_ref[...]
    logf = -(jnp.maximum(-z, 0.0) + jnp.log(1.0 + jnp.exp(-jnp.abs(z))))
    hi = logf.astype(BF16)
    r1 = logf - hi.astype(F32)
    mid = r1.astype(BF16)
    lo = (r1 - mid.astype(F32)).astype(BF16)
    row = lax.broadcasted_iota(jnp.int32, (tm, tm), 0)
    col = lax.broadcasted_iota(jnp.int32, (tm, tm), 1)
    tri = (col <= row).astype(BF16)
    csum = (jnp.dot(tri, hi, preferred_element_type=F32) + jnp.dot(tri, mid, preferred_element_type=F32)
            + jnp.dot(tri, lo, preferred_element_type=F32)) + carry_ref[0:1, :]
    carry_ref[...] = jnp.broadcast_to(csum[tm - 1:tm, :], carry_ref.shape)
    o_ref[...] = jnp.where(is_ki, ki, jnp.where(lane < SM_FB, wi, csum))


def _proj_small(h, w, col0, g_idx_pad, bf_pad, c0, cos, sin, tm):
    n, d = h.shape
    cb = col0 // LANES
    return pl.pallas_call(
        _small_kernel,
        grid=(n // tm,),
        in_specs=[pl.BlockSpec((tm, d), lambda i: (i, 0)),
                  pl.BlockSpec((d, LANES), lambda i: (0, cb)),
                  pl.BlockSpec((1, LANES), lambda i: (0, 0)),
                  pl.BlockSpec((1, LANES), lambda i: (0, 0)),
                  pl.BlockSpec((8, LANES), lambda i: (0, 0)),
                  pl.BlockSpec((tm, LANES), lambda i: (i, 0)),
                  pl.BlockSpec((tm, LANES), lambda i: (i, 0))],
        out_specs=pl.BlockSpec((tm, LANES), lambda i: (i, 0)),
        out_shape=jax.ShapeDtypeStruct((n, LANES), F32),
        scratch_shapes=[pltpu.VMEM((8, LANES), F32)],
        compiler_params=_cparams(("arbitrary",)),
        name="proj_small",
    )(h, w, g_idx_pad, bf_pad, c0, cos, sin)


COUNT_ROWS = 64


def _key_to_float(key):
    bits = key ^ ((key >> 31) & jnp.int32(0x7FFFFFFF))
    return lax.bitcast_convert_type(bits, F32)


def _select_kernel(qi_ref, kir_ref, kim_ref, wt_ref, or_ref, om_ref, s_ref, *, tq, topk):
    i = pl.program_id(0)
    n_real = or_ref.shape[1]
    n_chunks = n_real // tq
    krow = lax.broadcasted_iota(jnp.int32, (tq, tq), 0)
    qcol = lax.broadcasted_iota(jnp.int32, (tq, tq), 1)
    mrow = lax.broadcasted_iota(jnp.int32, (META_PAD, tq), 0)
    wt = wt_ref[...]

    def scores(kc):
        s = jnp.zeros((kc.shape[0], tq), F32)
        for h in range(IDX_HEADS):
            d = lax.dot_general(kc, qi_ref[h], NT_DIMS, preferred_element_type=F32)
            s = s + wt[h:h + 1, :] * jnp.maximum(d, 0.0)
        return s

    def causal(c):
        return krow + (c - i) * tq <= qcol

    def score_chunk(c, _):
        off = pl.multiple_of(c * tq, tq)
        s_ref[pl.ds(off, tq), :] = jnp.where(causal(c), scores(kir_ref[pl.ds(off, tq), :]), NEG)
        return 0

    lax.fori_loop(0, i + 1, score_chunk, 0)
    s_ref[n_real:, :] = jnp.where(mrow < N_META, scores(kim_ref[...]), NEG)

    def count(cmp, thr):
        def body(c, acc):
            off = c * tq
            for r in range(tq // COUNT_ROWS):
                blk = s_ref[pl.ds(pl.multiple_of(off + r * COUNT_ROWS, COUNT_ROWS), COUNT_ROWS), :]
                acc = acc + jnp.where(cmp(blk, thr), 1.0, 0.0)
            return acc

        acc = lax.fori_loop(0, i + 1, body, jnp.zeros((COUNT_ROWS, tq), F32))
        for r in range(META_PAD // COUNT_ROWS):
            blk = s_ref[n_real + r * COUNT_ROWS:n_real + (r + 1) * COUNT_ROWS, :]
            acc = acc + jnp.where(cmp(blk, thr), 1.0, 0.0)
        return jnp.sum(acc, axis=0, keepdims=True)

    kf = jnp.float32(topk)

    def bit_step(b, carry):
        t_key, cnt_t = carry
        cand = t_key + lax.shift_left(jnp.int32(1), 31 - b)
        cnt = count(lambda s, t: s >= t, _key_to_float(cand))
        take = cnt >= kf
        return jnp.where(take, cand, t_key), jnp.where(take, cnt, cnt_t)

    t0 = jnp.full((1, tq), jnp.iinfo(jnp.int32).min, jnp.int32)
    c0 = jnp.broadcast_to(((i + 1) * tq + META_PAD).astype(F32), (1, tq))
    t_key, cnt_t = lax.fori_loop(0, 32, bit_step, (t0, c0))
    thr = _key_to_float(t_key)

    def fill_chunk(c, _):
        or_ref[:, pl.ds(pl.multiple_of(c * tq, tq), tq)] = jnp.full((tq, tq), NEG, or_ref.dtype)
        return 0

    lax.fori_loop(i + 1, n_chunks, fill_chunk, 0)

    ambiguous = jnp.logical_and(cnt_t > kf, thr > NEG)
    any_amb = jnp.max(jnp.where(ambiguous, 1.0, 0.0)) > 0.0

    @pl.when(jnp.logical_not(any_amb))
    def _():
        def out_chunk(c, _):
            off = pl.multiple_of(c * tq, tq)
            sel = jnp.where(s_ref[pl.ds(off, tq), :] >= thr, 0.0, NEG)
            or_ref[:, pl.ds(off, tq)] = jnp.where(causal(c), sel, NEG).T.astype(or_ref.dtype)
            return 0

        lax.fori_loop(0, i + 1, out_chunk, 0)
        sel = jnp.where(s_ref[n_real:, :] >= thr, 0.0, NEG)
        om_ref[...] = jnp.where(mrow < N_META, sel, NEG).T.astype(om_ref.dtype)

    @pl.when(any_amb)
    def _():
        cnt_gt = count(lambda s, t: s > t, thr)
        quota = jnp.where(ambiguous, kf - cnt_gt, jnp.float32(3e38))

        def tie_select(s, valid, running):
            rows = s.shape[0]
            r = lax.broadcasted_iota(jnp.int32, (rows, rows), 0)
            c_ = lax.broadcasted_iota(jnp.int32, (rows, rows), 1)
            lower = (c_ <= r).astype(BF16)
            eq = jnp.where(s == thr, 1.0, 0.0)
            rank = jnp.dot(lower, eq.astype(BF16), preferred_element_type=F32) + running
            keep = jnp.where(s > thr, 0.0, jnp.where(jnp.logical_and(eq > 0.0, rank <= quota), 0.0, NEG))
            return jnp.where(valid, keep, NEG), running + jnp.sum(eq, axis=0, keepdims=True)

        bias_m, running = tie_select(s_ref[n_real:, :], mrow < N_META, jnp.zeros((1, tq), F32))
        om_ref[...] = bias_m.T.astype(om_ref.dtype)

        def out_chunk(c, running):
            off = pl.multiple_of(c * tq, tq)
            bias, running = tie_select(s_ref[pl.ds(off, tq), :], causal(c), running)
            or_ref[:, pl.ds(off, tq)] = bias.T.astype(or_ref.dtype)
            return running

        lax.fori_loop(0, i + 1, out_chunk, running)


def _select_bias(qi_hm, ki_real, ki_meta, wt, topk, tq):
    _, t, _ = qi_hm.shape
    return pl.pallas_call(
        functools.partial(_select_kernel, tq=tq, topk=topk),
        grid=(t // tq,),
        in_specs=[pl.BlockSpec((IDX_HEADS, tq, IDX_DIM), lambda i: (0, i, 0)),
                  pl.BlockSpec((t, IDX_DIM), lambda i: (0, 0)),
                  pl.BlockSpec((META_PAD, IDX_DIM), lambda i: (0, 0)),
                  pl.BlockSpec((IDX_HEADS, tq), lambda i: (0, i))],
        out_specs=[pl.BlockSpec((tq, t), lambda i: (i, 0)), pl.BlockSpec((tq, META_PAD), lambda i: (i, 0))],
        out_shape=[jax.ShapeDtypeStruct((t, t), BF16), jax.ShapeDtypeStruct((t, META_PAD), BF16)],
        scratch_shapes=[pltpu.VMEM((t + META_PAD, tq), F32)],
        compiler_params=_cparams(("parallel",)),
        name="select_bias",
    )(qi_hm, ki_real, ki_meta, wt)


def _attn_kernel(*refs, t, fox):
    if fox:
        (qi_ref, kj_ref, q_ref, k_ref, v_ref, km_ref, vm_ref, cq_ref, ck_ref, ckm_ref,
         o_ref, m_ref, l_ref, acc_ref, r_ref) = refs
        s_ref = None
    else:
        (qi_ref, kj_ref, q_ref, k_ref, v_ref, km_ref, vm_ref, b_ref, bm_ref,
         o_ref, m_ref, l_ref, acc_ref, s_ref) = refs
    stage_logits = s_ref is not None
    step = pl.program_id(0)
    i = qi_ref[step]
    j = kj_ref[step]
    nheads = q_ref.shape[1] // HEAD_DIM

    def block(k_blk_ref, v_blk_ref, bias_fn):
        width = k_blk_ref.shape[0]

        def logits(h):
            sl = slice(h * HEAD_DIM, (h + 1) * HEAD_DIM)
            return lax.dot_general(q_ref[:, sl], k_blk_ref[:, sl], NT_DIMS, preferred_element_type=F32)

        if stage_logits:
            for h in range(nheads):
                s_ref[h, :, :width] = logits(h)
        s_next = None if stage_logits else logits(0)
        for h in range(nheads):
            sl = slice(h * HEAD_DIM, (h + 1) * HEAD_DIM)
            if stage_logits:
                s = s_ref[h, :, :width]
            else:
                s, s_next = s_next, (logits(h + 1) if h + 1 < nheads else None)
            u = bias_fn(s, h)
            m_prev = m_ref[h]
            mx = jnp.max(u, axis=1, keepdims=True)
            if fox:
                r = r_ref[h]
                m_next = jnp.maximum(m_prev, mx + r)
                sub = m_next - r
            else:
                m_next = jnp.maximum(m_prev, mx)
                sub = m_next
            alpha = jnp.exp2(m_prev - m_next)
            ps = [jnp.exp2(u[:, c * LANES:(c + 1) * LANES] - sub) for c in range(width // LANES)]
            psum = ps[0]
            for pc in ps[1:]:
                psum = psum + pc
            p = ps[0] if len(ps) == 1 else jnp.concatenate(ps, axis=1)
            l_ref[h] = alpha * l_ref[h] + psum
            m_ref[h] = m_next
            acc_ref[h] = alpha * acc_ref[h] + jnp.dot(p.astype(BF16), v_blk_ref[:, sl],
                                                      preferred_element_type=F32)

    @pl.when(j == 0)
    def _():
        m_ref[...] = jnp.full(m_ref.shape, NEG, F32)
        l_ref[...] = jnp.zeros(l_ref.shape, F32)
        acc_ref[...] = jnp.zeros(acc_ref.shape, F32)
        mcol = lax.broadcasted_iota(jnp.int32, (t, META_PAD), 1)
        if fox:
            for h in range(nheads):
                r_ref[h] = jnp.broadcast_to(cq_ref[:, h:h + 1] * LOG2E, (t, LANES))

            def bias_fn(u, h):
                return jnp.where(mcol < N_META, u - ckm_ref[h:h + 1, :] * LOG2E, NEG)
        else:
            bias_m = bm_ref[...].astype(F32)

            def bias_fn(u, h):
                return u + bias_m
        block(km_ref, vm_ref, bias_fn)

    if fox:
        @pl.when(j < i)
        def _():
            block(k_ref, v_ref, lambda u, h: u - ck_ref[h:h + 1, :] * LOG2E)

        @pl.when(j == i)
        def _():
            row = lax.broadcasted_iota(jnp.int32, (t, t), 0)
            col = lax.broadcasted_iota(jnp.int32, (t, t), 1)
            block(k_ref, v_ref, lambda u, h: jnp.where(col <= row, u - ck_ref[h:h + 1, :] * LOG2E, NEG))
    else:
        bias = b_ref[...].astype(F32)
        block(k_ref, v_ref, lambda u, h: u + bias)

    @pl.when(j == i)
    def _():
        for h in range(nheads):
            sl = slice(h * HEAD_DIM, (h + 1) * HEAD_DIM)
            o_ref[:, sl] = (acc_ref[h] / jnp.sum(l_ref[h], axis=1, keepdims=True)).astype(o_ref.dtype)


def _attention(qkv, meta_kv, extra, *, fox, t_blk):
    q_arr, q_col, k_arr, k_col, v_arr, v_col = qkv
    km_arr, km_col, vm_arr, vm_col = meta_kv
    t = q_arr.shape[0]
    w = A_WIDTH
    nheads = w // HEAD_DIM
    nq = t // t_blk
    qi = jnp.asarray([i for i in range(nq) for _ in range(i + 1)], jnp.int32)
    kj = jnp.asarray([j for i in range(nq) for j in range(i + 1)], jnp.int32)

    in_specs = [pl.BlockSpec((t_blk, w), lambda s, qi, kj: (qi[s], q_col)),
                pl.BlockSpec((t_blk, w), lambda s, qi, kj: (kj[s], k_col)),
                pl.BlockSpec((t_blk, w), lambda s, qi, kj: (kj[s], v_col)),
                pl.BlockSpec((META_PAD, w), lambda s, qi, kj: (0, km_col)),
                pl.BlockSpec((META_PAD, w), lambda s, qi, kj: (0, vm_col))]
    scratch = [pltpu.VMEM((nheads, t_blk, LANES), F32), pltpu.VMEM((nheads, t_blk, LANES), F32),
               pltpu.VMEM((nheads, t_blk, HEAD_DIM), F32)]
    args = (q_arr, k_arr, v_arr, km_arr, vm_arr)
    if fox:
        cq, ct_real, ct_meta = extra
        in_specs += [pl.BlockSpec((t_blk, nheads), lambda s, qi, kj: (qi[s], 0)),
                     pl.BlockSpec((nheads, t_blk), lambda s, qi, kj: (0, kj[s])),
                     pl.BlockSpec((nheads, META_PAD), lambda s, qi, kj: (0, 0))]
        args += (cq, ct_real, ct_meta)
        scratch.append(pltpu.VMEM((nheads, t_blk, LANES), F32))
    else:
        bias_real, bias_meta = extra
        in_specs += [pl.BlockSpec((t_blk, t_blk), lambda s, qi, kj: (qi[s], kj[s])),
                     pl.BlockSpec((t_blk, META_PAD), lambda s, qi, kj: (qi[s], 0))]
        args += (bias_real, bias_meta)
        scratch.append(pltpu.VMEM((nheads, t_blk, t_blk), F32))
    return pl.pallas_call(
        functools.partial(_attn_kernel, t=t_blk, fox=fox),
        grid_spec=pltpu.PrefetchScalarGridSpec(
            num_scalar_prefetch=2,
            grid=(int(qi.shape[0]),),
            in_specs=in_specs,
            out_specs=pl.BlockSpec((t_blk, w), lambda s, qi, kj: (qi[s], 0)),
            scratch_shapes=scratch),
        out_shape=jax.ShapeDtypeStruct((t, w), BF16),
        compiler_params=_cparams(("arbitrary",)),
        name="attn_fox" if fox else "attn_dsa",
    )(qi, kj, *args)


def _route(logits):
    lane = lax.broadcasted_iota(jnp.int32, logits.shape, 1).astype(F32)
    big = jnp.float32(LANES)
    ninf = -jnp.inf
    gl = jnp.where(lane < N_GROUPS, logits, ninf)
    gmax = jnp.max(gl, axis=1, keepdims=True)
    g_sel = jnp.min(jnp.where(gl == gmax, lane, big), axis=1, keepdims=True)
    p_group = 1.0 / jnp.sum(jnp.exp(gl - gmax), axis=1, keepdims=True)
    e_lo = RT_E0 + g_sel * EXPERTS_PER_GROUP
    in_grp = jnp.logical_and(lane >= e_lo, lane < e_lo + EXPERTS_PER_GROUP)
    el = jnp.where(in_grp, logits, ninf)
    emax = jnp.max(el, axis=1, keepdims=True)
    ex = jnp.exp(el - emax)
    prob = ex / jnp.sum(ex, axis=1, keepdims=True)
    prob = jnp.where(in_grp, prob, -1.0)
    p1 = jnp.max(prob, axis=1, keepdims=True)
    i1 = jnp.min(jnp.where(prob == p1, lane, big), axis=1, keepdims=True)
    rest = jnp.where(lane == i1, -1.0, prob)
    p2 = jnp.max(rest, axis=1, keepdims=True)
    i2 = jnp.min(jnp.where(rest == p2, lane, big), axis=1, keepdims=True)
    tot = p1 + p2
    w1 = p_group * (p1 / tot)
    w2 = p_group * (p2 / tot)
    return jnp.where(lane == i1, w1, jnp.where(lane == i2, w2, 0.0)), g_sel


def _merge_kernel(h_ref, ya_ref, yb_ref, wga_ref, wgb_ref, wa_ref, wb_ref, wo_ref, x_ref, g_ref,
                  wr_ref, br_ref, h1_ref, xr_ref, plan_ref, acc_ref, cnt_ref):
    n = pl.program_id(1)

    @pl.when(jnp.logical_and(pl.program_id(0) == 0, n == 0))
    def _():
        cnt_ref[...] = jnp.zeros(cnt_ref.shape, F32)

    @pl.when(n == 0)
    def _():
        acc_ref[...] = jnp.zeros(acc_ref.shape, F32)

    h = h_ref[...]
    ga = jnp.dot(h, wga_ref[...], preferred_element_type=F32)
    gb = jnp.dot(h, wgb_ref[...], preferred_element_type=F32)
    pa = jnp.dot(ya_ref[...], wa_ref[...], preferred_element_type=F32)
    pb = jnp.dot(yb_ref[...], wb_ref[...], preferred_element_type=F32)
    merged = jax.nn.sigmoid(ga) * pa + jax.nn.sigmoid(gb) * pb
    acc_ref[...] += jnp.dot(merged.astype(BF16), wo_ref[...], preferred_element_type=F32)

    @pl.when(n == pl.num_programs(1) - 1)
    def _():
        h1 = x_ref[...] + acc_ref[...]
        h1_ref[...] = h1
        ms = jnp.mean(h1 * h1, axis=-1, keepdims=True)
        t = h1 * lax.rsqrt(ms + EPS) * g_ref[...]
        t_hi = t.astype(BF16)
        t_lo = (t - t_hi.astype(F32)).astype(BF16)
        wr = wr_ref[...]
        y = jnp.dot(t_hi, wr, preferred_element_type=F32)
        logits = (y[:, :LANES] + y[:, LANES:]
                  + jnp.dot(t_lo, wr[:, :LANES], preferred_element_type=F32)) + br_ref[...]
        gates, g_sel = _route(logits)
        d = t.shape[1]
        xr_ref[:, :d] = t
        xr_ref[:, d:] = gates
        tm = t.shape[0]
        lane = lax.broadcasted_iota(jnp.int32, gates.shape, 1).astype(F32)
        onehot = jnp.where(lane == g_sel, 1.0, 0.0).astype(BF16)
        rr = lax.broadcasted_iota(jnp.int32, (tm, tm), 0)
        cc = lax.broadcasted_iota(jnp.int32, (tm, tm), 1)
        cum = jnp.dot((cc <= rr).astype(BF16), onehot, preferred_element_type=F32) + cnt_ref[0:1, :]
        cnt_ref[...] = jnp.broadcast_to(cum[tm - 1:tm, :], cnt_ref.shape)
        plan_ref[...] = jnp.where(lane < N_GROUPS, cum, jnp.where(lane == N_GROUPS, g_sel, 0.0))


def _merge(h, ya, yb, w_all, ga_col0, gb_col0, wa, wb, wo, x, g_ffn, wr_split, br, tm, tn):
    t, d = h.shape
    row = lambda i, n: (i, 0)
    ga0, gb0 = ga_col0 // tn, gb_col0 // tn
    wga = wgb = w_all
    return pl.pallas_call(
        _merge_kernel,
        grid=(t // tm, d // tn),
        in_specs=[pl.BlockSpec((tm, d), row),
                  pl.BlockSpec((tm, A_WIDTH), row),
                  pl.BlockSpec((tm, B_WIDTH), row),
                  pl.BlockSpec((d, tn), lambda i, n: (0, ga0 + n)),
                  pl.BlockSpec((d, tn), lambda i, n: (0, gb0 + n)),
                  pl.BlockSpec((A_WIDTH, tn), lambda i, n: (0, n)),
                  pl.BlockSpec((B_WIDTH, tn), lambda i, n: (0, n)),
                  pl.BlockSpec((tn, d), lambda i, n: (n, 0)),
                  pl.BlockSpec((tm, d), row),
                  pl.BlockSpec((1, d), lambda i, n: (0, 0)),
                  pl.BlockSpec((d, 2 * LANES), lambda i, n: (0, 0)),
                  pl.BlockSpec((1, LANES), lambda i, n: (0, 0))],
        out_specs=[pl.BlockSpec((tm, d), row), pl.BlockSpec((tm, d + LANES), row),
                   pl.BlockSpec((tm, LANES), row)],
        out_shape=[jax.ShapeDtypeStruct((t, d), F32), jax.ShapeDtypeStruct((t, d + LANES), F32),
                   jax.ShapeDtypeStruct((t, LANES), F32)],
        scratch_shapes=[pltpu.VMEM((tm, d), F32), pltpu.VMEM((8, LANES), F32)],
        compiler_params=_cparams(("arbitrary", "arbitrary")),
        name="merge_out_router",
    )(h, ya, yb, wga, wgb, wa, wb, wo, x, g_ffn.reshape(1, d), wr_split, br)


def _row_copy(src_ref, src_row, dst_ref, dst_row, sem):
    return pltpu.make_async_copy(src_ref.at[pl.ds(src_row, 1), :], dst_ref.at[pl.ds(dst_row, 1), :], sem)


def _dispatch_kernel(pos_ref, x_ref, init_ref, xs_ref, sem):
    del init_ref
    i = pl.program_id(0)
    tm = x_ref.shape[0]

    def issue(r, _):
        _row_copy(x_ref, r, xs_ref, pos_ref[i * tm + r], sem).start()
        return 0

    lax.fori_loop(0, tm, issue, 0)

    def drain(r, _):
        _row_copy(x_ref, r, xs_ref, pos_ref[i * tm + r], sem).wait()
        return 0

    lax.fori_loop(0, tm, drain, 0)


def _dispatch(pos, xr, n_slots, tm):
    t, w = xr.shape
    return pl.pallas_call(
        _dispatch_kernel,
        grid_spec=pltpu.PrefetchScalarGridSpec(
            num_scalar_prefetch=1,
            grid=(t // tm,),
            in_specs=[pl.BlockSpec((tm, w), lambda i, pos: (i, 0)),
                      pl.BlockSpec(memory_space=pl.ANY)],
            out_specs=pl.BlockSpec(memory_space=pl.ANY),
            scratch_shapes=[pltpu.SemaphoreType.DMA]),
        out_shape=jax.ShapeDtypeStruct((n_slots, w), xr.dtype),
        input_output_aliases={2: 0},
        compiler_params=_cparams(("arbitrary",)),
        name="moe_dispatch",
    )(pos, xr, jnp.zeros((n_slots, w), xr.dtype))


def _group_ffn_kernel(tg_ref, tv_ref, x_ref, wg_ref, wu_ref, wd_ref, o_ref, acc_ref, xb_ref):
    j = pl.program_id(0)
    k = pl.program_id(1)
    d = o_ref.shape[1]

    @pl.when(k == 0)
    def _():
        acc_ref[...] = jnp.zeros(acc_ref.shape, F32)
        xb_ref[...] = x_ref[:, :d].astype(BF16)

    @pl.when(tv_ref[j] > 0)
    def _():
        gates = x_ref[:, d:]
        lane = lax.broadcasted_iota(jnp.int32, gates.shape, 1)
        e_lane = RT_E0 + tg_ref[j] * EXPERTS_PER_GROUP + k
        g_e = jnp.sum(jnp.where(lane == e_lane, gates, 0.0), axis=1, keepdims=True)
        xb = xb_ref[...]
        hg = jnp.dot(xb, wg_ref[0].astype(BF16), preferred_element_type=F32)
        hu = jnp.dot(xb, wu_ref[0].astype(BF16), preferred_element_type=F32)
        act = jax.nn.silu(hg) * hu * g_e
        acc_ref[...] += jnp.dot(act.astype(BF16), wd_ref[0].astype(BF16), preferred_element_type=F32)

    @pl.when(k == pl.num_programs(1) - 1)
    def _():
        o_ref[...] = acc_ref[...]


def _group_ffn(tile_group, tile_valid, xs, wg, wu, wd, ts):
    n_slots, w = xs.shape
    d = w - LANES
    f = wg.shape[2]
    expert = lambda j, k, tg, tv: (tg[j] * EXPERTS_PER_GROUP + k, 0, 0)
    return pl.pallas_call(
        _group_ffn_kernel,
        grid_spec=pltpu.PrefetchScalarGridSpec(
            num_scalar_prefetch=2,
            grid=(n_slots // ts, EXPERTS_PER_GROUP),
            in_specs=[pl.BlockSpec((ts, w), lambda j, k, tg, tv: (j, 0)),
                      pl.BlockSpec((1, d, f), expert),
                      pl.BlockSpec((1, d, f), expert),
                      pl.BlockSpec((1, f, d), expert)],
            out_specs=pl.BlockSpec((ts, d), lambda j, k, tg, tv: (j, 0)),
            scratch_shapes=[pltpu.VMEM((ts, d), F32), pltpu.VMEM((ts, d), BF16)]),
        out_shape=jax.ShapeDtypeStruct((n_slots, d), F32),
        compiler_params=_cparams(("arbitrary", "arbitrary")),
        name="moe_group_ffn",
    )(tile_group, tile_valid, xs, wg, wu, wd)


def _combine_kernel(pos_ref, ys_ref, h1_ref, gf_ref, o_ref, ybuf_ref, sem):
    i = pl.program_id(0)
    tm = h1_ref.shape[0]

    def issue(r, _):
        _row_copy(ys_ref, pos_ref[i * tm + r], ybuf_ref, r, sem).start()
        return 0

    lax.fori_loop(0, tm, issue, 0)

    def drain(r, _):
        _row_copy(ys_ref, pos_ref[i * tm + r], ybuf_ref, r, sem).wait()
        return 0

    lax.fori_loop(0, tm, drain, 0)
    h2 = h1_ref[...] + ybuf_ref[...]
    ms = jnp.mean(h2 * h2, axis=-1, keepdims=True)
    o_ref[...] = h2 * lax.rsqrt(ms + EPS) * gf_ref[...]


def _combine(pos, ys, h1, g_final, tm):
    t, d = h1.shape
    return pl.pallas_call(
        _combine_kernel,
        grid_spec=pltpu.PrefetchScalarGridSpec(
            num_scalar_prefetch=1,
            grid=(t // tm,),
            in_specs=[pl.BlockSpec(memory_space=pl.ANY),
                      pl.BlockSpec((tm, d), lambda i, pos: (i, 0)),
                      pl.BlockSpec((1, d), lambda i, pos: (0, 0))],
            out_specs=pl.BlockSpec((tm, d), lambda i, pos: (i, 0)),
            scratch_shapes=[pltpu.VMEM((tm, d), F32), pltpu.SemaphoreType.DMA]),
        out_shape=jax.ShapeDtypeStruct((t, d), F32),
        compiler_params=_cparams(("arbitrary",)),
        name="moe_combine_final",
    )(pos, ys, h1, g_final.reshape(1, d))


def _group_plan(plan, ts):
    t = plan.shape[0]
    g_sel = plan[:, N_GROUPS].astype(jnp.int32)
    cum_sel = jnp.take_along_axis(plan[:, :N_GROUPS], g_sel[:, None], axis=1)[:, 0]
    counts = plan[t - 1, :N_GROUPS].astype(jnp.int32)
    padded = (counts + ts - 1) // ts * ts
    ends = jnp.cumsum(padded)
    starts = ends - padded
    pos = starts[g_sel] + cum_sel.astype(jnp.int32) - 1
    n_tiles = (t + N_GROUPS * (ts - 1)) // ts
    tile_start = jnp.arange(n_tiles, dtype=jnp.int32) * ts
    tile_group = jnp.minimum(jnp.sum(tile_start[:, None] >= ends[None, :], axis=1), N_GROUPS - 1)
    tile_valid = (tile_start < ends[N_GROUPS - 1]).astype(jnp.int32)
    return pos, tile_group.astype(jnp.int32), tile_valid, n_tiles * ts


def _pick(n, pref):
    return pref if n % pref == 0 else n


def _pad_rows(a, rows):
    return jnp.pad(a, ((0, rows - a.shape[0]), (0, 0)))


def _layer(x, meta, g_mix, w_in, g_kv, w_kv_up, g_idx_k, b_f, w_branch_a, w_branch_b, w_out,
           g_ffn, w_group, b_group, w_expert, b_expert, w_gate_e, w_up_e, w_down_e, g_final):
    t, d = x.shape
    n_all = t + N_META
    topk = min(TOPK_MAX, n_all // 4)

    o = [0]
    for wdt in (A_WIDTH, A_KV_RANK, IDX_HEADS * IDX_DIM, IDX_DIM, IDX_HEADS, B_WIDTH, B_WIDTH, B_WIDTH,
                B_HEADS, D_MODEL, D_MODEL):
        o.append(o[-1] + wdt)
    w_in_t = w_in.T
    w_all = _pack_w_in(w_in_t, [(o[0], o[1]), (o[2], o[3]), (o[5], o[8]), (o[9], o[10]), (o[10], o[11]),
                                (o[1], o[2])])
    w_small = _pack_small(w_in_t, (o[3], o[5]), (o[8], o[9]))
    col_qa = 0
    col_qi = col_qa + A_WIDTH
    col_b = col_qi + IDX_HEADS * IDX_DIM
    col_ga = col_b + 3 * B_WIDTH
    col_gb = col_ga + D_MODEL
    col_ckv = col_gb + D_MODEL
    w_up_bf = w_kv_up.astype(BF16)
    g_idx_pad = jnp.zeros((1, LANES), F32).at[0, SM_KI:SM_KI + IDX_DIM].set(g_idx_k)
    bf_pad = jnp.zeros((1, LANES), F32).at[0, SM_FB:SM_FB + B_HEADS].set(b_f)

    pos_meta = jnp.arange(N_META)
    pos_real = jnp.arange(N_META, n_all)
    cos64_m, sin64_m = _rope_tables(pos_meta, HEAD_DIM // 2)
    cos64_r, sin64_r = _rope_tables(pos_real, HEAD_DIM // 2)
    cos32_m, sin32_m = _rope_tables(pos_meta, IDX_DIM // 2)
    cos32_r, sin32_r = _rope_tables(pos_real, IDX_DIM // 2)

    tm = _pick(t, 512)

    hm = _rmsnorm(meta, g_mix, N_META)
    ka_m, va_m = _proj_kv(hm, w_all, col_ckv, g_kv, w_up_bf, cos64_m, sin64_m, N_META)
    small_m = _proj_small(hm, w_small, 0, g_idx_pad, bf_pad, jnp.zeros((8, LANES), F32), cos32_m, sin32_m,
                          N_META)
    b_m = _proj_plain(hm, w_all, col_b, 3 * B_WIDTH, N_META, B_WIDTH)
    c_carry = jnp.broadcast_to(small_m[N_META - 1:N_META, :], (8, LANES))

    h = _rmsnorm(x, g_mix, tm)
    qk_scale = HEAD_DIM ** -0.5 * LOG2E
    tmw = _pick(t, 1024)
    qa = _proj_rope(h, w_all, col_qa, A_WIDTH, cos64_r, sin64_r, HEAD_DIM // 2, tmw, A_WIDTH, out_scale=qk_scale)
    ka_r, va_r = _proj_kv(h, w_all, col_ckv, g_kv, w_up_bf, cos64_r, sin64_r, tm)
    qi_hm = _proj_rope(h, w_all, col_qi, IDX_HEADS * IDX_DIM, cos32_r, sin32_r, IDX_DIM // 2, tmw,
                       IDX_HEADS * IDX_DIM, head_major=True)
    small_r = _proj_small(h, w_small, 0, g_idx_pad, bf_pad, c_carry, cos32_r, sin32_r, tm)
    b_r = _proj_plain(h, w_all, col_b, 3 * B_WIDTH, tmw, B_WIDTH, first_tile_scale=qk_scale)

    ki_real = small_r[:, SM_KI:SM_KI + IDX_DIM].astype(BF16)
    ki_meta = _pad_rows(small_m[:, SM_KI:SM_KI + IDX_DIM], META_PAD).astype(BF16)
    wt = small_r[:, SM_WI:SM_WI + IDX_HEADS].T
    cq = small_r[:, SM_FB:SM_FB + B_HEADS]
    ct_meta = _pad_rows(small_m[:, SM_FB:SM_FB + B_HEADS], META_PAD).T

    bias_real, bias_meta = _select_bias(qi_hm, ki_real, ki_meta, wt, topk, _pick(t, 256))
    ya = _attention((qa, 0, ka_r, 0, va_r, 0),
                    (_pad_rows(ka_m, META_PAD), 0, _pad_rows(va_m, META_PAD), 0),
                    (bias_real, bias_meta), fox=False, t_blk=_pick(t, 512))
    b_m_pad = _pad_rows(b_m, META_PAD)
    yb = _attention((b_r, 0, b_r, 1, b_r, 2), (b_m_pad, 1, b_m_pad, 2),
                    (cq, cq.T, ct_meta), fox=True, t_blk=_pick(t, 512))

    w_r = jnp.concatenate([w_group, w_expert, jnp.zeros((d, LANES - N_GROUPS - N_EXPERTS), F32)], axis=1)
    wr_hi = w_r.astype(BF16)
    wr_split = jnp.concatenate([wr_hi, (w_r - wr_hi.astype(F32)).astype(BF16)], axis=1)
    b_r_pad = jnp.concatenate([b_group, b_expert, jnp.zeros((LANES - N_GROUPS - N_EXPERTS,), F32)]).reshape(1, LANES)
    h1, xr, plan = _merge(h, ya, yb, w_all, col_ga, col_gb, w_branch_a.astype(BF16), w_branch_b.astype(BF16),
                          w_out.astype(BF16), x, g_ffn, wr_split, b_r_pad, tm, 256)

    ts = _pick(t, 512)
    pos, tile_group, tile_valid, n_slots = _group_plan(plan, ts)
    xs = _dispatch(pos, xr, n_slots, _pick(t, 256))
    ys = _group_ffn(tile_group, tile_valid, xs, w_gate_e, w_up_e, w_down_e.astype(BF16), ts)
    return _combine(pos, ys, h1, g_final, _pick(t, 256))


def kernel(x, meta_tokens, g_mix, w_in, g_kv, w_kv_up, g_idx_k, b_f, w_branch_a, w_branch_b, w_out, g_ffn,
           w_group, b_group, w_expert, b_expert, w_gate_e, w_up_e, w_down_e, g_final):
    assert x.shape[0] == 1 and g_mix.shape[0] == 1, "single batch, single layer"
    out = _layer(x[0], meta_tokens.astype(x.dtype), g_mix[0], w_in[0], g_kv[0], w_kv_up[0], g_idx_k[0], b_f[0],
                 w_branch_a[0], w_branch_b[0], w_out[0], g_ffn[0], w_group[0], b_group[0], w_expert[0],
                 b_expert[0], w_gate_e[0], w_up_e[0], w_down_e[0], g_final)
    return out[None]
```

```python
import functools

import jax
import jax.numpy as jnp
from jax import lax
from jax.experimental import pallas as pl
from jax.experimental.pallas import tpu as pltpu

F32 = jnp.float32
BF16 = jnp.bfloat16

D_MODEL = 2048
N_META = 16
HEAD_DIM = 128
ROPE_THETA = 10000.0
EPS = 1e-6
NEG = -1e30
A_HEADS = 8
A_WIDTH = A_HEADS * HEAD_DIM
A_KV_RANK = 512
IDX_HEADS = 16
IDX_DIM = 64
TOPK_MAX = 256
B_HEADS = 8
B_WIDTH = B_HEADS * HEAD_DIM
N_GROUPS = 4
EXPERTS_PER_GROUP = 4
N_EXPERTS = N_GROUPS * EXPERTS_PER_GROUP
D_FF_EXPERT = 512

LANES = 128
META_PAD = LANES
VMEM_LIMIT = 56 * 1024 * 1024
LOG2E = 1.4426950408889634
NT_DIMS = (((1,), (1,)), ((), ()))

SM_KI = 0
SM_WI = IDX_DIM
SM_FB = IDX_DIM + IDX_HEADS
RT_E0 = N_GROUPS


def _cparams(sem):
    return pltpu.CompilerParams(dimension_semantics=sem, vmem_limit_bytes=VMEM_LIMIT)


def _rms_kernel(x_ref, g_ref, o_ref):
    x = x_ref[...]
    ms = jnp.mean(x * x, axis=-1, keepdims=True)
    o_ref[...] = (x * lax.rsqrt(ms + EPS) * g_ref[...]).astype(o_ref.dtype)


def _rmsnorm(x, g, tm):
    n, d = x.shape
    return pl.pallas_call(
        _rms_kernel,
        grid=(n // tm,),
        in_specs=[pl.BlockSpec((tm, d), lambda i: (i, 0)), pl.BlockSpec((1, d), lambda i: (0, 0))],
        out_specs=pl.BlockSpec((tm, d), lambda i: (i, 0)),
        out_shape=jax.ShapeDtypeStruct((n, d), BF16),
        compiler_params=_cparams(("parallel",)),
        name="rmsnorm",
    )(x, g.reshape(1, d))


def _rope_tile(y, cos, sin_signed, half):
    if 2 * half == LANES:
        partner = pltpu.roll(y, half, axis=1)
    else:
        lane = lax.broadcasted_iota(jnp.int32, y.shape, 1)
        first = (lane % (2 * half)) < half
        partner = jnp.where(first, pltpu.roll(y, LANES - half, axis=1), pltpu.roll(y, half, axis=1))
    return y * cos + partner * sin_signed


def _rope_tables(pos, half):
    inv_freq = ROPE_THETA ** (-jnp.arange(half, dtype=F32) / half)
    ang = pos.astype(F32)[:, None] * inv_freq[None, :]
    cos, sin = jnp.cos(ang), jnp.sin(ang)
    reps = LANES // (2 * half)
    cos_t = jnp.tile(jnp.concatenate([cos, cos], axis=1), (1, reps))
    sin_t = jnp.tile(jnp.concatenate([-sin, sin], axis=1), (1, reps))
    return cos_t, sin_t


PACK_ROWS = 256


def _pack_kernel(src_ref, x_ref, o_ref):
    del src_ref
    o_ref[...] = x_ref[...].T.astype(o_ref.dtype)


def _pack_w_in(wt, sections):
    _, d = wt.shape
    starts = []
    for lo, hi in sections:
        assert lo % 8 == 0 and (hi - lo) % PACK_ROWS == 0
        starts += list(range(lo, hi, PACK_ROWS))
    src = jnp.asarray([s // 8 for s in starts], jnp.int32)
    return pl.pallas_call(
        _pack_kernel,
        grid_spec=pltpu.PrefetchScalarGridSpec(
            num_scalar_prefetch=1,
            grid=(len(starts),),
            in_specs=[pl.BlockSpec((pl.Element(PACK_ROWS), pl.Element(d)), lambda j, src: (src[j] * 8, 0))],
            out_specs=pl.BlockSpec((d, PACK_ROWS), lambda j, src: (0, j))),
        out_shape=jax.ShapeDtypeStruct((d, len(starts) * PACK_ROWS), BF16),
        compiler_params=_cparams(("arbitrary",)),
        name="pack_w_in",
    )(src, wt)


def _pack_small_kernel(a_ref, b_ref, o_ref, *, n_a, n_b):
    lane = lax.broadcasted_iota(jnp.int32, o_ref.shape, 1)
    o_ref[...] = jnp.where(lane < n_a, a_ref[...].T,
                           jnp.where(lane < n_a + n_b, b_ref[...].T, 0.0)).astype(o_ref.dtype)


def _pack_small(wt, rows_a, rows_b):
    _, d = wt.shape
    n_a, n_b = rows_a[1] - rows_a[0], rows_b[1] - rows_b[0]
    start_a, start_b = rows_a[0], rows_b[0] - n_a
    assert start_a % 8 == 0 and start_b % 8 == 0 and start_b >= 0 and n_a + n_b <= LANES
    window = lambda start: pl.BlockSpec((pl.Element(LANES), pl.Element(d)), lambda i: (start, 0))
    return pl.pallas_call(
        functools.partial(_pack_small_kernel, n_a=n_a, n_b=n_b),
        grid=(1,),
        in_specs=[window(start_a), window(start_b)],
        out_specs=pl.BlockSpec((d, LANES), lambda i: (0, 0)),
        out_shape=jax.ShapeDtypeStruct((d, LANES), BF16),
        compiler_params=_cparams(("arbitrary",)),
        name="pack_w_small",
    )(wt, wt)


def _proj_plain_kernel(h_ref, w_ref, o_ref, *, first_tile_scale):
    y = jnp.dot(h_ref[...], w_ref[...], preferred_element_type=F32)
    scale = jnp.where(pl.program_id(1) == 0, jnp.float32(first_tile_scale), jnp.float32(1.0))
    o_ref[...] = (y * scale).astype(o_ref.dtype)


def _proj_plain(h, w, col0, nc, tm, tn, first_tile_scale=1.0):
    n, d = h.shape
    c0 = col0 // tn
    return pl.pallas_call(
        functools.partial(_proj_plain_kernel, first_tile_scale=first_tile_scale),
        grid=(n // tm, nc // tn),
        in_specs=[pl.BlockSpec((tm, d), lambda i, j: (i, 0)), pl.BlockSpec((d, tn), lambda i, j: (0, c0 + j))],
        out_specs=pl.BlockSpec((tm, tn), lambda i, j: (i, j)),
        out_shape=jax.ShapeDtypeStruct((n, nc), BF16),
        compiler_params=_cparams(("parallel", "parallel")),
        name="proj_plain",
    )(h, w)


def _proj_rope_kernel(h_ref, w_ref, cos_ref, sin_ref, o_ref, *, half, head_major, out_scale):
    y = jnp.dot(h_ref[...], w_ref[...], preferred_element_type=F32)
    cos, sin = cos_ref[...], sin_ref[...]
    per_tile = LANES // (2 * half)
    for c in range(y.shape[1] // LANES):
        sl = slice(c * LANES, (c + 1) * LANES)
        roped = (_rope_tile(y[:, sl], cos, sin, half) * out_scale).astype(o_ref.dtype)
        if head_major:
            for k in range(per_tile):
                o_ref[c * per_tile + k] = roped[:, k * 2 * half:(k + 1) * 2 * half]
        else:
            o_ref[:, sl] = roped


def _proj_rope(h, w, col0, nc, cos, sin, half, tm, tn, head_major=False, out_scale=1.0):
    n, d = h.shape
    c0 = col0 // tn
    if head_major:
        assert tn == nc
        heads = nc // (2 * half)
        out_spec = pl.BlockSpec((heads, tm, 2 * half), lambda i, j: (0, i, 0))
        out_shape = jax.ShapeDtypeStruct((heads, n, 2 * half), BF16)
    else:
        out_spec = pl.BlockSpec((tm, tn), lambda i, j: (i, j))
        out_shape = jax.ShapeDtypeStruct((n, nc), BF16)
    return pl.pallas_call(
        functools.partial(_proj_rope_kernel, half=half, head_major=head_major, out_scale=out_scale),
        grid=(n // tm, nc // tn),
        in_specs=[pl.BlockSpec((tm, d), lambda i, j: (i, 0)), pl.BlockSpec((d, tn), lambda i, j: (0, c0 + j)),
                  pl.BlockSpec((tm, LANES), lambda i, j: (i, 0)), pl.BlockSpec((tm, LANES), lambda i, j: (i, 0))],
        out_specs=out_spec,
        out_shape=out_shape,
        compiler_params=_cparams(("parallel", "parallel")),
        name="proj_rope",
    )(h, w, cos, sin)


def _kv_kernel(h_ref, wc_ref, g_ref, wup_ref, cos_ref, sin_ref, ka_ref, va_ref):
    c = jnp.dot(h_ref[...], wc_ref[...], preferred_element_type=F32)
    ms = jnp.mean(c * c, axis=-1, keepdims=True)
    cn = (c * lax.rsqrt(ms + EPS) * g_ref[...]).astype(BF16)
    kv = jnp.dot(cn, wup_ref[...], preferred_element_type=F32)
    cos, sin = cos_ref[...], sin_ref[...]
    for hd in range(A_HEADS):
        sl = slice(hd * HEAD_DIM, (hd + 1) * HEAD_DIM)
        ka_ref[:, sl] = _rope_tile(kv[:, sl], cos, sin, HEAD_DIM // 2).astype(ka_ref.dtype)
    va_ref[...] = kv[:, A_WIDTH:].astype(va_ref.dtype)


def _proj_kv(h, wc, col0, g_kv, wup, cos, sin, tm):
    n, d = h.shape
    c0 = col0 // A_KV_RANK
    return pl.pallas_call(
        _kv_kernel,
        grid=(n // tm,),
        in_specs=[pl.BlockSpec((tm, d), lambda i: (i, 0)),
                  pl.BlockSpec((d, A_KV_RANK), lambda i: (0, c0)),
                  pl.BlockSpec((1, A_KV_RANK), lambda i: (0, 0)),
                  pl.BlockSpec((A_KV_RANK, 2 * A_WIDTH), lambda i: (0, 0)),
                  pl.BlockSpec((tm, LANES), lambda i: (i, 0)),
                  pl.BlockSpec((tm, LANES), lambda i: (i, 0))],
        out_specs=[pl.BlockSpec((tm, A_WIDTH), lambda i: (i, 0)), pl.BlockSpec((tm, A_WIDTH), lambda i: (i, 0))],
        out_shape=[jax.ShapeDtypeStruct((n, A_WIDTH), BF16), jax.ShapeDtypeStruct((n, A_WIDTH), BF16)],
        compiler_params=_cparams(("parallel",)),
        name="proj_kv",
    )(h, wc, g_kv.reshape(1, A_KV_RANK), wup, cos, sin)


def _small_kernel(h_ref, w_ref, g_ref, bf_ref, c0_ref, cos_ref, sin_ref, o_ref, carry_ref):
    tm = h_ref.shape[0]

    @pl.when(pl.program_id(0) == 0)
    def _():
        carry_ref[...] = c0_ref[...]

    y = jnp.dot(h_ref[...], w_ref[...], preferred_element_type=F32)
    lane = lax.broadcasted_iota(jnp.int32, y.shape, 1)
    is_ki = lane < SM_WI
    ms = jnp.sum(jnp.where(is_ki, y * y, 0.0), axis=-1, keepdims=True) * (1.0 / IDX_DIM)
    ki = _rope_tile(y * lax.rsqrt(ms + EPS) * g_ref[...], cos_ref[...], sin_ref[...], IDX_DIM // 2)
    wi = y * (IDX_HEADS ** -0.5 * IDX_DIM ** -0.5)
    z = y + bf_ref[...]
    logf = -(jnp.maximum(-z, 0.0) + jnp.log(1.0 + jnp.exp(-jnp.abs(z))))
    hi = logf.astype(BF16)
    r1 = logf - hi.astype(F32)
    mid = r1.astype(BF16)
    lo = (r1 - mid.astype(F32)).astype(BF16)
    row = lax.broadcasted_iota(jnp.int32, (tm, tm), 0)
    col = lax.broadcasted_iota(jnp.int32, (tm, tm), 1)
    tri = (col <= row).astype(BF16)
    csum = (jnp.dot(tri, hi, preferred_element_type=F32) + jnp.dot(tri, mid, preferred_element_type=F32)
            + jnp.dot(tri, lo, preferred_element_type=F32)) + carry_ref[0:1, :]
    carry_ref[...] = jnp.broadcast_to(csum[tm - 1:tm, :], carry_ref.shape)
    o_ref[...] = jnp.where(is_ki, ki, jnp.where(lane < SM_FB, wi, csum))


def _proj_small(h, w, col0, g_idx_pad, bf_pad, c0, cos, sin, tm):
    n, d = h.shape
    cb = col0 // LANES
    return pl.pallas_call(
        _small_kernel,
        grid=(n // tm,),
        in_specs=[pl.BlockSpec((tm, d), lambda i: (i, 0)),
                  pl.BlockSpec((d, LANES), lambda i: (0, cb)),
                  pl.BlockSpec((1, LANES), lambda i: (0, 0)),
                  pl.BlockSpec((1, LANES), lambda i: (0, 0)),
                  pl.BlockSpec((8, LANES), lambda i: (0, 0)),
                  pl.BlockSpec((tm, LANES), lambda i: (i, 0)),
                  pl.BlockSpec((tm, LANES), lambda i: (i, 0))],
        out_specs=pl.BlockSpec((tm, LANES), lambda i: (i, 0)),
        out_shape=jax.ShapeDtypeStruct((n, LANES), F32),
        scratch_shapes=[pltpu.VMEM((8, LANES), F32)],
        compiler_params=_cparams(("arbitrary",)),
        name="proj_small",
    )(h, w, g_idx_pad, bf_pad, c0, cos, sin)


COUNT_ROWS = 64


def _key_to_float(key):
    bits = key ^ ((key >> 31) & jnp.int32(0x7FFFFFFF))
    return lax.bitcast_convert_type(bits, F32)


def _select_kernel(qi_ref, kir_ref, kim_ref, wt_ref, or_ref, om_ref, s_ref, *, tq, topk):
    i = pl.program_id(0)
    n_real = or_ref.shape[1]
    n_chunks = n_real // tq
    krow = lax.broadcasted_iota(jnp.int32, (tq, tq), 0)
    qcol = lax.broadcasted_iota(jnp.int32, (tq, tq), 1)
    mrow = lax.broadcasted_iota(jnp.int32, (META_PAD, tq), 0)
    wt = wt_ref[...]

    def scores(kc):
        s = jnp.zeros((kc.shape[0], tq), F32)
        for h in range(IDX_HEADS):
            d = lax.dot_general(kc, qi_ref[h], NT_DIMS, preferred_element_type=F32)
            s = s + wt[h:h + 1, :] * jnp.maximum(d, 0.0)
        return s

    def causal(c):
        return krow + (c - i) * tq <= qcol

    def score_chunk(c, _):
        off = pl.multiple_of(c * tq, tq)
        s_ref[pl.ds(off, tq), :] = jnp.where(causal(c), scores(kir_ref[pl.ds(off, tq), :]), NEG)
        return 0

    lax.fori_loop(0, i + 1, score_chunk, 0)
    s_ref[n_real:, :] = jnp.where(mrow < N_META, scores(kim_ref[...]), NEG)

    def count(cmp, thr):
        def body(c, acc):
            off = c * tq
            for r in range(tq // COUNT_ROWS):
                blk = s_ref[pl.ds(pl.multiple_of(off + r * COUNT_ROWS, COUNT_ROWS), COUNT_ROWS), :]
                acc = acc + jnp.where(cmp(blk, thr), 1.0, 0.0)
            return acc

        acc = lax.fori_loop(0, i + 1, body, jnp.zeros((COUNT_ROWS, tq), F32))
        for r in range(META_PAD // COUNT_ROWS):
            blk = s_ref[n_real + r * COUNT_ROWS:n_real + (r + 1) * COUNT_ROWS, :]
            acc = acc + jnp.where(cmp(blk, thr), 1.0, 0.0)
        return jnp.sum(acc, axis=0, keepdims=True)

    kf = jnp.float32(topk)

    def bit_step(b, carry):
        t_key, cnt_t = carry
        cand = t_key + lax.shift_left(jnp.int32(1), 31 - b)
        cnt = count(lambda s, t: s >= t, _key_to_float(cand))
        take = cnt >= kf
        return jnp.where(take, cand, t_key), jnp.where(take, cnt, cnt_t)

    t0 = jnp.full((1, tq), jnp.iinfo(jnp.int32).min, jnp.int32)
    c0 = jnp.broadcast_to(((i + 1) * tq + META_PAD).astype(F32), (1, tq))
    t_key, cnt_t = lax.fori_loop(0, 32, bit_step, (t0, c0))
    thr = _key_to_float(t_key)

    def fill_chunk(c, _):
        or_ref[:, pl.ds(pl.multiple_of(c * tq, tq), tq)] = jnp.full((tq, tq), NEG, or_ref.dtype)
        return 0

    lax.fori_loop(i + 1, n_chunks, fill_chunk, 0)

    ambiguous = jnp.logical_and(cnt_t > kf, thr > NEG)
    any_amb = jnp.max(jnp.where(ambiguous, 1.0, 0.0)) > 0.0

    @pl.when(jnp.logical_not(any_amb))
    def _():
        def out_chunk(c, _):
            off = pl.multiple_of(c * tq, tq)
            sel = jnp.where(s_ref[pl.ds(off, tq), :] >= thr, 0.0, NEG)
            or_ref[:, pl.ds(off, tq)] = jnp.where(causal(c), sel, NEG).T.astype(or_ref.dtype)
            return 0

        lax.fori_loop(0, i + 1, out_chunk, 0)
        sel = jnp.where(s_ref[n_real:, :] >= thr, 0.0, NEG)
        om_ref[...] = jnp.where(mrow < N_META, sel, NEG).T.astype(om_ref.dtype)

    @pl.when(any_amb)
    def _():
        cnt_gt = count(lambda s, t: s > t, thr)
        quota = jnp.where(ambiguous, kf - cnt_gt, jnp.float32(3e38))

        def tie_select(s, valid, running):
            rows = s.shape[0]
            r = lax.broadcasted_iota(jnp.int32, (rows, rows), 0)
            c_ = lax.broadcasted_iota(jnp.int32, (rows, rows), 1)
            lower = (c_ <= r).astype(BF16)
            eq = jnp.where(s == thr, 1.0, 0.0)
            rank = jnp.dot(lower, eq.astype(BF16), preferred_element_type=F32) + running
            keep = jnp.where(s > thr, 0.0, jnp.where(jnp.logical_and(eq > 0.0, rank <= quota), 0.0, NEG))
            return jnp.where(valid, keep, NEG), running + jnp.sum(eq, axis=0, keepdims=True)

        bias_m, running = tie_select(s_ref[n_real:, :], mrow < N_META, jnp.zeros((1, tq), F32))
        om_ref[...] = bias_m.T.astype(om_ref.dtype)

        def out_chunk(c, running):
            off = pl.multiple_of(c * tq, tq)
            bias, running = tie_select(s_ref[pl.ds(off, tq), :], causal(c), running)
            or_ref[:, pl.ds(off, tq)] = bias.T.astype(or_ref.dtype)
            return running

        lax.fori_loop(0, i + 1, out_chunk, running)


def _select_bias(qi_hm, ki_real, ki_meta, wt, topk, tq):
    _, t, _ = qi_hm.shape
    return pl.pallas_call(
        functools.partial(_select_kernel, tq=tq, topk=topk),
        grid=(t // tq,),
        in_specs=[pl.BlockSpec((IDX_HEADS, tq, IDX_DIM), lambda i: (0, i, 0)),
                  pl.BlockSpec((t, IDX_DIM), lambda i: (0, 0)),
                  pl.BlockSpec((META_PAD, IDX_DIM), lambda i: (0, 0)),
                  pl.BlockSpec((IDX_HEADS, tq), lambda i: (0, i))],
        out_specs=[pl.BlockSpec((tq, t), lambda i: (i, 0)), pl.BlockSpec((tq, META_PAD), lambda i: (i, 0))],
        out_shape=[jax.ShapeDtypeStruct((t, t), BF16), jax.ShapeDtypeStruct((t, META_PAD), BF16)],
        scratch_shapes=[pltpu.VMEM((t + META_PAD, tq), F32)],
        compiler_params=_cparams(("parallel",)),
        name="select_bias",
    )(qi_hm, ki_real, ki_meta, wt)


def _attn_kernel(*refs, tq, tk, fox):
    if fox:
        (qi_ref, kj_ref, q_ref, k_ref, v_ref, km_ref, vm_ref, cq_ref, ck_ref, ckm_ref,
         o_ref, m_ref, l_ref, acc_ref, r_ref) = refs
        s_ref = None
    else:
        (qi_ref, kj_ref, q_ref, k_ref, v_ref, km_ref, vm_ref, b_ref, bm_ref,
         o_ref, m_ref, l_ref, acc_ref, s_ref) = refs
    stage_logits = s_ref is not None
    step = pl.program_id(0)
    i = qi_ref[step]
    j = kj_ref[step]
    ratio = tq // tk
    nheads = q_ref.shape[1] // HEAD_DIM

    def block(k_blk_ref, v_blk_ref, bias_fn):
        width = k_blk_ref.shape[0]

        def logits(h):
            sl = slice(h * HEAD_DIM, (h + 1) * HEAD_DIM)
            return lax.dot_general(q_ref[:, sl], k_blk_ref[:, sl], NT_DIMS, preferred_element_type=F32)

        if stage_logits:
            for h in range(nheads):
                s_ref[h, :, :width] = logits(h)
        s_next = None if stage_logits else logits(0)
        for h in range(nheads):
            sl = slice(h * HEAD_DIM, (h + 1) * HEAD_DIM)
            if stage_logits:
                s = s_ref[h, :, :width]
            else:
                s, s_next = s_next, (logits(h + 1) if h + 1 < nheads else None)
            u = bias_fn(s, h)
            m_prev = m_ref[h]
            mx = jnp.max(u, axis=1, keepdims=True)
            if fox:
                r = r_ref[h]
                m_next = jnp.maximum(m_prev, mx + r)
                sub = m_next - r
            else:
                m_next = jnp.maximum(m_prev, mx)
                sub = m_next
            alpha = jnp.exp2(m_prev - m_next)
            ps = [jnp.exp2(u[:, c * LANES:(c + 1) * LANES] - sub) for c in range(width // LANES)]
            psum = ps[0]
            for pc in ps[1:]:
                psum = psum + pc
            p = ps[0] if len(ps) == 1 else jnp.concatenate(ps, axis=1)
            l_ref[h] = alpha * l_ref[h] + psum
            m_ref[h] = m_next
            acc_ref[h] = alpha * acc_ref[h] + jnp.dot(p.astype(BF16), v_blk_ref[:, sl],
                                                      preferred_element_type=F32)

    @pl.when(j == 0)
    def _():
        m_ref[...] = jnp.full(m_ref.shape, NEG, F32)
        l_ref[...] = jnp.zeros(l_ref.shape, F32)
        acc_ref[...] = jnp.zeros(acc_ref.shape, F32)
        mcol = lax.broadcasted_iota(jnp.int32, (tq, META_PAD), 1)
        if fox:
            for h in range(nheads):
                r_ref[h] = jnp.broadcast_to(cq_ref[:, h:h + 1] * LOG2E, (tq, LANES))

            def bias_fn(u, h):
                return jnp.where(mcol < N_META, u - ckm_ref[h:h + 1, :] * LOG2E, NEG)
        else:
            bias_m = bm_ref[...].astype(F32)

            def bias_fn(u, h):
                return u + bias_m
        block(km_ref, vm_ref, bias_fn)

    if fox:
        @pl.when(j < ratio * i)
        def _():
            block(k_ref, v_ref, lambda u, h: u - ck_ref[h:h + 1, :] * LOG2E)

        @pl.when(j >= ratio * i)
        def _():
            row = lax.broadcasted_iota(jnp.int32, (tq, tk), 0)
            col = lax.broadcasted_iota(jnp.int32, (tq, tk), 1) + (j * tk - i * tq)
            block(k_ref, v_ref, lambda u, h: jnp.where(col <= row, u - ck_ref[h:h + 1, :] * LOG2E, NEG))
    else:
        bias = b_ref[...].astype(F32)
        block(k_ref, v_ref, lambda u, h: u + bias)

    @pl.when(j == ratio * (i + 1) - 1)
    def _():
        for h in range(nheads):
            sl = slice(h * HEAD_DIM, (h + 1) * HEAD_DIM)
            o_ref[:, sl] = (acc_ref[h] / jnp.sum(l_ref[h], axis=1, keepdims=True)).astype(o_ref.dtype)


def _attention(qkv, meta_kv, extra, *, fox, tq, tk):
    q_arr, q_col, k_arr, k_col, v_arr, v_col = qkv
    km_arr, km_col, vm_arr, vm_col = meta_kv
    t = q_arr.shape[0]
    w = A_WIDTH
    nheads = w // HEAD_DIM
    assert tq % tk == 0 and t % tq == 0
    nq, ratio = t // tq, tq // tk
    qi = jnp.asarray([i for i in range(nq) for _ in range(ratio * (i + 1))], jnp.int32)
    kj = jnp.asarray([j for i in range(nq) for j in range(ratio * (i + 1))], jnp.int32)

    in_specs = [pl.BlockSpec((tq, w), lambda s, qi, kj: (qi[s], q_col)),
                pl.BlockSpec((tk, w), lambda s, qi, kj: (kj[s], k_col)),
                pl.BlockSpec((tk, w), lambda s, qi, kj: (kj[s], v_col)),
                pl.BlockSpec((META_PAD, w), lambda s, qi, kj: (0, km_col)),
                pl.BlockSpec((META_PAD, w), lambda s, qi, kj: (0, vm_col))]
    scratch = [pltpu.VMEM((nheads, tq, LANES), F32), pltpu.VMEM((nheads, tq, LANES), F32),
               pltpu.VMEM((nheads, tq, HEAD_DIM), F32)]
    args = (q_arr, k_arr, v_arr, km_arr, vm_arr)
    if fox:
        cq, ct_real, ct_meta = extra
        in_specs += [pl.BlockSpec((tq, nheads), lambda s, qi, kj: (qi[s], 0)),
                     pl.BlockSpec((nheads, tk), lambda s, qi, kj: (0, kj[s])),
                     pl.BlockSpec((nheads, META_PAD), lambda s, qi, kj: (0, 0))]
        args += (cq, ct_real, ct_meta)
        scratch.append(pltpu.VMEM((nheads, tq, LANES), F32))
    else:
        bias_real, bias_meta = extra
        in_specs += [pl.BlockSpec((tq, tk), lambda s, qi, kj: (qi[s], kj[s])),
                     pl.BlockSpec((tq, META_PAD), lambda s, qi, kj: (qi[s], 0))]
        args += (bias_real, bias_meta)
        scratch.append(pltpu.VMEM((nheads, tq, max(tk, META_PAD)), F32))
    return pl.pallas_call(
        functools.partial(_attn_kernel, tq=tq, tk=tk, fox=fox),
        grid_spec=pltpu.PrefetchScalarGridSpec(
            num_scalar_prefetch=2,
            grid=(int(qi.shape[0]),),
            in_specs=in_specs,
            out_specs=pl.BlockSpec((tq, w), lambda s, qi, kj: (qi[s], 0)),
            scratch_shapes=scratch),
        out_shape=jax.ShapeDtypeStruct((t, w), BF16),
        compiler_params=_cparams(("arbitrary",)),
        name="attn_fox" if fox else "attn_dsa",
    )(qi, kj, *args)


def _route(logits):
    lane = lax.broadcasted_iota(jnp.int32, logits.shape, 1).astype(F32)
    big = jnp.float32(LANES)
    ninf = -jnp.inf
    gl = jnp.where(lane < N_GROUPS, logits, ninf)
    gmax = jnp.max(gl, axis=1, keepdims=True)
    g_sel = jnp.min(jnp.where(gl == gmax, lane, big), axis=1, keepdims=True)
    p_group = 1.0 / jnp.sum(jnp.exp(gl - gmax), axis=1, keepdims=True)
    e_lo = RT_E0 + g_sel * EXPERTS_PER_GROUP
    in_grp = jnp.logical_and(lane >= e_lo, lane < e_lo + EXPERTS_PER_GROUP)
    el = jnp.where(in_grp, logits, ninf)
    emax = jnp.max(el, axis=1, keepdims=True)
    ex = jnp.exp(el - emax)
    prob = ex / jnp.sum(ex, axis=1, keepdims=True)
    prob = jnp.where(in_grp, prob, -1.0)
    p1 = jnp.max(prob, axis=1, keepdims=True)
    i1 = jnp.min(jnp.where(prob == p1, lane, big), axis=1, keepdims=True)
    rest = jnp.where(lane == i1, -1.0, prob)
    p2 = jnp.max(rest, axis=1, keepdims=True)
    i2 = jnp.min(jnp.where(rest == p2, lane, big), axis=1, keepdims=True)
    tot = p1 + p2
    w1 = p_group * (p1 / tot)
    w2 = p_group * (p2 / tot)
    return jnp.where(lane == i1, w1, jnp.where(lane == i2, w2, 0.0)), g_sel


def _merge_kernel(h_ref, ya_ref, yb_ref, wga_ref, wgb_ref, wa_ref, wb_ref, wo_ref, x_ref, g_ref,
                  wr_ref, br_ref, h1_ref, xr_ref, plan_ref, acc_ref, cnt_ref):
    n = pl.program_id(1)

    @pl.when(jnp.logical_and(pl.program_id(0) == 0, n == 0))
    def _():
        cnt_ref[...] = jnp.zeros(cnt_ref.shape, F32)

    @pl.when(n == 0)
    def _():
        acc_ref[...] = jnp.zeros(acc_ref.shape, F32)

    h = h_ref[...]
    ga = jnp.dot(h, wga_ref[...], preferred_element_type=F32)
    gb = jnp.dot(h, wgb_ref[...], preferred_element_type=F32)
    pa = jnp.dot(ya_ref[...], wa_ref[...], preferred_element_type=F32)
    pb = jnp.dot(yb_ref[...], wb_ref[...], preferred_element_type=F32)
    merged = jax.nn.sigmoid(ga) * pa + jax.nn.sigmoid(gb) * pb
    acc_ref[...] += jnp.dot(merged.astype(BF16), wo_ref[...], preferred_element_type=F32)

    @pl.when(n == pl.num_programs(1) - 1)
    def _():
        h1 = x_ref[...] + acc_ref[...]
        h1_ref[...] = h1
        ms = jnp.mean(h1 * h1, axis=-1, keepdims=True)
        t = h1 * lax.rsqrt(ms + EPS) * g_ref[...]
        t_hi = t.astype(BF16)
        t_lo = (t - t_hi.astype(F32)).astype(BF16)
        wr = wr_ref[...]
        y = jnp.dot(t_hi, wr, preferred_element_type=F32)
        logits = (y[:, :LANES] + y[:, LANES:]
                  + jnp.dot(t_lo, wr[:, :LANES], preferred_element_type=F32)) + br_ref[...]
        gates, g_sel = _route(logits)
        d = t.shape[1]
        xr_ref[:, :d] = t
        xr_ref[:, d:] = gates
        tm = t.shape[0]
        lane = lax.broadcasted_iota(jnp.int32, gates.shape, 1).astype(F32)
        onehot = jnp.where(lane == g_sel, 1.0, 0.0).astype(BF16)
        rr = lax.broadcasted_iota(jnp.int32, (tm, tm), 0)
        cc = lax.broadcasted_iota(jnp.int32, (tm, tm), 1)
        cum = jnp.dot((cc <= rr).astype(BF16), onehot, preferred_element_type=F32) + cnt_ref[0:1, :]
        cnt_ref[...] = jnp.broadcast_to(cum[tm - 1:tm, :], cnt_ref.shape)
        plan_ref[...] = jnp.where(lane < N_GROUPS, cum, jnp.where(lane == N_GROUPS, g_sel, 0.0))


def _merge(h, ya, yb, w_all, ga_col0, gb_col0, wa, wb, wo, x, g_ffn, wr_split, br, tm, tn):
    t, d = h.shape
    row = lambda i, n: (i, 0)
    ga0, gb0 = ga_col0 // tn, gb_col0 // tn
    wga = wgb = w_all
    return pl.pallas_call(
        _merge_kernel,
        grid=(t // tm, d // tn),
        in_specs=[pl.BlockSpec((tm, d), row),
                  pl.BlockSpec((tm, A_WIDTH), row),
                  pl.BlockSpec((tm, B_WIDTH), row),
                  pl.BlockSpec((d, tn), lambda i, n: (0, ga0 + n)),
                  pl.BlockSpec((d, tn), lambda i, n: (0, gb0 + n)),
                  pl.BlockSpec((A_WIDTH, tn), lambda i, n: (0, n)),
                  pl.BlockSpec((B_WIDTH, tn), lambda i, n: (0, n)),
                  pl.BlockSpec((tn, d), lambda i, n: (n, 0)),
                  pl.BlockSpec((tm, d), row),
                  pl.BlockSpec((1, d), lambda i, n: (0, 0)),
                  pl.BlockSpec((d, 2 * LANES), lambda i, n: (0, 0)),
                  pl.BlockSpec((1, LANES), lambda i, n: (0, 0))],
        out_specs=[pl.BlockSpec((tm, d), row), pl.BlockSpec((tm, d + LANES), row),
                   pl.BlockSpec((tm, LANES), row)],
        out_shape=[jax.ShapeDtypeStruct((t, d), F32), jax.ShapeDtypeStruct((t, d + LANES), F32),
                   jax.ShapeDtypeStruct((t, LANES), F32)],
        scratch_shapes=[pltpu.VMEM((tm, d), F32), pltpu.VMEM((8, LANES), F32)],
        compiler_params=_cparams(("arbitrary", "arbitrary")),
        name="merge_out_router",
    )(h, ya, yb, wga, wgb, wa, wb, wo, x, g_ffn.reshape(1, d), wr_split, br)


def _row_copy(src_ref, src_row, dst_ref, dst_row, sem):
    return pltpu.make_async_copy(src_ref.at[pl.ds(src_row, 1), :], dst_ref.at[pl.ds(dst_row, 1), :], sem)


def _group_ffn_kernel(tg_ref, tv_ref, src_ref, xr_ref, wg_ref, wu_ref, wd_ref, o_ref,
                      acc_ref, x_ref, xb_ref, g_ref, sem):
    j = pl.program_id(0)
    k = pl.program_id(1)
    ts, d = o_ref.shape
    last_tile = pl.num_programs(0) - 1

    def for_tile_rows(tile, fn):
        def body(r, _):
            fn(_row_copy(xr_ref, src_ref[tile * ts + r], x_ref, r, sem))
            return 0

        lax.fori_loop(0, ts, body, 0)

    @pl.when(k == 0)
    def _():
        acc_ref[...] = jnp.zeros(acc_ref.shape, F32)

    @pl.when(jnp.logical_and(k == 0, j == 0))
    def _():
        for_tile_rows(j, lambda c: c.start())

    @pl.when(jnp.logical_and(k == 0, tv_ref[j] > 0))
    def _():
        for_tile_rows(j, lambda c: c.wait())
        xb_ref[...] = x_ref[:, :d].astype(BF16)
        g_ref[...] = x_ref[:, d:]

    nxt = jnp.minimum(j + 1, last_tile)

    @pl.when(jnp.logical_and(k == 1, jnp.logical_and(j < last_tile, tv_ref[nxt] > 0)))
    def _():
        for_tile_rows(j + 1, lambda c: c.start())

    @pl.when(tv_ref[j] > 0)
    def _():
        gates = g_ref[...]
        lane = lax.broadcasted_iota(jnp.int32, gates.shape, 1)
        e_lane = RT_E0 + tg_ref[j] * EXPERTS_PER_GROUP + k
        g_e = jnp.sum(jnp.where(lane == e_lane, gates, 0.0), axis=1, keepdims=True)
        xb = xb_ref[...]
        hg = jnp.dot(xb, wg_ref[0].astype(BF16), preferred_element_type=F32)
        hu = jnp.dot(xb, wu_ref[0].astype(BF16), preferred_element_type=F32)
        act = jax.nn.silu(hg) * hu * g_e
        acc_ref[...] += jnp.dot(act.astype(BF16), wd_ref[0].astype(BF16), preferred_element_type=F32)

    @pl.when(k == pl.num_programs(1) - 1)
    def _():
        o_ref[...] = acc_ref[...]


def _group_ffn(tile_group, tile_valid, src, xr, wg, wu, wd, ts):
    n_slots = src.shape[0]
    w = xr.shape[1]
    d = w - LANES
    f = wg.shape[2]
    expert = lambda j, k, tg, tv, src: (tg[j] * EXPERTS_PER_GROUP + k, 0, 0)
    return pl.pallas_call(
        _group_ffn_kernel,
        grid_spec=pltpu.PrefetchScalarGridSpec(
            num_scalar_prefetch=3,
            grid=(n_slots // ts, EXPERTS_PER_GROUP),
            in_specs=[pl.BlockSpec(memory_space=pl.ANY),
                      pl.BlockSpec((1, d, f), expert),
                      pl.BlockSpec((1, d, f), expert),
                      pl.BlockSpec((1, f, d), expert)],
            out_specs=pl.BlockSpec((ts, d), lambda j, k, tg, tv, src: (j, 0)),
            scratch_shapes=[pltpu.VMEM((ts, d), F32), pltpu.VMEM((ts, w), F32), pltpu.VMEM((ts, d), BF16),
                            pltpu.VMEM((ts, LANES), F32), pltpu.SemaphoreType.DMA]),
        out_shape=jax.ShapeDtypeStruct((n_slots, d), F32),
        compiler_params=_cparams(("arbitrary", "arbitrary")),
        name="moe_group_ffn",
    )(tile_group, tile_valid, src, xr, wg, wu, wd)


def _combine_kernel(pos_ref, ys_ref, h1_ref, gf_ref, o_ref, ybuf_ref, sem):
    i = pl.program_id(0)
    tm = h1_ref.shape[0]

    def issue(r, _):
        _row_copy(ys_ref, pos_ref[i * tm + r], ybuf_ref, r, sem).start()
        return 0

    lax.fori_loop(0, tm, issue, 0)

    def drain(r, _):
        _row_copy(ys_ref, pos_ref[i * tm + r], ybuf_ref, r, sem).wait()
        return 0

    lax.fori_loop(0, tm, drain, 0)
    h2 = h1_ref[...] + ybuf_ref[...]
    ms = jnp.mean(h2 * h2, axis=-1, keepdims=True)
    o_ref[...] = h2 * lax.rsqrt(ms + EPS) * gf_ref[...]


def _combine(pos, ys, h1, g_final, tm):
    t, d = h1.shape
    return pl.pallas_call(
        _combine_kernel,
        grid_spec=pltpu.PrefetchScalarGridSpec(
            num_scalar_prefetch=1,
            grid=(t // tm,),
            in_specs=[pl.BlockSpec(memory_space=pl.ANY),
                      pl.BlockSpec((tm, d), lambda i, pos: (i, 0)),
                      pl.BlockSpec((1, d), lambda i, pos: (0, 0))],
            out_specs=pl.BlockSpec((tm, d), lambda i, pos: (i, 0)),
            scratch_shapes=[pltpu.VMEM((tm, d), F32), pltpu.SemaphoreType.DMA]),
        out_shape=jax.ShapeDtypeStruct((t, d), F32),
        compiler_params=_cparams(("arbitrary",)),
        name="moe_combine_final",
    )(pos, ys, h1, g_final.reshape(1, d))


def _group_plan(plan, ts):
    t = plan.shape[0]
    g_sel = plan[:, N_GROUPS].astype(jnp.int32)
    cum_sel = jnp.take_along_axis(plan[:, :N_GROUPS], g_sel[:, None], axis=1)[:, 0]
    counts = plan[t - 1, :N_GROUPS].astype(jnp.int32)
    padded = (counts + ts - 1) // ts * ts
    ends = jnp.cumsum(padded)
    starts = ends - padded
    pos = starts[g_sel] + cum_sel.astype(jnp.int32) - 1
    n_tiles = (t + N_GROUPS * (ts - 1)) // ts
    tile_start = jnp.arange(n_tiles, dtype=jnp.int32) * ts
    tile_group = jnp.minimum(jnp.sum(tile_start[:, None] >= ends[None, :], axis=1), N_GROUPS - 1)
    tile_valid = (tile_start < ends[N_GROUPS - 1]).astype(jnp.int32)
    src = jnp.zeros((n_tiles * ts,), jnp.int32).at[pos].set(jnp.arange(t, dtype=jnp.int32))
    return pos, src, tile_group.astype(jnp.int32), tile_valid


def _pick(n, pref):
    return pref if n % pref == 0 else n


def _pad_rows(a, rows):
    return jnp.pad(a, ((0, rows - a.shape[0]), (0, 0)))


def _layer(x, meta, g_mix, w_in, g_kv, w_kv_up, g_idx_k, b_f, w_branch_a, w_branch_b, w_out,
           g_ffn, w_group, b_group, w_expert, b_expert, w_gate_e, w_up_e, w_down_e, g_final):
    t, d = x.shape
    n_all = t + N_META
    topk = min(TOPK_MAX, n_all // 4)

    o = [0]
    for wdt in (A_WIDTH, A_KV_RANK, IDX_HEADS * IDX_DIM, IDX_DIM, IDX_HEADS, B_WIDTH, B_WIDTH, B_WIDTH,
                B_HEADS, D_MODEL, D_MODEL):
        o.append(o[-1] + wdt)
    w_in_t = w_in.T
    w_all = _pack_w_in(w_in_t, [(o[0], o[1]), (o[2], o[3]), (o[5], o[8]), (o[9], o[10]), (o[10], o[11]),
                                (o[1], o[2])])
    w_small = _pack_small(w_in_t, (o[3], o[5]), (o[8], o[9]))
    col_qa = 0
    col_qi = col_qa + A_WIDTH
    col_b = col_qi + IDX_HEADS * IDX_DIM
    col_ga = col_b + 3 * B_WIDTH
    col_gb = col_ga + D_MODEL
    col_ckv = col_gb + D_MODEL
    w_up_bf = w_kv_up.astype(BF16)
    g_idx_pad = jnp.zeros((1, LANES), F32).at[0, SM_KI:SM_KI + IDX_DIM].set(g_idx_k)
    bf_pad = jnp.zeros((1, LANES), F32).at[0, SM_FB:SM_FB + B_HEADS].set(b_f)

    pos_meta = jnp.arange(N_META)
    pos_real = jnp.arange(N_META, n_all)
    cos64_m, sin64_m = _rope_tables(pos_meta, HEAD_DIM // 2)
    cos64_r, sin64_r = _rope_tables(pos_real, HEAD_DIM // 2)
    cos32_m, sin32_m = _rope_tables(pos_meta, IDX_DIM // 2)
    cos32_r, sin32_r = _rope_tables(pos_real, IDX_DIM // 2)

    tm = _pick(t, 512)

    hm = _rmsnorm(meta, g_mix, N_META)
    ka_m, va_m = _proj_kv(hm, w_all, col_ckv, g_kv, w_up_bf, cos64_m, sin64_m, N_META)
    small_m = _proj_small(hm, w_small, 0, g_idx_pad, bf_pad, jnp.zeros((8, LANES), F32), cos32_m, sin32_m,
                          N_META)
    b_m = _proj_plain(hm, w_all, col_b, 3 * B_WIDTH, N_META, B_WIDTH)
    c_carry = jnp.broadcast_to(small_m[N_META - 1:N_META, :], (8, LANES))

    h = _rmsnorm(x, g_mix, tm)
    qk_scale = HEAD_DIM ** -0.5 * LOG2E
    tmw = _pick(t, 1024)
    qa = _proj_rope(h, w_all, col_qa, A_WIDTH, cos64_r, sin64_r, HEAD_DIM // 2, tmw, A_WIDTH, out_scale=qk_scale)
    ka_r, va_r = _proj_kv(h, w_all, col_ckv, g_kv, w_up_bf, cos64_r, sin64_r, tm)
    qi_hm = _proj_rope(h, w_all, col_qi, IDX_HEADS * IDX_DIM, cos32_r, sin32_r, IDX_DIM // 2, tmw,
                       IDX_HEADS * IDX_DIM, head_major=True)
    small_r = _proj_small(h, w_small, 0, g_idx_pad, bf_pad, c_carry, cos32_r, sin32_r, tm)
    b_r = _proj_plain(h, w_all, col_b, 3 * B_WIDTH, tmw, B_WIDTH, first_tile_scale=qk_scale)

    ki_real = small_r[:, SM_KI:SM_KI + IDX_DIM].astype(BF16)
    ki_meta = _pad_rows(small_m[:, SM_KI:SM_KI + IDX_DIM], META_PAD).astype(BF16)
    wt = small_r[:, SM_WI:SM_WI + IDX_HEADS].T
    cq = small_r[:, SM_FB:SM_FB + B_HEADS]
    ct_meta = _pad_rows(small_m[:, SM_FB:SM_FB + B_HEADS], META_PAD).T

    bias_real, bias_meta = _select_bias(qi_hm, ki_real, ki_meta, wt, topk, _pick(t, 256))
    ya = _attention((qa, 0, ka_r, 0, va_r, 0),
                    (_pad_rows(ka_m, META_PAD), 0, _pad_rows(va_m, META_PAD), 0),
                    (bias_real, bias_meta), fox=False, tq=_pick(t, 512), tk=_pick(t, 512))
    b_m_pad = _pad_rows(b_m, META_PAD)
    yb = _attention((b_r, 0, b_r, 1, b_r, 2), (b_m_pad, 1, b_m_pad, 2),
                    (cq, cq.T, ct_meta), fox=True, tq=_pick(t, 512), tk=_pick(t, 512))

    w_r = jnp.concatenate([w_group, w_expert, jnp.zeros((d, LANES - N_GROUPS - N_EXPERTS), F32)], axis=1)
    wr_hi = w_r.astype(BF16)
    wr_split = jnp.concatenate([wr_hi, (w_r - wr_hi.astype(F32)).astype(BF16)], axis=1)
    b_r_pad = jnp.concatenate([b_group, b_expert, jnp.zeros((LANES - N_GROUPS - N_EXPERTS,), F32)]).reshape(1, LANES)
    h1, xr, plan = _merge(h, ya, yb, w_all, col_ga, col_gb, w_branch_a.astype(BF16), w_branch_b.astype(BF16),
                          w_out.astype(BF16), x, g_ffn, wr_split, b_r_pad, tm, 256)

    ts = _pick(t, 512)
    pos, src, tile_group, tile_valid = _group_plan(plan, ts)
    ys = _group_ffn(tile_group, tile_valid, src, xr, w_gate_e, w_up_e, w_down_e.astype(BF16), ts)
    return _combine(pos, ys, h1, g_final, _pick(t, 256))


def kernel(x, meta_tokens, g_mix, w_in, g_kv, w_kv_up, g_idx_k, b_f, w_branch_a, w_branch_b, w_out, g_ffn,
           w_group, b_group, w_expert, b_expert, w_gate_e, w_up_e, w_down_e, g_final):
    assert x.shape[0] == 1 and g_mix.shape[0] == 1, "single batch, single layer"
    out = _layer(x[0], meta_tokens.astype(x.dtype), g_mix[0], w_in[0], g_kv[0], w_kv_up[0], g_idx_k[0], b_f[0],
                 w_branch_a[0], w_branch_b[0], w_out[0], g_ffn[0], w_group[0], b_group[0], w_expert[0],
                 b_expert[0], w_gate_e[0], w_up_e[0], w_down_e[0], g_final)
    return out[None]
```

```python
import functools

import jax
import jax.numpy as jnp
from jax import lax
from jax.experimental import pallas as pl
from jax.experimental.pallas import tpu as pltpu

F32 = jnp.float32
BF16 = jnp.bfloat16

D_MODEL = 2048
N_META = 16
HEAD_DIM = 128
ROPE_THETA = 10000.0
EPS = 1e-6
NEG = -1e30
A_HEADS = 8
A_WIDTH = A_HEADS * HEAD_DIM
A_KV_RANK = 512
IDX_HEADS = 16
IDX_DIM = 64
TOPK_MAX = 256
B_HEADS = 8
B_WIDTH = B_HEADS * HEAD_DIM
N_GROUPS = 4
EXPERTS_PER_GROUP = 4
N_EXPERTS = N_GROUPS * EXPERTS_PER_GROUP
D_FF_EXPERT = 512

LANES = 128
META_PAD = LANES
VMEM_LIMIT = 56 * 1024 * 1024
LOG2E = 1.4426950408889634
NT_DIMS = (((1,), (1,)), ((), ()))

SM_KI = 0
SM_WI = IDX_DIM
SM_FB = IDX_DIM + IDX_HEADS
RT_E0 = N_GROUPS


def _cparams(sem):
    return pltpu.CompilerParams(dimension_semantics=sem, vmem_limit_bytes=VMEM_LIMIT)


def _rms_kernel(x_ref, g_ref, o_ref):
    x = x_ref[...]
    ms = jnp.mean(x * x, axis=-1, keepdims=True)
    o_ref[...] = (x * lax.rsqrt(ms + EPS) * g_ref[...]).astype(o_ref.dtype)


def _rmsnorm(x, g, tm):
    n, d = x.shape
    return pl.pallas_call(
        _rms_kernel,
        grid=(n // tm,),
        in_specs=[pl.BlockSpec((tm, d), lambda i: (i, 0)), pl.BlockSpec((1, d), lambda i: (0, 0))],
        out_specs=pl.BlockSpec((tm, d), lambda i: (i, 0)),
        out_shape=jax.ShapeDtypeStruct((n, d), BF16),
        compiler_params=_cparams(("parallel",)),
        name="rmsnorm",
    )(x, g.reshape(1, d))


def _rope_tile(y, cos, sin_signed, half):
    if 2 * half == LANES:
        partner = pltpu.roll(y, half, axis=1)
    else:
        lane = lax.broadcasted_iota(jnp.int32, y.shape, 1)
        first = (lane % (2 * half)) < half
        partner = jnp.where(first, pltpu.roll(y, LANES - half, axis=1), pltpu.roll(y, half, axis=1))
    return y * cos + partner * sin_signed


def _rope_tables(pos, half):
    inv_freq = ROPE_THETA ** (-jnp.arange(half, dtype=F32) / half)
    ang = pos.astype(F32)[:, None] * inv_freq[None, :]
    cos, sin = jnp.cos(ang), jnp.sin(ang)
    reps = LANES // (2 * half)
    cos_t = jnp.tile(jnp.concatenate([cos, cos], axis=1), (1, reps))
    sin_t = jnp.tile(jnp.concatenate([-sin, sin], axis=1), (1, reps))
    return cos_t, sin_t


PACK_ROWS = 256


def _pack_kernel(src_ref, x_ref, o_ref):
    del src_ref
    o_ref[...] = x_ref[...].T.astype(o_ref.dtype)


def _pack_w_in(wt, sections):
    _, d = wt.shape
    starts = []
    for lo, hi in sections:
        assert lo % 8 == 0 and (hi - lo) % PACK_ROWS == 0
        starts += list(range(lo, hi, PACK_ROWS))
    src = jnp.asarray([s // 8 for s in starts], jnp.int32)
    return pl.pallas_call(
        _pack_kernel,
        grid_spec=pltpu.PrefetchScalarGridSpec(
            num_scalar_prefetch=1,
            grid=(len(starts),),
            in_specs=[pl.BlockSpec((pl.Element(PACK_ROWS), pl.Element(d)), lambda j, src: (src[j] * 8, 0))],
            out_specs=pl.BlockSpec((d, PACK_ROWS), lambda j, src: (0, j))),
        out_shape=jax.ShapeDtypeStruct((d, len(starts) * PACK_ROWS), BF16),
        compiler_params=_cparams(("arbitrary",)),
        name="pack_w_in",
    )(src, wt)


def _pack_small_kernel(a_ref, b_ref, o_ref, *, n_a, n_b):
    lane = lax.broadcasted_iota(jnp.int32, o_ref.shape, 1)
    o_ref[...] = jnp.where(lane < n_a, a_ref[...].T,
                           jnp.where(lane < n_a + n_b, b_ref[...].T, 0.0)).astype(o_ref.dtype)


def _pack_small(wt, rows_a, rows_b):
    _, d = wt.shape
    n_a, n_b = rows_a[1] - rows_a[0], rows_b[1] - rows_b[0]
    start_a, start_b = rows_a[0], rows_b[0] - n_a
    assert start_a % 8 == 0 and start_b % 8 == 0 and start_b >= 0 and n_a + n_b <= LANES
    window = lambda start: pl.BlockSpec((pl.Element(LANES), pl.Element(d)), lambda i: (start, 0))
    return pl.pallas_call(
        functools.partial(_pack_small_kernel, n_a=n_a, n_b=n_b),
        grid=(1,),
        in_specs=[window(start_a), window(start_b)],
        out_specs=pl.BlockSpec((d, LANES), lambda i: (0, 0)),
        out_shape=jax.ShapeDtypeStruct((d, LANES), BF16),
        compiler_params=_cparams(("arbitrary",)),
        name="pack_w_small",
    )(wt, wt)


def _proj_plain_kernel(h_ref, w_ref, o_ref, *, first_tile_scale):
    y = jnp.dot(h_ref[...], w_ref[...], preferred_element_type=F32)
    scale = jnp.where(pl.program_id(1) == 0, jnp.float32(first_tile_scale), jnp.float32(1.0))
    o_ref[...] = (y * scale).astype(o_ref.dtype)


def _proj_plain(h, w, col0, nc, tm, tn, first_tile_scale=1.0):
    n, d = h.shape
    c0 = col0 // tn
    return pl.pallas_call(
        functools.partial(_proj_plain_kernel, first_tile_scale=first_tile_scale),
        grid=(n // tm, nc // tn),
        in_specs=[pl.BlockSpec((tm, d), lambda i, j: (i, 0)), pl.BlockSpec((d, tn), lambda i, j: (0, c0 + j))],
        out_specs=pl.BlockSpec((tm, tn), lambda i, j: (i, j)),
        out_shape=jax.ShapeDtypeStruct((n, nc), BF16),
        compiler_params=_cparams(("parallel", "parallel")),
        name="proj_plain",
    )(h, w)


def _proj_rope_kernel(h_ref, w_ref, cos_ref, sin_ref, o_ref, *, half, head_major, out_scale):
    y = jnp.dot(h_ref[...], w_ref[...], preferred_element_type=F32)
    cos, sin = cos_ref[...], sin_ref[...]
    per_tile = LANES // (2 * half)
    for c in range(y.shape[1] // LANES):
        sl = slice(c * LANES, (c + 1) * LANES)
        roped = (_rope_tile(y[:, sl], cos, sin, half) * out_scale).astype(o_ref.dtype)
        if head_major:
            for k in range(per_tile):
                o_ref[c * per_tile + k] = roped[:, k * 2 * half:(k + 1) * 2 * half]
        else:
            o_ref[:, sl] = roped


def _proj_rope(h, w, col0, nc, cos, sin, half, tm, tn, head_major=False, out_scale=1.0):
    n, d = h.shape
    c0 = col0 // tn
    if head_major:
        assert tn == nc
        heads = nc // (2 * half)
        out_spec = pl.BlockSpec((heads, tm, 2 * half), lambda i, j: (0, i, 0))
        out_shape = jax.ShapeDtypeStruct((heads, n, 2 * half), BF16)
    else:
        out_spec = pl.BlockSpec((tm, tn), lambda i, j: (i, j))
        out_shape = jax.ShapeDtypeStruct((n, nc), BF16)
    return pl.pallas_call(
        functools.partial(_proj_rope_kernel, half=half, head_major=head_major, out_scale=out_scale),
        grid=(n // tm, nc // tn),
        in_specs=[pl.BlockSpec((tm, d), lambda i, j: (i, 0)), pl.BlockSpec((d, tn), lambda i, j: (0, c0 + j)),
                  pl.BlockSpec((tm, LANES), lambda i, j: (i, 0)), pl.BlockSpec((tm, LANES), lambda i, j: (i, 0))],
        out_specs=out_spec,
        out_shape=out_shape,
        compiler_params=_cparams(("parallel", "parallel")),
        name="proj_rope",
    )(h, w, cos, sin)


def _kv_kernel(h_ref, wc_ref, g_ref, wup_ref, cos_ref, sin_ref, ka_ref, va_ref):
    c = jnp.dot(h_ref[...], wc_ref[...], preferred_element_type=F32)
    ms = jnp.mean(c * c, axis=-1, keepdims=True)
    cn = (c * lax.rsqrt(ms + EPS) * g_ref[...]).astype(BF16)
    kv = jnp.dot(cn, wup_ref[...], preferred_element_type=F32)
    cos, sin = cos_ref[...], sin_ref[...]
    for hd in range(A_HEADS):
        sl = slice(hd * HEAD_DIM, (hd + 1) * HEAD_DIM)
        ka_ref[:, sl] = _rope_tile(kv[:, sl], cos, sin, HEAD_DIM // 2).astype(ka_ref.dtype)
    va_ref[...] = kv[:, A_WIDTH:].astype(va_ref.dtype)


def _proj_kv(h, wc, col0, g_kv, wup, cos, sin, tm):
    n, d = h.shape
    c0 = col0 // A_KV_RANK
    return pl.pallas_call(
        _kv_kernel,
        grid=(n // tm,),
        in_specs=[pl.BlockSpec((tm, d), lambda i: (i, 0)),
                  pl.BlockSpec((d, A_KV_RANK), lambda i: (0, c0)),
                  pl.BlockSpec((1, A_KV_RANK), lambda i: (0, 0)),
                  pl.BlockSpec((A_KV_RANK, 2 * A_WIDTH), lambda i: (0, 0)),
                  pl.BlockSpec((tm, LANES), lambda i: (i, 0)),
                  pl.BlockSpec((tm, LANES), lambda i: (i, 0))],
        out_specs=[pl.BlockSpec((tm, A_WIDTH), lambda i: (i, 0)), pl.BlockSpec((tm, A_WIDTH), lambda i: (i, 0))],
        out_shape=[jax.ShapeDtypeStruct((n, A_WIDTH), BF16), jax.ShapeDtypeStruct((n, A_WIDTH), BF16)],
        compiler_params=_cparams(("parallel",)),
        name="proj_kv",
    )(h, wc, g_kv.reshape(1, A_KV_RANK), wup, cos, sin)


def _small_kernel(h_ref, w_ref, g_ref, bf_ref, c0_ref, cos_ref, sin_ref, o_ref, carry_ref):
    tm = h_ref.shape[0]

    @pl.when(pl.program_id(0) == 0)
    def _():
        carry_ref[...] = c0_ref[...]

    y = jnp.dot(h_ref[...], w_ref[...], preferred_element_type=F32)
    lane = lax.broadcasted_iota(jnp.int32, y.shape, 1)
    is_ki = lane < SM_WI
    ms = jnp.sum(jnp.where(is_ki, y * y, 0.0), axis=-1, keepdims=True) * (1.0 / IDX_DIM)
    ki = _rope_tile(y * lax.rsqrt(ms + EPS) * g_ref[...], cos_ref[...], sin_ref[...], IDX_DIM // 2)
    wi = y * (IDX_HEADS ** -0.5 * IDX_DIM ** -0.5)
    z = y + bf_ref[...]
    logf = -(jnp.maximum(-z, 0.0) + jnp.log(1.0 + jnp.exp(-jnp.abs(z))))
    hi = logf.astype(BF16)
    r1 = logf - hi.astype(F32)
    mid = r1.astype(BF16)
    lo = (r1 - mid.astype(F32)).astype(BF16)
    row = lax.broadcasted_iota(jnp.int32, (tm, tm), 0)
    col = lax.broadcasted_iota(jnp.int32, (tm, tm), 1)
    tri = (col <= row).astype(BF16)
    csum = (jnp.dot(tri, hi, preferred_element_type=F32) + jnp.dot(tri, mid, preferred_element_type=F32)
            + jnp.dot(tri, lo, preferred_element_type=F32)) + carry_ref[0:1, :]
    carry_ref[...] = jnp.broadcast_to(csum[tm - 1:tm, :], carry_ref.shape)
    o_ref[...] = jnp.where(is_ki, ki, jnp.where(lane < SM_FB, wi, csum))


def _proj_small(h, w, col0, g_idx_pad, bf_pad, c0, cos, sin, tm):
    n, d = h.shape
    cb = col0 // LANES
    return pl.pallas_call(
        _small_kernel,
        grid=(n // tm,),
        in_specs=[pl.BlockSpec((tm, d), lambda i: (i, 0)),
                  pl.BlockSpec((d, LANES), lambda i: (0, cb)),
                  pl.BlockSpec((1, LANES), lambda i: (0, 0)),
                  pl.BlockSpec((1, LANES), lambda i: (0, 0)),
                  pl.BlockSpec((8, LANES), lambda i: (0, 0)),
                  pl.BlockSpec((tm, LANES), lambda i: (i, 0)),
                  pl.BlockSpec((tm, LANES), lambda i: (i, 0))],
        out_specs=pl.BlockSpec((tm, LANES), lambda i: (i, 0)),
        out_shape=jax.ShapeDtypeStruct((n, LANES), F32),
        scratch_shapes=[pltpu.VMEM((8, LANES), F32)],
        compiler_params=_cparams(("arbitrary",)),
        name="proj_small",
    )(h, w, g_idx_pad, bf_pad, c0, cos, sin)


COUNT_ROWS = 64
COUNT_UNROLL = 4


def _key_to_float(key):
    bits = key ^ ((key >> 31) & jnp.int32(0x7FFFFFFF))
    return lax.bitcast_convert_type(bits, F32)


def _select_kernel(qi_ref, kir_ref, kim_ref, wt_ref, or_ref, om_ref, s_ref, *, tq, topk):
    i = pl.program_id(0)
    n_real = or_ref.shape[1]
    n_chunks = n_real // tq
    krow = lax.broadcasted_iota(jnp.int32, (tq, tq), 0)
    qcol = lax.broadcasted_iota(jnp.int32, (tq, tq), 1)
    mrow = lax.broadcasted_iota(jnp.int32, (META_PAD, tq), 0)
    wt = wt_ref[...]

    def scores(kc):
        s = jnp.zeros((kc.shape[0], tq), F32)
        for h in range(IDX_HEADS):
            d = lax.dot_general(kc, qi_ref[h], NT_DIMS, preferred_element_type=F32)
            s = s + wt[h:h + 1, :] * jnp.maximum(d, 0.0)
        return s

    def causal(c):
        return krow + (c - i) * tq <= qcol

    def score_chunk(c, _):
        off = pl.multiple_of(c * tq, tq)
        s_ref[pl.ds(off, tq), :] = jnp.where(causal(c), scores(kir_ref[pl.ds(off, tq), :]), NEG)
        return 0

    lax.fori_loop(0, i + 1, score_chunk, 0)
    s_ref[n_real:, :] = jnp.where(mrow < N_META, scores(kim_ref[...]), NEG)

    def count(cmp, thr):
        def body(c, acc):
            off = c * tq
            for r in range(tq // COUNT_ROWS):
                blk = s_ref[pl.ds(pl.multiple_of(off + r * COUNT_ROWS, COUNT_ROWS), COUNT_ROWS), :]
                acc = acc + jnp.where(cmp(blk, thr), 1.0, 0.0)
            return acc

        groups = (i + 1) // COUNT_UNROLL

        def body_group(g, acc):
            for u in range(COUNT_UNROLL):
                acc = body(g * COUNT_UNROLL + u, acc)
            return acc

        acc = lax.fori_loop(0, groups, body_group, jnp.zeros((COUNT_ROWS, tq), F32))
        acc = lax.fori_loop(groups * COUNT_UNROLL, i + 1, body, acc)
        for r in range(META_PAD // COUNT_ROWS):
            blk = s_ref[n_real + r * COUNT_ROWS:n_real + (r + 1) * COUNT_ROWS, :]
            acc = acc + jnp.where(cmp(blk, thr), 1.0, 0.0)
        return jnp.sum(acc, axis=0, keepdims=True)

    kf = jnp.float32(topk)

    def bit_step(b, carry):
        t_key, cnt_t = carry
        cand = t_key + lax.shift_left(jnp.int32(1), 31 - b)
        cnt = count(lambda s, t: s >= t, _key_to_float(cand))
        take = cnt >= kf
        return jnp.where(take, cand, t_key), jnp.where(take, cnt, cnt_t)

    t0 = jnp.full((1, tq), jnp.iinfo(jnp.int32).min, jnp.int32)
    c0 = jnp.broadcast_to(((i + 1) * tq + META_PAD).astype(F32), (1, tq))
    t_key, cnt_t = lax.fori_loop(0, 32, bit_step, (t0, c0))
    thr = _key_to_float(t_key)

    def fill_chunk(c, _):
        or_ref[:, pl.ds(pl.multiple_of(c * tq, tq), tq)] = jnp.full((tq, tq), NEG, or_ref.dtype)
        return 0

    lax.fori_loop(i + 1, n_chunks, fill_chunk, 0)

    ambiguous = jnp.logical_and(cnt_t > kf, thr > NEG)
    any_amb = jnp.max(jnp.where(ambiguous, 1.0, 0.0)) > 0.0

    @pl.when(jnp.logical_not(any_amb))
    def _():
        def out_chunk(c, _):
            off = pl.multiple_of(c * tq, tq)
            sel = jnp.where(s_ref[pl.ds(off, tq), :] >= thr, 0.0, NEG)
            or_ref[:, pl.ds(off, tq)] = jnp.where(causal(c), sel, NEG).T.astype(or_ref.dtype)
            return 0

        lax.fori_loop(0, i + 1, out_chunk, 0)
        sel = jnp.where(s_ref[n_real:, :] >= thr, 0.0, NEG)
        om_ref[...] = jnp.where(mrow < N_META, sel, NEG).T.astype(om_ref.dtype)

    @pl.when(any_amb)
    def _():
        cnt_gt = count(lambda s, t: s > t, thr)
        quota = jnp.where(ambiguous, kf - cnt_gt, jnp.float32(3e38))

        def tie_select(s, valid, running):
            rows = s.shape[0]
            r = lax.broadcasted_iota(jnp.int32, (rows, rows), 0)
            c_ = lax.broadcasted_iota(jnp.int32, (rows, rows), 1)
            lower = (c_ <= r).astype(BF16)
            eq = jnp.where(s == thr, 1.0, 0.0)
            rank = jnp.dot(lower, eq.astype(BF16), preferred_element_type=F32) + running
            keep = jnp.where(s > thr, 0.0, jnp.where(jnp.logical_and(eq > 0.0, rank <= quota), 0.0, NEG))
            return jnp.where(valid, keep, NEG), running + jnp.sum(eq, axis=0, keepdims=True)

        bias_m, running = tie_select(s_ref[n_real:, :], mrow < N_META, jnp.zeros((1, tq), F32))
        om_ref[...] = bias_m.T.astype(om_ref.dtype)

        def out_chunk(c, running):
            off = pl.multiple_of(c * tq, tq)
            bias, running = tie_select(s_ref[pl.ds(off, tq), :], causal(c), running)
            or_ref[:, pl.ds(off, tq)] = bias.T.astype(or_ref.dtype)
            return running

        lax.fori_loop(0, i + 1, out_chunk, running)


def _select_bias(qi_hm, ki_real, ki_meta, wt, topk, tq):
    _, t, _ = qi_hm.shape
    return pl.pallas_call(
        functools.partial(_select_kernel, tq=tq, topk=topk),
        grid=(t // tq,),
        in_specs=[pl.BlockSpec((IDX_HEADS, tq, IDX_DIM), lambda i: (0, i, 0)),
                  pl.BlockSpec((t, IDX_DIM), lambda i: (0, 0)),
                  pl.BlockSpec((META_PAD, IDX_DIM), lambda i: (0, 0)),
                  pl.BlockSpec((IDX_HEADS, tq), lambda i: (0, i))],
        out_specs=[pl.BlockSpec((tq, t), lambda i: (i, 0)), pl.BlockSpec((tq, META_PAD), lambda i: (i, 0))],
        out_shape=[jax.ShapeDtypeStruct((t, t), BF16), jax.ShapeDtypeStruct((t, META_PAD), BF16)],
        scratch_shapes=[pltpu.VMEM((t + META_PAD, tq), F32)],
        compiler_params=_cparams(("parallel",)),
        name="select_bias",
    )(qi_hm, ki_real, ki_meta, wt)


def _attn_kernel(*refs, t, fox):
    if fox:
        (qi_ref, kj_ref, q_ref, k_ref, v_ref, km_ref, vm_ref, cq_ref, ck_ref, ckm_ref,
         o_ref, m_ref, l_ref, acc_ref, r_ref) = refs
        s_ref = None
    else:
        (qi_ref, kj_ref, q_ref, k_ref, v_ref, km_ref, vm_ref, b_ref, bm_ref,
         o_ref, m_ref, l_ref, acc_ref, s_ref) = refs
    stage_logits = s_ref is not None
    step = pl.program_id(0)
    i = qi_ref[step]
    j = kj_ref[step]
    nheads = q_ref.shape[1] // HEAD_DIM

    def block(k_blk_ref, v_blk_ref, bias_fn):
        width = k_blk_ref.shape[0]

        def logits(h):
            sl = slice(h * HEAD_DIM, (h + 1) * HEAD_DIM)
            return lax.dot_general(q_ref[:, sl], k_blk_ref[:, sl], NT_DIMS, preferred_element_type=F32)

        if stage_logits:
            for h in range(nheads):
                s_ref[h, :, :width] = logits(h)
        s_next = None if stage_logits else logits(0)
        for h in range(nheads):
            sl = slice(h * HEAD_DIM, (h + 1) * HEAD_DIM)
            if stage_logits:
                s = s_ref[h, :, :width]
            else:
                s, s_next = s_next, (logits(h + 1) if h + 1 < nheads else None)
            u = bias_fn(s, h)
            m_prev = m_ref[h]
            mx = jnp.max(u, axis=1, keepdims=True)
            if fox:
                r = r_ref[h]
                m_next = jnp.maximum(m_prev, mx + r)
                sub = m_next - r
            else:
                m_next = jnp.maximum(m_prev, mx)
                sub = m_next
            alpha = jnp.exp2(m_prev - m_next)
            ps = [jnp.exp2(u[:, c * LANES:(c + 1) * LANES] - sub) for c in range(width // LANES)]
            psum = ps[0]
            for pc in ps[1:]:
                psum = psum + pc
            p = ps[0] if len(ps) == 1 else jnp.concatenate(ps, axis=1)
            l_ref[h] = alpha * l_ref[h] + psum
            m_ref[h] = m_next
            acc_ref[h] = alpha * acc_ref[h] + jnp.dot(p.astype(BF16), v_blk_ref[:, sl],
                                                      preferred_element_type=F32)

    @pl.when(j == 0)
    def _():
        m_ref[...] = jnp.full(m_ref.shape, NEG, F32)
        l_ref[...] = jnp.zeros(l_ref.shape, F32)
        acc_ref[...] = jnp.zeros(acc_ref.shape, F32)
        mcol = lax.broadcasted_iota(jnp.int32, (t, META_PAD), 1)
        if fox:
            for h in range(nheads):
                r_ref[h] = jnp.broadcast_to(cq_ref[:, h:h + 1] * LOG2E, (t, LANES))

            def bias_fn(u, h):
                return jnp.where(mcol < N_META, u - ckm_ref[h:h + 1, :] * LOG2E, NEG)
        else:
            bias_m = bm_ref[...].astype(F32)

            def bias_fn(u, h):
                return u + bias_m
        block(km_ref, vm_ref, bias_fn)

    if fox:
        @pl.when(j < i)
        def _():
            block(k_ref, v_ref, lambda u, h: u - ck_ref[h:h + 1, :] * LOG2E)

        @pl.when(j == i)
        def _():
            row = lax.broadcasted_iota(jnp.int32, (t, t), 0)
            col = lax.broadcasted_iota(jnp.int32, (t, t), 1)
            block(k_ref, v_ref, lambda u, h: jnp.where(col <= row, u - ck_ref[h:h + 1, :] * LOG2E, NEG))
    else:
        bias = b_ref[...].astype(F32)
        block(k_ref, v_ref, lambda u, h: u + bias)

    @pl.when(j == i)
    def _():
        for h in range(nheads):
            sl = slice(h * HEAD_DIM, (h + 1) * HEAD_DIM)
            o_ref[:, sl] = (acc_ref[h] / jnp.sum(l_ref[h], axis=1, keepdims=True)).astype(o_ref.dtype)


def _attention(qkv, meta_kv, extra, *, fox, t_blk):
    q_arr, q_col, k_arr, k_col, v_arr, v_col = qkv
    km_arr, km_col, vm_arr, vm_col = meta_kv
    t = q_arr.shape[0]
    w = A_WIDTH
    nheads = w // HEAD_DIM
    nq = t // t_blk
    qi = jnp.asarray([i for i in range(nq) for _ in range(i + 1)], jnp.int32)
    kj = jnp.asarray([j for i in range(nq) for j in range(i + 1)], jnp.int32)

    in_specs = [pl.BlockSpec((t_blk, w), lambda s, qi, kj: (qi[s], q_col)),
                pl.BlockSpec((t_blk, w), lambda s, qi, kj: (kj[s], k_col)),
                pl.BlockSpec((t_blk, w), lambda s, qi, kj: (kj[s], v_col)),
                pl.BlockSpec((META_PAD, w), lambda s, qi, kj: (0, km_col)),
                pl.BlockSpec((META_PAD, w), lambda s, qi, kj: (0, vm_col))]
    scratch = [pltpu.VMEM((nheads, t_blk, LANES), F32), pltpu.VMEM((nheads, t_blk, LANES), F32),
               pltpu.VMEM((nheads, t_blk, HEAD_DIM), F32)]
    args = (q_arr, k_arr, v_arr, km_arr, vm_arr)
    if fox:
        cq, ct_real, ct_meta = extra
        in_specs += [pl.BlockSpec((t_blk, nheads), lambda s, qi, kj: (qi[s], 0)),
                     pl.BlockSpec((nheads, t_blk), lambda s, qi, kj: (0, kj[s])),
                     pl.BlockSpec((nheads, META_PAD), lambda s, qi, kj: (0, 0))]
        args += (cq, ct_real, ct_meta)
        scratch.append(pltpu.VMEM((nheads, t_blk, LANES), F32))
    else:
        bias_real, bias_meta = extra
        in_specs += [pl.BlockSpec((t_blk, t_blk), lambda s, qi, kj: (qi[s], kj[s])),
                     pl.BlockSpec((t_blk, META_PAD), lambda s, qi, kj: (qi[s], 0))]
        args += (bias_real, bias_meta)
        scratch.append(pltpu.VMEM((nheads, t_blk, t_blk), F32))
    return pl.pallas_call(
        functools.partial(_attn_kernel, t=t_blk, fox=fox),
        grid_spec=pltpu.PrefetchScalarGridSpec(
            num_scalar_prefetch=2,
            grid=(int(qi.shape[0]),),
            in_specs=in_specs,
            out_specs=pl.BlockSpec((t_blk, w), lambda s, qi, kj: (qi[s], 0)),
            scratch_shapes=scratch),
        out_shape=jax.ShapeDtypeStruct((t, w), BF16),
        compiler_params=_cparams(("arbitrary",)),
        name="attn_fox" if fox else "attn_dsa",
    )(qi, kj, *args)


def _route(logits):
    lane = lax.broadcasted_iota(jnp.int32, logits.shape, 1).astype(F32)
    big = jnp.float32(LANES)
    ninf = -jnp.inf
    gl = jnp.where(lane < N_GROUPS, logits, ninf)
    gmax = jnp.max(gl, axis=1, keepdims=True)
    g_sel = jnp.min(jnp.where(gl == gmax, lane, big), axis=1, keepdims=True)
    p_group = 1.0 / jnp.sum(jnp.exp(gl - gmax), axis=1, keepdims=True)
    e_lo = RT_E0 + g_sel * EXPERTS_PER_GROUP
    in_grp = jnp.logical_and(lane >= e_lo, lane < e_lo + EXPERTS_PER_GROUP)
    el = jnp.where(in_grp, logits, ninf)
    emax = jnp.max(el, axis=1, keepdims=True)
    ex = jnp.exp(el - emax)
    prob = ex / jnp.sum(ex, axis=1, keepdims=True)
    prob = jnp.where(in_grp, prob, -1.0)
    p1 = jnp.max(prob, axis=1, keepdims=True)
    i1 = jnp.min(jnp.where(prob == p1, lane, big), axis=1, keepdims=True)
    rest = jnp.where(lane == i1, -1.0, prob)
    p2 = jnp.max(rest, axis=1, keepdims=True)
    i2 = jnp.min(jnp.where(rest == p2, lane, big), axis=1, keepdims=True)
    tot = p1 + p2
    w1 = p_group * (p1 / tot)
    w2 = p_group * (p2 / tot)
    return jnp.where(lane == i1, w1, jnp.where(lane == i2, w2, 0.0)), g_sel


def _merge_kernel(h_ref, ya_ref, yb_ref, wga_ref, wgb_ref, wa_ref, wb_ref, wo_ref, x_ref, g_ref,
                  wr_ref, br_ref, h1_ref, xr_ref, plan_ref, acc_ref, cnt_ref):
    n = pl.program_id(1)

    @pl.when(jnp.logical_and(pl.program_id(0) == 0, n == 0))
    def _():
        cnt_ref[...] = jnp.zeros(cnt_ref.shape, F32)

    @pl.when(n == 0)
    def _():
        acc_ref[...] = jnp.zeros(acc_ref.shape, F32)

    h = h_ref[...]
    ga = jnp.dot(h, wga_ref[...], preferred_element_type=F32)
    gb = jnp.dot(h, wgb_ref[...], preferred_element_type=F32)
    pa = jnp.dot(ya_ref[...], wa_ref[...], preferred_element_type=F32)
    pb = jnp.dot(yb_ref[...], wb_ref[...], preferred_element_type=F32)
    merged = jax.nn.sigmoid(ga) * pa + jax.nn.sigmoid(gb) * pb
    acc_ref[...] += jnp.dot(merged.astype(BF16), wo_ref[...], preferred_element_type=F32)

    @pl.when(n == pl.num_programs(1) - 1)
    def _():
        h1 = x_ref[...] + acc_ref[...]
        h1_ref[...] = h1
        ms = jnp.mean(h1 * h1, axis=-1, keepdims=True)
        t = h1 * lax.rsqrt(ms + EPS) * g_ref[...]
        t_hi = t.astype(BF16)
        t_lo = (t - t_hi.astype(F32)).astype(BF16)
        wr = wr_ref[...]
        y = jnp.dot(t_hi, wr, preferred_element_type=F32)
        logits = (y[:, :LANES] + y[:, LANES:]
                  + jnp.dot(t_lo, wr[:, :LANES], preferred_element_type=F32)) + br_ref[...]
        gates, g_sel = _route(logits)
        d = t.shape[1]
        xr_ref[:, :d] = t
        xr_ref[:, d:] = gates
        tm = t.shape[0]
        lane = lax.broadcasted_iota(jnp.int32, gates.shape, 1).astype(F32)
        onehot = jnp.where(lane == g_sel, 1.0, 0.0).astype(BF16)
        rr = lax.broadcasted_iota(jnp.int32, (tm, tm), 0)
        cc = lax.broadcasted_iota(jnp.int32, (tm, tm), 1)
        cum = jnp.dot((cc <= rr).astype(BF16), onehot, preferred_element_type=F32) + cnt_ref[0:1, :]
        cnt_ref[...] = jnp.broadcast_to(cum[tm - 1:tm, :], cnt_ref.shape)
        plan_ref[...] = jnp.where(lane < N_GROUPS, cum, jnp.where(lane == N_GROUPS, g_sel, 0.0))


def _merge(h, ya, yb, w_all, ga_col0, gb_col0, wa, wb, wo, x, g_ffn, wr_split, br, tm, tn):
    t, d = h.shape
    row = lambda i, n: (i, 0)
    ga0, gb0 = ga_col0 // tn, gb_col0 // tn
    wga = wgb = w_all
    return pl.pallas_call(
        _merge_kernel,
        grid=(t // tm, d // tn),
        in_specs=[pl.BlockSpec((tm, d), row),
                  pl.BlockSpec((tm, A_WIDTH), row),
                  pl.BlockSpec((tm, B_WIDTH), row),
                  pl.BlockSpec((d, tn), lambda i, n: (0, ga0 + n)),
                  pl.BlockSpec((d, tn), lambda i, n: (0, gb0 + n)),
                  pl.BlockSpec((A_WIDTH, tn), lambda i, n: (0, n)),
                  pl.BlockSpec((B_WIDTH, tn), lambda i, n: (0, n)),
                  pl.BlockSpec((tn, d), lambda i, n: (n, 0)),
                  pl.BlockSpec((tm, d), row),
                  pl.BlockSpec((1, d), lambda i, n: (0, 0)),
                  pl.BlockSpec((d, 2 * LANES), lambda i, n: (0, 0)),
                  pl.BlockSpec((1, LANES), lambda i, n: (0, 0))],
        out_specs=[pl.BlockSpec((tm, d), row), pl.BlockSpec((tm, d + LANES), row),
                   pl.BlockSpec((tm, LANES), row)],
        out_shape=[jax.ShapeDtypeStruct((t, d), F32), jax.ShapeDtypeStruct((t, d + LANES), F32),
                   jax.ShapeDtypeStruct((t, LANES), F32)],
        scratch_shapes=[pltpu.VMEM((tm, d), F32), pltpu.VMEM((8, LANES), F32)],
        compiler_params=_cparams(("arbitrary", "arbitrary")),
        name="merge_out_router",
    )(h, ya, yb, wga, wgb, wa, wb, wo, x, g_ffn.reshape(1, d), wr_split, br)


def _row_copy(src_ref, src_row, dst_ref, dst_row, sem):
    return pltpu.make_async_copy(src_ref.at[pl.ds(src_row, 1), :], dst_ref.at[pl.ds(dst_row, 1), :], sem)


def _dispatch_kernel(pos_ref, x_ref, init_ref, xs_ref, sem):
    del init_ref
    i = pl.program_id(0)
    tm = x_ref.shape[0]

    def issue(r, _):
        _row_copy(x_ref, r, xs_ref, pos_ref[i * tm + r], sem).start()
        return 0

    lax.fori_loop(0, tm, issue, 0)

    def drain(r, _):
        _row_copy(x_ref, r, xs_ref, pos_ref[i * tm + r], sem).wait()
        return 0

    lax.fori_loop(0, tm, drain, 0)


def _dispatch(pos, xr, n_slots, tm):
    t, w = xr.shape
    return pl.pallas_call(
        _dispatch_kernel,
        grid_spec=pltpu.PrefetchScalarGridSpec(
            num_scalar_prefetch=1,
            grid=(t // tm,),
            in_specs=[pl.BlockSpec((tm, w), lambda i, pos: (i, 0)),
                      pl.BlockSpec(memory_space=pl.ANY)],
            out_specs=pl.BlockSpec(memory_space=pl.ANY),
            scratch_shapes=[pltpu.SemaphoreType.DMA]),
        out_shape=jax.ShapeDtypeStruct((n_slots, w), xr.dtype),
        input_output_aliases={2: 0},
        compiler_params=_cparams(("arbitrary",)),
        name="moe_dispatch",
    )(pos, xr, jnp.zeros((n_slots, w), xr.dtype))


def _group_ffn_kernel(tg_ref, tv_ref, x_ref, wg_ref, wu_ref, wd_ref, o_ref, acc_ref, xb_ref):
    j = pl.program_id(0)
    k = pl.program_id(1)
    d = o_ref.shape[1]

    @pl.when(k == 0)
    def _():
        acc_ref[...] = jnp.zeros(acc_ref.shape, F32)
        xb_ref[...] = x_ref[:, :d].astype(BF16)

    @pl.when(tv_ref[j] > 0)
    def _():
        gates = x_ref[:, d:]
        lane = lax.broadcasted_iota(jnp.int32, gates.shape, 1)
        e_lane = RT_E0 + tg_ref[j] * EXPERTS_PER_GROUP + k
        g_e = jnp.sum(jnp.where(lane == e_lane, gates, 0.0), axis=1, keepdims=True)
        xb = xb_ref[...]
        hg = jnp.dot(xb, wg_ref[0].astype(BF16), preferred_element_type=F32)
        hu = jnp.dot(xb, wu_ref[0].astype(BF16), preferred_element_type=F32)
        act = jax.nn.silu(hg) * hu * g_e
        acc_ref[...] += jnp.dot(act.astype(BF16), wd_ref[0].astype(BF16), preferred_element_type=F32)

    @pl.when(k == pl.num_programs(1) - 1)
    def _():
        o_ref[...] = acc_ref[...]


def _group_ffn(tile_group, tile_valid, xs, wg, wu, wd, ts):
    n_slots, w = xs.shape
    d = w - LANES
    f = wg.shape[2]
    expert = lambda j, k, tg, tv: (tg[j] * EXPERTS_PER_GROUP + k, 0, 0)
    return pl.pallas_call(
        _group_ffn_kernel,
        grid_spec=pltpu.PrefetchScalarGridSpec(
            num_scalar_prefetch=2,
            grid=(n_slots // ts, EXPERTS_PER_GROUP),
            in_specs=[pl.BlockSpec((ts, w), lambda j, k, tg, tv: (j, 0)),
                      pl.BlockSpec((1, d, f), expert),
                      pl.BlockSpec((1, d, f), expert),
                      pl.BlockSpec((1, f, d), expert)],
            out_specs=pl.BlockSpec((ts, d), lambda j, k, tg, tv: (j, 0)),
            scratch_shapes=[pltpu.VMEM((ts, d), F32), pltpu.VMEM((ts, d), BF16)]),
        out_shape=jax.ShapeDtypeStruct((n_slots, d), F32),
        compiler_params=_cparams(("arbitrary", "arbitrary")),
        name="moe_group_ffn",
    )(tile_group, tile_valid, xs, wg, wu, wd)


def _combine_kernel(pos_ref, ys_ref, h1_ref, gf_ref, o_ref, ybuf_ref, sem):
    i = pl.program_id(0)
    tm = h1_ref.shape[0]

    def issue(r, _):
        _row_copy(ys_ref, pos_ref[i * tm + r], ybuf_ref, r, sem).start()
        return 0

    lax.fori_loop(0, tm, issue, 0)

    def drain(r, _):
        _row_copy(ys_ref, pos_ref[i * tm + r], ybuf_ref, r, sem).wait()
        return 0

    lax.fori_loop(0, tm, drain, 0)
    h2 = h1_ref[...] + ybuf_ref[...]
    ms = jnp.mean(h2 * h2, axis=-1, keepdims=True)
    o_ref[...] = h2 * lax.rsqrt(ms + EPS) * gf_ref[...]


def _combine(pos, ys, h1, g_final, tm):
    t, d = h1.shape
    return pl.pallas_call(
        _combine_kernel,
        grid_spec=pltpu.PrefetchScalarGridSpec(
            num_scalar_prefetch=1,
            grid=(t // tm,),
            in_specs=[pl.BlockSpec(memory_space=pl.ANY),
                      pl.BlockSpec((tm, d), lambda i, pos: (i, 0)),
                      pl.BlockSpec((1, d), lambda i, pos: (0, 0))],
            out_specs=pl.BlockSpec((tm, d), lambda i, pos: (i, 0)),
            scratch_shapes=[pltpu.VMEM((tm, d), F32), pltpu.SemaphoreType.DMA]),
        out_shape=jax.ShapeDtypeStruct((t, d), F32),
        compiler_params=_cparams(("arbitrary",)),
        name="moe_combine_final",
    )(pos, ys, h1, g_final.reshape(1, d))


def _group_plan(plan, ts):
    t = plan.shape[0]
    g_sel = plan[:, N_GROUPS].astype(jnp.int32)
    cum_sel = jnp.take_along_axis(plan[:, :N_GROUPS], g_sel[:, None], axis=1)[:, 0]
    counts = plan[t - 1, :N_GROUPS].astype(jnp.int32)
    padded = (counts + ts - 1) // ts * ts
    ends = jnp.cumsum(padded)
    starts = ends - padded
    pos = starts[g_sel] + cum_sel.astype(jnp.int32) - 1
    n_tiles = (t + N_GROUPS * (ts - 1)) // ts
    tile_start = jnp.arange(n_tiles, dtype=jnp.int32) * ts
    tile_group = jnp.minimum(jnp.sum(tile_start[:, None] >= ends[None, :], axis=1), N_GROUPS - 1)
    tile_valid = (tile_start < ends[N_GROUPS - 1]).astype(jnp.int32)
    return pos, tile_group.astype(jnp.int32), tile_valid, n_tiles * ts


def _pick(n, pref):
    return pref if n % pref == 0 else n


def _pad_rows(a, rows):
    return jnp.pad(a, ((0, rows - a.shape[0]), (0, 0)))


def _layer(x, meta, g_mix, w_in, g_kv, w_kv_up, g_idx_k, b_f, w_branch_a, w_branch_b, w_out,
           g_ffn, w_group, b_group, w_expert, b_expert, w_gate_e, w_up_e, w_down_e, g_final):
    t, d = x.shape
    n_all = t + N_META
    topk = min(TOPK_MAX, n_all // 4)

    o = [0]
    for wdt in (A_WIDTH, A_KV_RANK, IDX_HEADS * IDX_DIM, IDX_DIM, IDX_HEADS, B_WIDTH, B_WIDTH, B_WIDTH,
                B_HEADS, D_MODEL, D_MODEL):
        o.append(o[-1] + wdt)
    w_in_t = w_in.T
    w_all = _pack_w_in(w_in_t, [(o[0], o[1]), (o[2], o[3]), (o[5], o[8]), (o[9], o[10]), (o[10], o[11]),
                                (o[1], o[2])])
    w_small = _pack_small(w_in_t, (o[3], o[5]), (o[8], o[9]))
    col_qa = 0
    col_qi = col_qa + A_WIDTH
    col_b = col_qi + IDX_HEADS * IDX_DIM
    col_ga = col_b + 3 * B_WIDTH
    col_gb = col_ga + D_MODEL
    col_ckv = col_gb + D_MODEL
    w_up_bf = w_kv_up.astype(BF16)
    g_idx_pad = jnp.zeros((1, LANES), F32).at[0, SM_KI:SM_KI + IDX_DIM].set(g_idx_k)
    bf_pad = jnp.zeros((1, LANES), F32).at[0, SM_FB:SM_FB + B_HEADS].set(b_f)

    pos_meta = jnp.arange(N_META)
    pos_real = jnp.arange(N_META, n_all)
    cos64_m, sin64_m = _rope_tables(pos_meta, HEAD_DIM // 2)
    cos64_r, sin64_r = _rope_tables(pos_real, HEAD_DIM // 2)
    cos32_m, sin32_m = _rope_tables(pos_meta, IDX_DIM // 2)
    cos32_r, sin32_r = _rope_tables(pos_real, IDX_DIM // 2)

    tm = _pick(t, 512)

    hm = _rmsnorm(meta, g_mix, N_META)
    ka_m, va_m = _proj_kv(hm, w_all, col_ckv, g_kv, w_up_bf, cos64_m, sin64_m, N_META)
    small_m = _proj_small(hm, w_small, 0, g_idx_pad, bf_pad, jnp.zeros((8, LANES), F32), cos32_m, sin32_m,
                          N_META)
    b_m = _proj_plain(hm, w_all, col_b, 3 * B_WIDTH, N_META, B_WIDTH)
    c_carry = jnp.broadcast_to(small_m[N_META - 1:N_META, :], (8, LANES))

    h = _rmsnorm(x, g_mix, tm)
    qk_scale = HEAD_DIM ** -0.5 * LOG2E
    tmw = _pick(t, 1024)
    qa = _proj_rope(h, w_all, col_qa, A_WIDTH, cos64_r, sin64_r, HEAD_DIM // 2, tmw, A_WIDTH, out_scale=qk_scale)
    ka_r, va_r = _proj_kv(h, w_all, col_ckv, g_kv, w_up_bf, cos64_r, sin64_r, tm)
    qi_hm = _proj_rope(h, w_all, col_qi, IDX_HEADS * IDX_DIM, cos32_r, sin32_r, IDX_DIM // 2, tmw,
                       IDX_HEADS * IDX_DIM, head_major=True)
    small_r = _proj_small(h, w_small, 0, g_idx_pad, bf_pad, c_carry, cos32_r, sin32_r, tm)
    b_r = _proj_plain(h, w_all, col_b, 3 * B_WIDTH, tmw, B_WIDTH, first_tile_scale=qk_scale)

    ki_real = small_r[:, SM_KI:SM_KI + IDX_DIM].astype(BF16)
    ki_meta = _pad_rows(small_m[:, SM_KI:SM_KI + IDX_DIM], META_PAD).astype(BF16)
    wt = small_r[:, SM_WI:SM_WI + IDX_HEADS].T
    cq = small_r[:, SM_FB:SM_FB + B_HEADS]
    ct_meta = _pad_rows(small_m[:, SM_FB:SM_FB + B_HEADS], META_PAD).T

    bias_real, bias_meta = _select_bias(qi_hm, ki_real, ki_meta, wt, topk, _pick(t, 256))
    ya = _attention((qa, 0, ka_r, 0, va_r, 0),
                    (_pad_rows(ka_m, META_PAD), 0, _pad_rows(va_m, META_PAD), 0),
                    (bias_real, bias_meta), fox=False, t_blk=_pick(t, 512))
    b_m_pad = _pad_rows(b_m, META_PAD)
    yb = _attention((b_r, 0, b_r, 1, b_r, 2), (b_m_pad, 1, b_m_pad, 2),
                    (cq, cq.T, ct_meta), fox=True, t_blk=_pick(t, 512))

    w_r = jnp.concatenate([w_group, w_expert, jnp.zeros((d, LANES - N_GROUPS - N_EXPERTS), F32)], axis=1)
    wr_hi = w_r.astype(BF16)
    wr_split = jnp.concatenate([wr_hi, (w_r - wr_hi.astype(F32)).astype(BF16)], axis=1)
    b_r_pad = jnp.concatenate([b_group, b_expert, jnp.zeros((LANES - N_GROUPS - N_EXPERTS,), F32)]).reshape(1, LANES)
    h1, xr, plan = _merge(h, ya, yb, w_all, col_ga, col_gb, w_branch_a.astype(BF16), w_branch_b.astype(BF16),
                          w_out.astype(BF16), x, g_ffn, wr_split, b_r_pad, tm, 256)

    ts = _pick(t, 512)
    pos, tile_group, tile_valid, n_slots = _group_plan(plan, ts)
    xs = _dispatch(pos, xr, n_slots, _pick(t, 256))
    ys = _group_ffn(tile_group, tile_valid, xs, w_gate_e, w_up_e, w_down_e.astype(BF16), ts)
    return _combine(pos, ys, h1, g_final, _pick(t, 256))


def kernel(x, meta_tokens, g_mix, w_in, g_kv, w_kv_up, g_idx_k, b_f, w_branch_a, w_branch_b, w_out, g_ffn,
           w_group, b_group, w_expert, b_expert, w_gate_e, w_up_e, w_down_e, g_final):
    assert x.shape[0] == 1 and g_mix.shape[0] == 1, "single batch, single layer"
    out = _layer(x[0], meta_tokens.astype(x.dtype), g_mix[0], w_in[0], g_kv[0], w_kv_up[0], g_idx_k[0], b_f[0],
                 w_branch_a[0], w_branch_b[0], w_out[0], g_ffn[0], w_group[0], b_group[0], w_expert[0],
                 b_expert[0], w_gate_e[0], w_up_e[0], w_down_e[0], g_final)
    return out[None]
```

```python
import functools

import jax
import jax.numpy as jnp
from jax import lax
from jax.experimental import pallas as pl
from jax.experimental.pallas import tpu as pltpu

F32 = jnp.float32
BF16 = jnp.bfloat16

D_MODEL = 2048
N_META = 16
HEAD_DIM = 128
ROPE_THETA = 10000.0
EPS = 1e-6
NEG = -1e30
A_HEADS = 8
A_WIDTH = A_HEADS * HEAD_DIM
A_KV_RANK = 512
IDX_HEADS = 16
IDX_DIM = 64
TOPK_MAX = 256
B_HEADS = 8
B_WIDTH = B_HEADS * HEAD_DIM
N_GROUPS = 4
EXPERTS_PER_GROUP = 4
N_EXPERTS = N_GROUPS * EXPERTS_PER_GROUP
D_FF_EXPERT = 512

LANES = 128
META_PAD = LANES
VMEM_LIMIT = 56 * 1024 * 1024
LOG2E = 1.4426950408889634
NT_DIMS = (((1,), (1,)), ((), ()))

SM_KI = 0
SM_WI = IDX_DIM
SM_FB = IDX_DIM + IDX_HEADS
RT_E0 = N_GROUPS


def _cparams(sem):
    return pltpu.CompilerParams(dimension_semantics=sem, vmem_limit_bytes=VMEM_LIMIT)


def _rms_kernel(x_ref, g_ref, o_ref):
    x = x_ref[...]
    ms = jnp.mean(x * x, axis=-1, keepdims=True)
    o_ref[...] = (x * lax.rsqrt(ms + EPS) * g_ref[...]).astype(o_ref.dtype)


def _rmsnorm(x, g, tm):
    n, d = x.shape
    return pl.pallas_call(
        _rms_kernel,
        grid=(n // tm,),
        in_specs=[pl.BlockSpec((tm, d), lambda i: (i, 0)), pl.BlockSpec((1, d), lambda i: (0, 0))],
        out_specs=pl.BlockSpec((tm, d), lambda i: (i, 0)),
        out_shape=jax.ShapeDtypeStruct((n, d), BF16),
        compiler_params=_cparams(("parallel",)),
        name="rmsnorm",
    )(x, g.reshape(1, d))


def _rope_tile(y, cos, sin_signed, half):
    if 2 * half == LANES:
        partner = pltpu.roll(y, half, axis=1)
    else:
        lane = lax.broadcasted_iota(jnp.int32, y.shape, 1)
        first = (lane % (2 * half)) < half
        partner = jnp.where(first, pltpu.roll(y, LANES - half, axis=1), pltpu.roll(y, half, axis=1))
    return y * cos + partner * sin_signed


def _rope_tables(pos, half):
    inv_freq = ROPE_THETA ** (-jnp.arange(half, dtype=F32) / half)
    ang = pos.astype(F32)[:, None] * inv_freq[None, :]
    cos, sin = jnp.cos(ang), jnp.sin(ang)
    reps = LANES // (2 * half)
    cos_t = jnp.tile(jnp.concatenate([cos, cos], axis=1), (1, reps))
    sin_t = jnp.tile(jnp.concatenate([-sin, sin], axis=1), (1, reps))
    return cos_t, sin_t


PACK_ROWS = 256


def _pack_kernel(src_ref, x_ref, o_ref):
    del src_ref
    o_ref[...] = x_ref[...].T.astype(o_ref.dtype)


def _pack_w_in(wt, sections):
    _, d = wt.shape
    starts = []
    for lo, hi in sections:
        assert lo % 8 == 0 and (hi - lo) % PACK_ROWS == 0
        starts += list(range(lo, hi, PACK_ROWS))
    src = jnp.asarray([s // 8 for s in starts], jnp.int32)
    return pl.pallas_call(
        _pack_kernel,
        grid_spec=pltpu.PrefetchScalarGridSpec(
            num_scalar_prefetch=1,
            grid=(len(starts),),
            in_specs=[pl.BlockSpec((pl.Element(PACK_ROWS), pl.Element(d)), lambda j, src: (src[j] * 8, 0))],
            out_specs=pl.BlockSpec((d, PACK_ROWS), lambda j, src: (0, j))),
        out_shape=jax.ShapeDtypeStruct((d, len(starts) * PACK_ROWS), BF16),
        compiler_params=_cparams(("arbitrary",)),
        name="pack_w_in",
    )(src, wt)


def _pack_small_kernel(a_ref, b_ref, o_ref, *, n_a, n_b):
    lane = lax.broadcasted_iota(jnp.int32, o_ref.shape, 1)
    o_ref[...] = jnp.where(lane < n_a, a_ref[...].T,
                           jnp.where(lane < n_a + n_b, b_ref[...].T, 0.0)).astype(o_ref.dtype)


def _pack_small(wt, rows_a, rows_b):
    _, d = wt.shape
    n_a, n_b = rows_a[1] - rows_a[0], rows_b[1] - rows_b[0]
    start_a, start_b = rows_a[0], rows_b[0] - n_a
    assert start_a % 8 == 0 and start_b % 8 == 0 and start_b >= 0 and n_a + n_b <= LANES
    window = lambda start: pl.BlockSpec((pl.Element(LANES), pl.Element(d)), lambda i: (start, 0))
    return pl.pallas_call(
        functools.partial(_pack_small_kernel, n_a=n_a, n_b=n_b),
        grid=(1,),
        in_specs=[window(start_a), window(start_b)],
        out_specs=pl.BlockSpec((d, LANES), lambda i: (0, 0)),
        out_shape=jax.ShapeDtypeStruct((d, LANES), BF16),
        compiler_params=_cparams(("arbitrary",)),
        name="pack_w_small",
    )(wt, wt)


def _proj_plain_kernel(h_ref, w_ref, o_ref, *, first_tile_scale):
    y = jnp.dot(h_ref[...], w_ref[...], preferred_element_type=F32)
    scale = jnp.where(pl.program_id(1) == 0, jnp.float32(first_tile_scale), jnp.float32(1.0))
    o_ref[...] = (y * scale).astype(o_ref.dtype)


def _proj_plain(h, w, col0, nc, tm, tn, first_tile_scale=1.0):
    n, d = h.shape
    c0 = col0 // tn
    return pl.pallas_call(
        functools.partial(_proj_plain_kernel, first_tile_scale=first_tile_scale),
        grid=(n // tm, nc // tn),
        in_specs=[pl.BlockSpec((tm, d), lambda i, j: (i, 0)), pl.BlockSpec((d, tn), lambda i, j: (0, c0 + j))],
        out_specs=pl.BlockSpec((tm, tn), lambda i, j: (i, j)),
        out_shape=jax.ShapeDtypeStruct((n, nc), BF16),
        compiler_params=_cparams(("parallel", "parallel")),
        name="proj_plain",
    )(h, w)


def _proj_rope_kernel(h_ref, w_ref, cos_ref, sin_ref, o_ref, *, half, head_major, out_scale):
    y = jnp.dot(h_ref[...], w_ref[...], preferred_element_type=F32)
    cos, sin = cos_ref[...], sin_ref[...]
    per_tile = LANES // (2 * half)
    for c in range(y.shape[1] // LANES):
        sl = slice(c * LANES, (c + 1) * LANES)
        roped = (_rope_tile(y[:, sl], cos, sin, half) * out_scale).astype(o_ref.dtype)
        if head_major:
            for k in range(per_tile):
                o_ref[c * per_tile + k] = roped[:, k * 2 * half:(k + 1) * 2 * half]
        else:
            o_ref[:, sl] = roped


def _proj_rope(h, w, col0, nc, cos, sin, half, tm, tn, head_major=False, out_scale=1.0):
    n, d = h.shape
    c0 = col0 // tn
    if head_major:
        assert tn == nc
        heads = nc // (2 * half)
        out_spec = pl.BlockSpec((heads, tm, 2 * half), lambda i, j: (0, i, 0))
        out_shape = jax.ShapeDtypeStruct((heads, n, 2 * half), BF16)
    else:
        out_spec = pl.BlockSpec((tm, tn), lambda i, j: (i, j))
        out_shape = jax.ShapeDtypeStruct((n, nc), BF16)
    return pl.pallas_call(
        functools.partial(_proj_rope_kernel, half=half, head_major=head_major, out_scale=out_scale),
        grid=(n // tm, nc // tn),
        in_specs=[pl.BlockSpec((tm, d), lambda i, j: (i, 0)), pl.BlockSpec((d, tn), lambda i, j: (0, c0 + j)),
                  pl.BlockSpec((tm, LANES), lambda i, j: (i, 0)), pl.BlockSpec((tm, LANES), lambda i, j: (i, 0))],
        out_specs=out_spec,
        out_shape=out_shape,
        compiler_params=_cparams(("parallel", "parallel")),
        name="proj_rope",
    )(h, w, cos, sin)


def _kv_kernel(h_ref, wc_ref, g_ref, wup_ref, cos_ref, sin_ref, ka_ref, va_ref):
    c = jnp.dot(h_ref[...], wc_ref[...], preferred_element_type=F32)
    ms = jnp.mean(c * c, axis=-1, keepdims=True)
    cn = (c * lax.rsqrt(ms + EPS) * g_ref[...]).astype(BF16)
    kv = jnp.dot(cn, wup_ref[...], preferred_element_type=F32)
    cos, sin = cos_ref[...], sin_ref[...]
    for hd in range(A_HEADS):
        sl = slice(hd * HEAD_DIM, (hd + 1) * HEAD_DIM)
        ka_ref[:, sl] = _rope_tile(kv[:, sl], cos, sin, HEAD_DIM // 2).astype(ka_ref.dtype)
    va_ref[...] = kv[:, A_WIDTH:].astype(va_ref.dtype)


def _proj_kv(h, wc, col0, g_kv, wup, cos, sin, tm):
    n, d = h.shape
    c0 = col0 // A_KV_RANK
    return pl.pallas_call(
        _kv_kernel,
        grid=(n // tm,),
        in_specs=[pl.BlockSpec((tm, d), lambda i: (i, 0)),
                  pl.BlockSpec((d, A_KV_RANK), lambda i: (0, c0)),
                  pl.BlockSpec((1, A_KV_RANK), lambda i: (0, 0)),
                  pl.BlockSpec((A_KV_RANK, 2 * A_WIDTH), lambda i: (0, 0)),
                  pl.BlockSpec((tm, LANES), lambda i: (i, 0)),
                  pl.BlockSpec((tm, LANES), lambda i: (i, 0))],
        out_specs=[pl.BlockSpec((tm, A_WIDTH), lambda i: (i, 0)), pl.BlockSpec((tm, A_WIDTH), lambda i: (i, 0))],
        out_shape=[jax.ShapeDtypeStruct((n, A_WIDTH), BF16), jax.ShapeDtypeStruct((n, A_WIDTH), BF16)],
        compiler_params=_cparams(("parallel",)),
        name="proj_kv",
    )(h, wc, g_kv.reshape(1, A_KV_RANK), wup, cos, sin)


def _small_kernel(h_ref, w_ref, g_ref, bf_ref, c0_ref, cos_ref, sin_ref, o_ref, carry_ref):
    tm = h_ref.shape[0]

    @pl.when(pl.program_id(0) == 0)
    def _():
        carry_ref[...] = c0_ref[...]

    y = jnp.dot(h_ref[...], w_ref[...], preferred_element_type=F32)
    lane = lax.broadcasted_iota(jnp.int32, y.shape, 1)
    is_ki = lane < SM_WI
    ms = jnp.sum(jnp.where(is_ki, y * y, 0.0), axis=-1, keepdims=True) * (1.0 / IDX_DIM)
    ki = _rope_tile(y * lax.rsqrt(ms + EPS) * g_ref[...], cos_ref[...], sin_ref[...], IDX_DIM // 2)
    wi = y * (IDX_HEADS ** -0.5 * IDX_DIM ** -0.5)
    z = y + bf_ref[...]
    logf = -(jnp.maximum(-z, 0.0) + jnp.log(1.0 + jnp.exp(-jnp.abs(z))))
    hi = logf.astype(BF16)
    r1 = logf - hi.astype(F32)
    mid = r1.astype(BF16)
    lo = (r1 - mid.astype(F32)).astype(BF16)
    row = lax.broadcasted_iota(jnp.int32, (tm, tm), 0)
    col = lax.broadcasted_iota(jnp.int32, (tm, tm), 1)
    tri = (col <= row).astype(BF16)
    csum = (jnp.dot(tri, hi, preferred_element_type=F32) + jnp.dot(tri, mid, preferred_element_type=F32)
            + jnp.dot(tri, lo, preferred_element_type=F32)) + carry_ref[0:1, :]
    carry_ref[...] = jnp.broadcast_to(csum[tm - 1:tm, :], carry_ref.shape)
    o_ref[...] = jnp.where(is_ki, ki, jnp.where(lane < SM_FB, wi, csum))


def _proj_small(h, w, col0, g_idx_pad, bf_pad, c0, cos, sin, tm):
    n, d = h.shape
    cb = col0 // LANES
    return pl.pallas_call(
        _small_kernel,
        grid=(n // tm,),
        in_specs=[pl.BlockSpec((tm, d), lambda i: (i, 0)),
                  pl.BlockSpec((d, LANES), lambda i: (0, cb)),
                  pl.BlockSpec((1, LANES), lambda i: (0, 0)),
                  pl.BlockSpec((1, LANES), lambda i: (0, 0)),
                  pl.BlockSpec((8, LANES), lambda i: (0, 0)),
                  pl.BlockSpec((tm, LANES), lambda i: (i, 0)),
                  pl.BlockSpec((tm, LANES), lambda i: (i, 0))],
        out_specs=pl.BlockSpec((tm, LANES), lambda i: (i, 0)),
        out_shape=jax.ShapeDtypeStruct((n, LANES), F32),
        scratch_shapes=[pltpu.VMEM((8, LANES), F32)],
        compiler_params=_cparams(("arbitrary",)),
        name="proj_small",
    )(h, w, g_idx_pad, bf_pad, c0, cos, sin)


COUNT_ROWS = 64
COUNT_UNROLL = 4


def _key_to_float(key):
    bits = key ^ ((key >> 31) & jnp.int32(0x7FFFFFFF))
    return lax.bitcast_convert_type(bits, F32)


def _select_kernel(qi_ref, kir_ref, kim_ref, wt_ref, or_ref, om_ref, s_ref, *, tq, topk):
    i = pl.program_id(0)
    n_real = or_ref.shape[1]
    n_chunks = n_real // tq
    krow = lax.broadcasted_iota(jnp.int32, (tq, tq), 0)
    qcol = lax.broadcasted_iota(jnp.int32, (tq, tq), 1)
    mrow = lax.broadcasted_iota(jnp.int32, (META_PAD, tq), 0)
    wt = wt_ref[...]

    def scores(kc):
        s = jnp.zeros((kc.shape[0], tq), F32)
        for h in range(IDX_HEADS):
            d = lax.dot_general(kc, qi_ref[h], NT_DIMS, preferred_element_type=F32)
            s = s + wt[h:h + 1, :] * jnp.maximum(d, 0.0)
        return s

    def causal(c):
        return krow + (c - i) * tq <= qcol

    def score_chunk(c, _):
        off = pl.multiple_of(c * tq, tq)
        s_ref[pl.ds(off, tq), :] = jnp.where(causal(c), scores(kir_ref[pl.ds(off, tq), :]), NEG)
        return 0

    lax.fori_loop(0, i + 1, score_chunk, 0)
    s_ref[n_real:, :] = jnp.where(mrow < N_META, scores(kim_ref[...]), NEG)

    def count(cmp, thr):
        def body(c, acc):
            off = c * tq
            for r in range(tq // COUNT_ROWS):
                blk = s_ref[pl.ds(pl.multiple_of(off + r * COUNT_ROWS, COUNT_ROWS), COUNT_ROWS), :]
                acc = acc + jnp.where(cmp(blk, thr), 1.0, 0.0)
            return acc

        groups = (i + 1) // COUNT_UNROLL

        def body_group(g, acc):
            for u in range(COUNT_UNROLL):
                acc = body(g * COUNT_UNROLL + u, acc)
            return acc

        acc = lax.fori_loop(0, groups, body_group, jnp.zeros((COUNT_ROWS, tq), F32))
        acc = lax.fori_loop(groups * COUNT_UNROLL, i + 1, body, acc)
        for r in range(META_PAD // COUNT_ROWS):
            blk = s_ref[n_real + r * COUNT_ROWS:n_real + (r + 1) * COUNT_ROWS, :]
            acc = acc + jnp.where(cmp(blk, thr), 1.0, 0.0)
        return jnp.sum(acc, axis=0, keepdims=True)

    kf = jnp.float32(topk)

    def bit_step(b, carry):
        t_key, cnt_t = carry
        cand = t_key + lax.shift_left(jnp.int32(1), 31 - b)
        cnt = count(lambda s, t: s >= t, _key_to_float(cand))
        take = cnt >= kf
        return jnp.where(take, cand, t_key), jnp.where(take, cnt, cnt_t)

    t0 = jnp.full((1, tq), jnp.iinfo(jnp.int32).min, jnp.int32)
    c0 = jnp.broadcast_to(((i + 1) * tq + META_PAD).astype(F32), (1, tq))
    t_key, cnt_t = lax.fori_loop(0, 32, bit_step, (t0, c0))
    thr = _key_to_float(t_key)

    def fill_chunk(c, _):
        or_ref[:, pl.ds(pl.multiple_of(c * tq, tq), tq)] = jnp.full((tq, tq), NEG, or_ref.dtype)
        return 0

    lax.fori_loop(i + 1, n_chunks, fill_chunk, 0)

    ambiguous = jnp.logical_and(cnt_t > kf, thr > NEG)
    any_amb = jnp.max(jnp.where(ambiguous, 1.0, 0.0)) > 0.0

    @pl.when(jnp.logical_not(any_amb))
    def _():
        def out_chunk(c, _):
            off = pl.multiple_of(c * tq, tq)
            sel = jnp.where(s_ref[pl.ds(off, tq), :] >= thr, 0.0, NEG)
            or_ref[:, pl.ds(off, tq)] = jnp.where(causal(c), sel, NEG).T.astype(or_ref.dtype)
            return 0

        lax.fori_loop(0, i + 1, out_chunk, 0)
        sel = jnp.where(s_ref[n_real:, :] >= thr, 0.0, NEG)
        om_ref[...] = jnp.where(mrow < N_META, sel, NEG).T.astype(om_ref.dtype)

    @pl.when(any_amb)
    def _():
        cnt_gt = count(lambda s, t: s > t, thr)
        quota = jnp.where(ambiguous, kf - cnt_gt, jnp.float32(3e38))

        def tie_select(s, valid, running):
            rows = s.shape[0]
            r = lax.broadcasted_iota(jnp.int32, (rows, rows), 0)
            c_ = lax.broadcasted_iota(jnp.int32, (rows, rows), 1)
            lower = (c_ <= r).astype(BF16)
            eq = jnp.where(s == thr, 1.0, 0.0)
            rank = jnp.dot(lower, eq.astype(BF16), preferred_element_type=F32) + running
            keep = jnp.where(s > thr, 0.0, jnp.where(jnp.logical_and(eq > 0.0, rank <= quota), 0.0, NEG))
            return jnp.where(valid, keep, NEG), running + jnp.sum(eq, axis=0, keepdims=True)

        bias_m, running = tie_select(s_ref[n_real:, :], mrow < N_META, jnp.zeros((1, tq), F32))
        om_ref[...] = bias_m.T.astype(om_ref.dtype)

        def out_chunk(c, running):
            off = pl.multiple_of(c * tq, tq)
            bias, running = tie_select(s_ref[pl.ds(off, tq), :], causal(c), running)
            or_ref[:, pl.ds(off, tq)] = bias.T.astype(or_ref.dtype)
            return running

        lax.fori_loop(0, i + 1, out_chunk, running)


def _select_bias(qi_hm, ki_real, ki_meta, wt, topk, tq):
    _, t, _ = qi_hm.shape
    return pl.pallas_call(
        functools.partial(_select_kernel, tq=tq, topk=topk),
        grid=(t // tq,),
        in_specs=[pl.BlockSpec((IDX_HEADS, tq, IDX_DIM), lambda i: (0, i, 0)),
                  pl.BlockSpec((t, IDX_DIM), lambda i: (0, 0)),
                  pl.BlockSpec((META_PAD, IDX_DIM), lambda i: (0, 0)),
                  pl.BlockSpec((IDX_HEADS, tq), lambda i: (0, i))],
        out_specs=[pl.BlockSpec((tq, t), lambda i: (i, 0)), pl.BlockSpec((tq, META_PAD), lambda i: (i, 0))],
        out_shape=[jax.ShapeDtypeStruct((t, t), BF16), jax.ShapeDtypeStruct((t, META_PAD), BF16)],
        scratch_shapes=[pltpu.VMEM((t + META_PAD, tq), F32)],
        compiler_params=_cparams(("parallel",)),
        name="select_bias",
    )(qi_hm, ki_real, ki_meta, wt)


def _attn_kernel(*refs, t, fox):
    if fox:
        (qi_ref, kj_ref, q_ref, k_ref, v_ref, km_ref, vm_ref, cq_ref, ck_ref, ckm_ref,
         o_ref, m_ref, l_ref, acc_ref, r_ref) = refs
        s_ref = None
    else:
        (qi_ref, kj_ref, q_ref, k_ref, v_ref, km_ref, vm_ref, b_ref, bm_ref,
         o_ref, m_ref, l_ref, acc_ref, s_ref) = refs
    stage_logits = s_ref is not None
    step = pl.program_id(0)
    i = qi_ref[step]
    j = kj_ref[step]
    nheads = q_ref.shape[1] // HEAD_DIM

    def block(k_blk_ref, v_blk_ref, bias_fn):
        width = k_blk_ref.shape[0]

        def logits(h):
            sl = slice(h * HEAD_DIM, (h + 1) * HEAD_DIM)
            return lax.dot_general(q_ref[:, sl], k_blk_ref[:, sl], NT_DIMS, preferred_element_type=F32)

        if stage_logits:
            for h in range(nheads):
                s_ref[h, :, :width] = logits(h)
        s_next = None if stage_logits else logits(0)
        for h in range(nheads):
            sl = slice(h * HEAD_DIM, (h + 1) * HEAD_DIM)
            if stage_logits:
                s = s_ref[h, :, :width]
            else:
                s, s_next = s_next, (logits(h + 1) if h + 1 < nheads else None)
            u = bias_fn(s, h)
            m_prev = m_ref[h]
            mx = jnp.max(u, axis=1, keepdims=True)
            if fox:
                r = r_ref[h]
                m_next = jnp.maximum(m_prev, mx + r)
                sub = m_next - r
            else:
                m_next = jnp.maximum(m_prev, mx)
                sub = m_next
            alpha = jnp.exp2(m_prev - m_next)
            ps = [jnp.exp2(u[:, c * LANES:(c + 1) * LANES] - sub) for c in range(width // LANES)]
            psum = ps[0]
            for pc in ps[1:]:
                psum = psum + pc
            p = ps[0] if len(ps) == 1 else jnp.concatenate(ps, axis=1)
            l_ref[h] = alpha * l_ref[h] + psum
            m_ref[h] = m_next
            acc_ref[h] = alpha * acc_ref[h] + jnp.dot(p.astype(BF16), v_blk_ref[:, sl],
                                                      preferred_element_type=F32)

    @pl.when(j == 0)
    def _():
        m_ref[...] = jnp.full(m_ref.shape, NEG, F32)
        l_ref[...] = jnp.zeros(l_ref.shape, F32)
        acc_ref[...] = jnp.zeros(acc_ref.shape, F32)
        mcol = lax.broadcasted_iota(jnp.int32, (t, META_PAD), 1)
        if fox:
            for h in range(nheads):
                r_ref[h] = jnp.broadcast_to(cq_ref[:, h:h + 1] * LOG2E, (t, LANES))

            def bias_fn(u, h):
                return jnp.where(mcol < N_META, u - ckm_ref[h:h + 1, :] * LOG2E, NEG)
        else:
            bias_m = bm_ref[...].astype(F32)

            def bias_fn(u, h):
                return u + bias_m
        block(km_ref, vm_ref, bias_fn)

    if fox:
        @pl.when(j < i)
        def _():
            block(k_ref, v_ref, lambda u, h: u - ck_ref[h:h + 1, :] * LOG2E)

        @pl.when(j == i)
        def _():
            row = lax.broadcasted_iota(jnp.int32, (t, t), 0)
            col = lax.broadcasted_iota(jnp.int32, (t, t), 1)
            block(k_ref, v_ref, lambda u, h: jnp.where(col <= row, u - ck_ref[h:h + 1, :] * LOG2E, NEG))
    else:
        bias = b_ref[...].astype(F32)
        block(k_ref, v_ref, lambda u, h: u + bias)

    @pl.when(j == i)
    def _():
        for h in range(nheads):
            sl = slice(h * HEAD_DIM, (h + 1) * HEAD_DIM)
            o_ref[:, sl] = (acc_ref[h] / jnp.sum(l_ref[h], axis=1, keepdims=True)).astype(o_ref.dtype)


def _attention(qkv, meta_kv, extra, *, fox, t_blk):
    q_arr, q_col, k_arr, k_col, v_arr, v_col = qkv
    km_arr, km_col, vm_arr, vm_col = meta_kv
    t = q_arr.shape[0]
    w = A_WIDTH
    nheads = w // HEAD_DIM
    nq = t // t_blk
    qi = jnp.asarray([i for i in range(nq) for _ in range(i + 1)], jnp.int32)
    kj = jnp.asarray([j for i in range(nq) for j in range(i + 1)], jnp.int32)

    in_specs = [pl.BlockSpec((t_blk, w), lambda s, qi, kj: (qi[s], q_col)),
                pl.BlockSpec((t_blk, w), lambda s, qi, kj: (kj[s], k_col)),
                pl.BlockSpec((t_blk, w), lambda s, qi, kj: (kj[s], v_col)),
                pl.BlockSpec((META_PAD, w), lambda s, qi, kj: (0, km_col)),
                pl.BlockSpec((META_PAD, w), lambda s, qi, kj: (0, vm_col))]
    scratch = [pltpu.VMEM((nheads, t_blk, LANES), F32), pltpu.VMEM((nheads, t_blk, LANES), F32),
               pltpu.VMEM((nheads, t_blk, HEAD_DIM), F32)]
    args = (q_arr, k_arr, v_arr, km_arr, vm_arr)
    if fox:
        cq, ct_real, ct_meta = extra
        in_specs += [pl.BlockSpec((t_blk, nheads), lambda s, qi, kj: (qi[s], 0)),
                     pl.BlockSpec((nheads, t_blk), lambda s, qi, kj: (0, kj[s])),
                     pl.BlockSpec((nheads, META_PAD), lambda s, qi, kj: (0, 0))]
        args += (cq, ct_real, ct_meta)
        scratch.append(pltpu.VMEM((nheads, t_blk, LANES), F32))
    else:
        bias_real, bias_meta = extra
        in_specs += [pl.BlockSpec((t_blk, t_blk), lambda s, qi, kj: (qi[s], kj[s])),
                     pl.BlockSpec((t_blk, META_PAD), lambda s, qi, kj: (qi[s], 0))]
        args += (bias_real, bias_meta)
        scratch.append(pltpu.VMEM((nheads, t_blk, t_blk), F32))
    return pl.pallas_call(
        functools.partial(_attn_kernel, t=t_blk, fox=fox),
        grid_spec=pltpu.PrefetchScalarGridSpec(
            num_scalar_prefetch=2,
            grid=(int(qi.shape[0]),),
            in_specs=in_specs,
            out_specs=pl.BlockSpec((t_blk, w), lambda s, qi, kj: (qi[s], 0)),
            scratch_shapes=scratch),
        out_shape=jax.ShapeDtypeStruct((t, w), BF16),
        compiler_params=_cparams(("arbitrary",)),
        name="attn_fox" if fox else "attn_dsa",
    )(qi, kj, *args)


def _route(logits):
    lane = lax.broadcasted_iota(jnp.int32, logits.shape, 1).astype(F32)
    big = jnp.float32(LANES)
    ninf = -jnp.inf
    gl = jnp.where(lane < N_GROUPS, logits, ninf)
    gmax = jnp.max(gl, axis=1, keepdims=True)
    g_sel = jnp.min(jnp.where(gl == gmax, lane, big), axis=1, keepdims=True)
    p_group = 1.0 / jnp.sum(jnp.exp(gl - gmax), axis=1, keepdims=True)
    e_lo = RT_E0 + g_sel * EXPERTS_PER_GROUP
    in_grp = jnp.logical_and(lane >= e_lo, lane < e_lo + EXPERTS_PER_GROUP)
    el = jnp.where(in_grp, logits, ninf)
    emax = jnp.max(el, axis=1, keepdims=True)
    ex = jnp.exp(el - emax)
    prob = ex / jnp.sum(ex, axis=1, keepdims=True)
    prob = jnp.where(in_grp, prob, -1.0)
    p1 = jnp.max(prob, axis=1, keepdims=True)
    i1 = jnp.min(jnp.where(prob == p1, lane, big), axis=1, keepdims=True)
    rest = jnp.where(lane == i1, -1.0, prob)
    p2 = jnp.max(rest, axis=1, keepdims=True)
    i2 = jnp.min(jnp.where(rest == p2, lane, big), axis=1, keepdims=True)
    tot = p1 + p2
    w1 = p_group * (p1 / tot)
    w2 = p_group * (p2 / tot)
    return jnp.where(lane == i1, w1, jnp.where(lane == i2, w2, 0.0)), g_sel


def _merge_kernel(h_ref, ya_ref, yb_ref, wga_ref, wgb_ref, wa_ref, wb_ref, wo_ref, x_ref, g_ref,
                  wr_ref, br_ref, h1_ref, xr_ref, plan_ref, acc_ref, cnt_ref):
    n = pl.program_id(1)

    @pl.when(jnp.logical_and(pl.program_id(0) == 0, n == 0))
    def _():
        cnt_ref[...] = jnp.zeros(cnt_ref.shape, F32)

    @pl.when(n == 0)
    def _():
        acc_ref[...] = jnp.zeros(acc_ref.shape, F32)

    h = h_ref[...]
    ga = jnp.dot(h, wga_ref[...], preferred_element_type=F32)
    gb = jnp.dot(h, wgb_ref[...], preferred_element_type=F32)
    pa = jnp.dot(ya_ref[...], wa_ref[...], preferred_element_type=F32)
    pb = jnp.dot(yb_ref[...], wb_ref[...], preferred_element_type=F32)
    merged = jax.nn.sigmoid(ga) * pa + jax.nn.sigmoid(gb) * pb
    acc_ref[...] += jnp.dot(merged.astype(BF16), wo_ref[...], preferred_element_type=F32)

    @pl.when(n == pl.num_programs(1) - 1)
    def _():
        h1 = x_ref[...] + acc_ref[...]
        h1_ref[...] = h1
        ms = jnp.mean(h1 * h1, axis=-1, keepdims=True)
        t = h1 * lax.rsqrt(ms + EPS) * g_ref[...]
        t_hi = t.astype(BF16)
        t_lo = (t - t_hi.astype(F32)).astype(BF16)
        wr = wr_ref[...]
        y = jnp.dot(t_hi, wr, preferred_element_type=F32)
        logits = (y[:, :LANES] + y[:, LANES:]
                  + jnp.dot(t_lo, wr[:, :LANES], preferred_element_type=F32)) + br_ref[...]
        gates, g_sel = _route(logits)
        d = t.shape[1]
        xr_ref[:, :d] = t
        xr_ref[:, d:] = gates
        tm = t.shape[0]
        lane = lax.broadcasted_iota(jnp.int32, gates.shape, 1).astype(F32)
        onehot = jnp.where(lane == g_sel, 1.0, 0.0).astype(BF16)
        rr = lax.broadcasted_iota(jnp.int32, (tm, tm), 0)
        cc = lax.broadcasted_iota(jnp.int32, (tm, tm), 1)
        cum = jnp.dot((cc <= rr).astype(BF16), onehot, preferred_element_type=F32) + cnt_ref[0:1, :]
        cnt_ref[...] = jnp.broadcast_to(cum[tm - 1:tm, :], cnt_ref.shape)
        plan_ref[...] = jnp.where(lane < N_GROUPS, cum, jnp.where(lane == N_GROUPS, g_sel, 0.0))


def _merge(h, ya, yb, w_all, ga_col0, gb_col0, wa, wb, wo, x, g_ffn, wr_split, br, tm, tn):
    t, d = h.shape
    row = lambda i, n: (i, 0)
    ga0, gb0 = ga_col0 // tn, gb_col0 // tn
    wga = wgb = w_all
    return pl.pallas_call(
        _merge_kernel,
        grid=(t // tm, d // tn),
        in_specs=[pl.BlockSpec((tm, d), row),
                  pl.BlockSpec((tm, A_WIDTH), row),
                  pl.BlockSpec((tm, B_WIDTH), row),
                  pl.BlockSpec((d, tn), lambda i, n: (0, ga0 + n)),
                  pl.BlockSpec((d, tn), lambda i, n: (0, gb0 + n)),
                  pl.BlockSpec((A_WIDTH, tn), lambda i, n: (0, n)),
                  pl.BlockSpec((B_WIDTH, tn), lambda i, n: (0, n)),
                  pl.BlockSpec((tn, d), lambda i, n: (n, 0)),
                  pl.BlockSpec((tm, d), row),
                  pl.BlockSpec((1, d), lambda i, n: (0, 0)),
                  pl.BlockSpec((d, 2 * LANES), lambda i, n: (0, 0)),
                  pl.BlockSpec((1, LANES), lambda i, n: (0, 0))],
        out_specs=[pl.BlockSpec((tm, d), row), pl.BlockSpec((tm, d + LANES), row),
                   pl.BlockSpec((tm, LANES), row)],
        out_shape=[jax.ShapeDtypeStruct((t, d), F32), jax.ShapeDtypeStruct((t, d + LANES), F32),
                   jax.ShapeDtypeStruct((t, LANES), F32)],
        scratch_shapes=[pltpu.VMEM((tm, d), F32), pltpu.VMEM((8, LANES), F32)],
        compiler_params=_cparams(("arbitrary", "arbitrary")),
        name="merge_out_router",
    )(h, ya, yb, wga, wgb, wa, wb, wo, x, g_ffn.reshape(1, d), wr_split, br)


DMA_LOOP_UNROLL = 8


def _row_copy(src_ref, src_row, dst_ref, dst_row, sem):
    return pltpu.make_async_copy(src_ref.at[pl.ds(src_row, 1), :], dst_ref.at[pl.ds(dst_row, 1), :], sem)


def _dispatch_kernel(pos_ref, x_ref, init_ref, xs_ref, sem):
    del init_ref
    i = pl.program_id(0)
    tm = x_ref.shape[0]

    def issue(r, _):
        _row_copy(x_ref, r, xs_ref, pos_ref[i * tm + r], sem).start()
        return 0

    lax.fori_loop(0, tm, issue, 0, unroll=DMA_LOOP_UNROLL)

    def drain(r, _):
        _row_copy(x_ref, r, xs_ref, pos_ref[i * tm + r], sem).wait()
        return 0

    lax.fori_loop(0, tm, drain, 0, unroll=DMA_LOOP_UNROLL)


def _dispatch(pos, xr, n_slots, tm):
    t, w = xr.shape
    return pl.pallas_call(
        _dispatch_kernel,
        grid_spec=pltpu.PrefetchScalarGridSpec(
            num_scalar_prefetch=1,
            grid=(t // tm,),
            in_specs=[pl.BlockSpec((tm, w), lambda i, pos: (i, 0)),
                      pl.BlockSpec(memory_space=pl.ANY)],
            out_specs=pl.BlockSpec(memory_space=pl.ANY),
            scratch_shapes=[pltpu.SemaphoreType.DMA]),
        out_shape=jax.ShapeDtypeStruct((n_slots, w), xr.dtype),
        input_output_aliases={2: 0},
        compiler_params=_cparams(("arbitrary",)),
        name="moe_dispatch",
    )(pos, xr, jnp.zeros((n_slots, w), xr.dtype))


def _group_ffn_kernel(tg_ref, tv_ref, x_ref, wg_ref, wu_ref, wd_ref, o_ref, acc_ref, xb_ref):
    j = pl.program_id(0)
    k = pl.program_id(1)
    d = o_ref.shape[1]

    @pl.when(k == 0)
    def _():
        acc_ref[...] = jnp.zeros(acc_ref.shape, F32)
        xb_ref[...] = x_ref[:, :d].astype(BF16)

    @pl.when(tv_ref[j] > 0)
    def _():
        gates = x_ref[:, d:]
        lane = lax.broadcasted_iota(jnp.int32, gates.shape, 1)
        e_lane = RT_E0 + tg_ref[j] * EXPERTS_PER_GROUP + k
        g_e = jnp.sum(jnp.where(lane == e_lane, gates, 0.0), axis=1, keepdims=True)
        xb = xb_ref[...]
        hg = jnp.dot(xb, wg_ref[0].astype(BF16), preferred_element_type=F32)
        hu = jnp.dot(xb, wu_ref[0].astype(BF16), preferred_element_type=F32)
        act = jax.nn.silu(hg) * hu * g_e
        acc_ref[...] += jnp.dot(act.astype(BF16), wd_ref[0].astype(BF16), preferred_element_type=F32)

    @pl.when(k == pl.num_programs(1) - 1)
    def _():
        o_ref[...] = acc_ref[...]


def _group_ffn(tile_group, tile_valid, xs, wg, wu, wd, ts):
    n_slots, w = xs.shape
    d = w - LANES
    f = wg.shape[2]
    expert = lambda j, k, tg, tv: (tg[j] * EXPERTS_PER_GROUP + k, 0, 0)
    return pl.pallas_call(
        _group_ffn_kernel,
        grid_spec=pltpu.PrefetchScalarGridSpec(
            num_scalar_prefetch=2,
            grid=(n_slots // ts, EXPERTS_PER_GROUP),
            in_specs=[pl.BlockSpec((ts, w), lambda j, k, tg, tv: (j, 0)),
                      pl.BlockSpec((1, d, f), expert),
                      pl.BlockSpec((1, d, f), expert),
                      pl.BlockSpec((1, f, d), expert)],
            out_specs=pl.BlockSpec((ts, d), lambda j, k, tg, tv: (j, 0)),
            scratch_shapes=[pltpu.VMEM((ts, d), F32), pltpu.VMEM((ts, d), BF16)]),
        out_shape=jax.ShapeDtypeStruct((n_slots, d), F32),
        compiler_params=_cparams(("arbitrary", "arbitrary")),
        name="moe_group_ffn",
    )(tile_group, tile_valid, xs, wg, wu, wd)


def _combine_kernel(pos_ref, ys_ref, h1_ref, gf_ref, o_ref, ybuf_ref, sem):
    i = pl.program_id(0)
    tm = h1_ref.shape[0]

    def issue(r, _):
        _row_copy(ys_ref, pos_ref[i * tm + r], ybuf_ref, r, sem).start()
        return 0

    lax.fori_loop(0, tm, issue, 0, unroll=DMA_LOOP_UNROLL)

    def drain(r, _):
        _row_copy(ys_ref, pos_ref[i * tm + r], ybuf_ref, r, sem).wait()
        return 0

    lax.fori_loop(0, tm, drain, 0, unroll=DMA_LOOP_UNROLL)
    h2 = h1_ref[...] + ybuf_ref[...]
    ms = jnp.mean(h2 * h2, axis=-1, keepdims=True)
    o_ref[...] = h2 * lax.rsqrt(ms + EPS) * gf_ref[...]


def _combine(pos, ys, h1, g_final, tm):
    t, d = h1.shape
    return pl.pallas_call(
        _combine_kernel,
        grid_spec=pltpu.PrefetchScalarGridSpec(
            num_scalar_prefetch=1,
            grid=(t // tm,),
            in_specs=[pl.BlockSpec(memory_space=pl.ANY),
                      pl.BlockSpec((tm, d), lambda i, pos: (i, 0)),
                      pl.BlockSpec((1, d), lambda i, pos: (0, 0))],
            out_specs=pl.BlockSpec((tm, d), lambda i, pos: (i, 0)),
            scratch_shapes=[pltpu.VMEM((tm, d), F32), pltpu.SemaphoreType.DMA]),
        out_shape=jax.ShapeDtypeStruct((t, d), F32),
        compiler_params=_cparams(("arbitrary",)),
        name="moe_combine_final",
    )(pos, ys, h1, g_final.reshape(1, d))


def _group_plan(plan, ts):
    t = plan.shape[0]
    g_sel = plan[:, N_GROUPS].astype(jnp.int32)
    cum_sel = jnp.take_along_axis(plan[:, :N_GROUPS], g_sel[:, None], axis=1)[:, 0]
    counts = plan[t - 1, :N_GROUPS].astype(jnp.int32)
    padded = (counts + ts - 1) // ts * ts
    ends = jnp.cumsum(padded)
    starts = ends - padded
    pos = starts[g_sel] + cum_sel.astype(jnp.int32) - 1
    n_tiles = (t + N_GROUPS * (ts - 1)) // ts
    tile_start = jnp.arange(n_tiles, dtype=jnp.int32) * ts
    tile_group = jnp.minimum(jnp.sum(tile_start[:, None] >= ends[None, :], axis=1), N_GROUPS - 1)
    tile_valid = (tile_start < ends[N_GROUPS - 1]).astype(jnp.int32)
    return pos, tile_group.astype(jnp.int32), tile_valid, n_tiles * ts


def _pick(n, pref):
    return pref if n % pref == 0 else n


def _pad_rows(a, rows):
    return jnp.pad(a, ((0, rows - a.shape[0]), (0, 0)))


def _layer(x, meta, g_mix, w_in, g_kv, w_kv_up, g_idx_k, b_f, w_branch_a, w_branch_b, w_out,
           g_ffn, w_group, b_group, w_expert, b_expert, w_gate_e, w_up_e, w_down_e, g_final):
    t, d = x.shape
    n_all = t + N_META
    topk = min(TOPK_MAX, n_all // 4)

    o = [0]
    for wdt in (A_WIDTH, A_KV_RANK, IDX_HEADS * IDX_DIM, IDX_DIM, IDX_HEADS, B_WIDTH, B_WIDTH, B_WIDTH,
                B_HEADS, D_MODEL, D_MODEL):
        o.append(o[-1] + wdt)
    w_in_t = w_in.T
    w_all = _pack_w_in(w_in_t, [(o[0], o[1]), (o[2], o[3]), (o[5], o[8]), (o[9], o[10]), (o[10], o[11]),
                                (o[1], o[2])])
    w_small = _pack_small(w_in_t, (o[3], o[5]), (o[8], o[9]))
    col_qa = 0
    col_qi = col_qa + A_WIDTH
    col_b = col_qi + IDX_HEADS * IDX_DIM
    col_ga = col_b + 3 * B_WIDTH
    col_gb = col_ga + D_MODEL
    col_ckv = col_gb + D_MODEL
    w_up_bf = w_kv_up.astype(BF16)
    g_idx_pad = jnp.zeros((1, LANES), F32).at[0, SM_KI:SM_KI + IDX_DIM].set(g_idx_k)
    bf_pad = jnp.zeros((1, LANES), F32).at[0, SM_FB:SM_FB + B_HEADS].set(b_f)

    pos_meta = jnp.arange(N_META)
    pos_real = jnp.arange(N_META, n_all)
    cos64_m, sin64_m = _rope_tables(pos_meta, HEAD_DIM // 2)
    cos64_r, sin64_r = _rope_tables(pos_real, HEAD_DIM // 2)
    cos32_m, sin32_m = _rope_tables(pos_meta, IDX_DIM // 2)
    cos32_r, sin32_r = _rope_tables(pos_real, IDX_DIM // 2)

    tm = _pick(t, 512)

    hm = _rmsnorm(meta, g_mix, N_META)
    ka_m, va_m = _proj_kv(hm, w_all, col_ckv, g_kv, w_up_bf, cos64_m, sin64_m, N_META)
    small_m = _proj_small(hm, w_small, 0, g_idx_pad, bf_pad, jnp.zeros((8, LANES), F32), cos32_m, sin32_m,
                          N_META)
    b_m = _proj_plain(hm, w_all, col_b, 3 * B_WIDTH, N_META, B_WIDTH)
    c_carry = jnp.broadcast_to(small_m[N_META - 1:N_META, :], (8, LANES))

    h = _rmsnorm(x, g_mix, tm)
    qk_scale = HEAD_DIM ** -0.5 * LOG2E
    tmw = _pick(t, 1024)
    qa = _proj_rope(h, w_all, col_qa, A_WIDTH, cos64_r, sin64_r, HEAD_DIM // 2, tmw, A_WIDTH, out_scale=qk_scale)
    ka_r, va_r = _proj_kv(h, w_all, col_ckv, g_kv, w_up_bf, cos64_r, sin64_r, tm)
    qi_hm = _proj_rope(h, w_all, col_qi, IDX_HEADS * IDX_DIM, cos32_r, sin32_r, IDX_DIM // 2, tmw,
                       IDX_HEADS * IDX_DIM, head_major=True)
    small_r = _proj_small(h, w_small, 0, g_idx_pad, bf_pad, c_carry, cos32_r, sin32_r, tm)
    b_r = _proj_plain(h, w_all, col_b, 3 * B_WIDTH, tmw, B_WIDTH, first_tile_scale=qk_scale)

    ki_real = small_r[:, SM_KI:SM_KI + IDX_DIM].astype(BF16)
    ki_meta = _pad_rows(small_m[:, SM_KI:SM_KI + IDX_DIM], META_PAD).astype(BF16)
    wt = small_r[:, SM_WI:SM_WI + IDX_HEADS].T
    cq = small_r[:, SM_FB:SM_FB + B_HEADS]
    ct_meta = _pad_rows(small_m[:, SM_FB:SM_FB + B_HEADS], META_PAD).T

    bias_real, bias_meta = _select_bias(qi_hm, ki_real, ki_meta, wt, topk, _pick(t, 256))
    ya = _attention((qa, 0, ka_r, 0, va_r, 0),
                    (_pad_rows(ka_m, META_PAD), 0, _pad_rows(va_m, META_PAD), 0),
                    (bias_real, bias_meta), fox=False, t_blk=_pick(t, 512))
    b_m_pad = _pad_rows(b_m, META_PAD)
    yb = _attention((b_r, 0, b_r, 1, b_r, 2), (b_m_pad, 1, b_m_pad, 2),
                    (cq, cq.T, ct_meta), fox=True, t_blk=_pick(t, 512))

    w_r = jnp.concatenate([w_group, w_expert, jnp.zeros((d, LANES - N_GROUPS - N_EXPERTS), F32)], axis=1)
    wr_hi = w_r.astype(BF16)
    wr_split = jnp.concatenate([wr_hi, (w_r - wr_hi.astype(F32)).astype(BF16)], axis=1)
    b_r_pad = jnp.concatenate([b_group, b_expert, jnp.zeros((LANES - N_GROUPS - N_EXPERTS,), F32)]).reshape(1, LANES)
    h1, xr, plan = _merge(h, ya, yb, w_all, col_ga, col_gb, w_branch_a.astype(BF16), w_branch_b.astype(BF16),
                          w_out.astype(BF16), x, g_ffn, wr_split, b_r_pad, tm, 256)

    ts = _pick(t, 512)
    pos, tile_group, tile_valid, n_slots = _group_plan(plan, ts)
    xs = _dispatch(pos, xr, n_slots, _pick(t, 256))
    ys = _group_ffn(tile_group, tile_valid, xs, w_gate_e, w_up_e, w_down_e.astype(BF16), ts)
    return _combine(pos, ys, h1, g_final, _pick(t, 256))


def kernel(x, meta_tokens, g_mix, w_in, g_kv, w_kv_up, g_idx_k, b_f, w_branch_a, w_branch_b, w_out, g_ffn,
           w_group, b_group, w_expert, b_expert, w_gate_e, w_up_e, w_down_e, g_final):
    assert x.shape[0] == 1 and g_mix.shape[0] == 1, "single batch, single layer"
    out = _layer(x[0], meta_tokens.astype(x.dtype), g_mix[0], w_in[0], g_kv[0], w_kv_up[0], g_idx_k[0], b_f[0],
                 w_branch_a[0], w_branch_b[0], w_out[0], g_ffn[0], w_group[0], b_group[0], w_expert[0],
                 b_expert[0], w_gate_e[0], w_up_e[0], w_down_e[0], g_final)
    return out[None]
```

```python
import functools

import jax
import jax.numpy as jnp
from jax import lax
from jax.experimental import pallas as pl
from jax.experimental.pallas import tpu as pltpu

F32 = jnp.float32
BF16 = jnp.bfloat16

D_MODEL = 2048
N_META = 16
HEAD_DIM = 128
ROPE_THETA = 10000.0
EPS = 1e-6
NEG = -1e30
A_HEADS = 8
A_WIDTH = A_HEADS * HEAD_DIM
A_KV_RANK = 512
IDX_HEADS = 16
IDX_DIM = 64
TOPK_MAX = 256
B_HEADS = 8
B_WIDTH = B_HEADS * HEAD_DIM
N_GROUPS = 4
EXPERTS_PER_GROUP = 4
N_EXPERTS = N_GROUPS * EXPERTS_PER_GROUP
D_FF_EXPERT = 512

LANES = 128
META_PAD = LANES
VMEM_LIMIT = 56 * 1024 * 1024
LOG2E = 1.4426950408889634
NT_DIMS = (((1,), (1,)), ((), ()))

SM_KI = 0
SM_WI = IDX_DIM
SM_FB = IDX_DIM + IDX_HEADS
RT_E0 = N_GROUPS


def _cparams(sem):
    return pltpu.CompilerParams(dimension_semantics=sem, vmem_limit_bytes=VMEM_LIMIT)


def _rms_kernel(x_ref, g_ref, o_ref):
    x = x_ref[...]
    ms = jnp.mean(x * x, axis=-1, keepdims=True)
    o_ref[...] = (x * lax.rsqrt(ms + EPS) * g_ref[...]).astype(o_ref.dtype)


def _rmsnorm(x, g, tm):
    n, d = x.shape
    return pl.pallas_call(
        _rms_kernel,
        grid=(n // tm,),
        in_specs=[pl.BlockSpec((tm, d), lambda i: (i, 0)), pl.BlockSpec((1, d), lambda i: (0, 0))],
        out_specs=pl.BlockSpec((tm, d), lambda i: (i, 0)),
        out_shape=jax.ShapeDtypeStruct((n, d), BF16),
        compiler_params=_cparams(("parallel",)),
        name="rmsnorm",
    )(x, g.reshape(1, d))


def _rope_tile(y, cos, sin_signed, half):
    if 2 * half == LANES:
        partner = pltpu.roll(y, half, axis=1)
    else:
        lane = lax.broadcasted_iota(jnp.int32, y.shape, 1)
        first = (lane % (2 * half)) < half
        partner = jnp.where(first, pltpu.roll(y, LANES - half, axis=1), pltpu.roll(y, half, axis=1))
    return y * cos + partner * sin_signed


def _rope_tables(pos, half):
    inv_freq = ROPE_THETA ** (-jnp.arange(half, dtype=F32) / half)
    ang = pos.astype(F32)[:, None] * inv_freq[None, :]
    cos, sin = jnp.cos(ang), jnp.sin(ang)
    reps = LANES // (2 * half)
    cos_t = jnp.tile(jnp.concatenate([cos, cos], axis=1), (1, reps))
    sin_t = jnp.tile(jnp.concatenate([-sin, sin], axis=1), (1, reps))
    return cos_t, sin_t


PACK_ROWS = 256


def _pack_kernel(src_ref, x_ref, o_ref):
    del src_ref
    o_ref[...] = x_ref[...].T.astype(o_ref.dtype)


def _pack_w_in(wt, sections):
    _, d = wt.shape
    starts = []
    for lo, hi in sections:
        assert lo % 8 == 0 and (hi - lo) % PACK_ROWS == 0
        starts += list(range(lo, hi, PACK_ROWS))
    src = jnp.asarray([s // 8 for s in starts], jnp.int32)
    return pl.pallas_call(
        _pack_kernel,
        grid_spec=pltpu.PrefetchScalarGridSpec(
            num_scalar_prefetch=1,
            grid=(len(starts),),
            in_specs=[pl.BlockSpec((pl.Element(PACK_ROWS), pl.Element(d)), lambda j, src: (src[j] * 8, 0))],
            out_specs=pl.BlockSpec((d, PACK_ROWS), lambda j, src: (0, j))),
        out_shape=jax.ShapeDtypeStruct((d, len(starts) * PACK_ROWS), BF16),
        compiler_params=_cparams(("arbitrary",)),
        name="pack_w_in",
    )(src, wt)


def _pack_small_kernel(a_ref, b_ref, o_ref, *, n_a, n_b):
    lane = lax.broadcasted_iota(jnp.int32, o_ref.shape, 1)
    o_ref[...] = jnp.where(lane < n_a, a_ref[...].T,
                           jnp.where(lane < n_a + n_b, b_ref[...].T, 0.0)).astype(o_ref.dtype)


def _pack_small(wt, rows_a, rows_b):
    _, d = wt.shape
    n_a, n_b = rows_a[1] - rows_a[0], rows_b[1] - rows_b[0]
    start_a, start_b = rows_a[0], rows_b[0] - n_a
    assert start_a % 8 == 0 and start_b % 8 == 0 and start_b >= 0 and n_a + n_b <= LANES
    window = lambda start: pl.BlockSpec((pl.Element(LANES), pl.Element(d)), lambda i: (start, 0))
    return pl.pallas_call(
        functools.partial(_pack_small_kernel, n_a=n_a, n_b=n_b),
        grid=(1,),
        in_specs=[window(start_a), window(start_b)],
        out_specs=pl.BlockSpec((d, LANES), lambda i: (0, 0)),
        out_shape=jax.ShapeDtypeStruct((d, LANES), BF16),
        compiler_params=_cparams(("arbitrary",)),
        name="pack_w_small",
    )(wt, wt)


def _proj_plain_kernel(h_ref, w_ref, o_ref, *, first_tile_scale):
    y = jnp.dot(h_ref[...], w_ref[...], preferred_element_type=F32)
    scale = jnp.where(pl.program_id(1) == 0, jnp.float32(first_tile_scale), jnp.float32(1.0))
    o_ref[...] = (y * scale).astype(o_ref.dtype)


def _proj_plain(h, w, col0, nc, tm, tn, first_tile_scale=1.0):
    n, d = h.shape
    c0 = col0 // tn
    return pl.pallas_call(
        functools.partial(_proj_plain_kernel, first_tile_scale=first_tile_scale),
        grid=(n // tm, nc // tn),
        in_specs=[pl.BlockSpec((tm, d), lambda i, j: (i, 0)), pl.BlockSpec((d, tn), lambda i, j: (0, c0 + j))],
        out_specs=pl.BlockSpec((tm, tn), lambda i, j: (i, j)),
        out_shape=jax.ShapeDtypeStruct((n, nc), BF16),
        compiler_params=_cparams(("parallel", "parallel")),
        name="proj_plain",
    )(h, w)


def _proj_rope_kernel(h_ref, w_ref, cos_ref, sin_ref, o_ref, *, half, head_major, out_scale):
    y = jnp.dot(h_ref[...], w_ref[...], preferred_element_type=F32)
    cos, sin = cos_ref[...], sin_ref[...]
    per_tile = LANES // (2 * half)
    for c in range(y.shape[1] // LANES):
        sl = slice(c * LANES, (c + 1) * LANES)
        roped = (_rope_tile(y[:, sl], cos, sin, half) * out_scale).astype(o_ref.dtype)
        if head_major:
            for k in range(per_tile):
                o_ref[c * per_tile + k] = roped[:, k * 2 * half:(k + 1) * 2 * half]
        else:
            o_ref[:, sl] = roped


def _proj_rope(h, w, col0, nc, cos, sin, half, tm, tn, head_major=False, out_scale=1.0):
    n, d = h.shape
    c0 = col0 // tn
    if head_major:
        assert tn == nc
        heads = nc // (2 * half)
        out_spec = pl.BlockSpec((heads, tm, 2 * half), lambda i, j: (0, i, 0))
        out_shape = jax.ShapeDtypeStruct((heads, n, 2 * half), BF16)
    else:
        out_spec = pl.BlockSpec((tm, tn), lambda i, j: (i, j))
        out_shape = jax.ShapeDtypeStruct((n, nc), BF16)
    return pl.pallas_call(
        functools.partial(_proj_rope_kernel, half=half, head_major=head_major, out_scale=out_scale),
        grid=(n // tm, nc // tn),
        in_specs=[pl.BlockSpec((tm, d), lambda i, j: (i, 0)), pl.BlockSpec((d, tn), lambda i, j: (0, c0 + j)),
                  pl.BlockSpec((tm, LANES), lambda i, j: (i, 0)), pl.BlockSpec((tm, LANES), lambda i, j: (i, 0))],
        out_specs=out_spec,
        out_shape=out_shape,
        compiler_params=_cparams(("parallel", "parallel")),
        name="proj_rope",
    )(h, w, cos, sin)


def _kv_kernel(h_ref, wc_ref, g_ref, wup_ref, cos_ref, sin_ref, ka_ref, va_ref):
    c = jnp.dot(h_ref[...], wc_ref[...], preferred_element_type=F32)
    ms = jnp.mean(c * c, axis=-1, keepdims=True)
    cn = (c * lax.rsqrt(ms + EPS) * g_ref[...]).astype(BF16)
    kv = jnp.dot(cn, wup_ref[...], preferred_element_type=F32)
    cos, sin = cos_ref[...], sin_ref[...]
    for hd in range(A_HEADS):
        sl = slice(hd * HEAD_DIM, (hd + 1) * HEAD_DIM)
        ka_ref[:, sl] = _rope_tile(kv[:, sl], cos, sin, HEAD_DIM // 2).astype(ka_ref.dtype)
    va_ref[...] = kv[:, A_WIDTH:].astype(va_ref.dtype)


def _proj_kv(h, wc, col0, g_kv, wup, cos, sin, tm):
    n, d = h.shape
    c0 = col0 // A_KV_RANK
    return pl.pallas_call(
        _kv_kernel,
        grid=(n // tm,),
        in_specs=[pl.BlockSpec((tm, d), lambda i: (i, 0)),
                  pl.BlockSpec((d, A_KV_RANK), lambda i: (0, c0)),
                  pl.BlockSpec((1, A_KV_RANK), lambda i: (0, 0)),
                  pl.BlockSpec((A_KV_RANK, 2 * A_WIDTH), lambda i: (0, 0)),
                  pl.BlockSpec((tm, LANES), lambda i: (i, 0)),
                  pl.BlockSpec((tm, LANES), lambda i: (i, 0))],
        out_specs=[pl.BlockSpec((tm, A_WIDTH), lambda i: (i, 0)), pl.BlockSpec((tm, A_WIDTH), lambda i: (i, 0))],
        out_shape=[jax.ShapeDtypeStruct((n, A_WIDTH), BF16), jax.ShapeDtypeStruct((n, A_WIDTH), BF16)],
        compiler_params=_cparams(("parallel",)),
        name="proj_kv",
    )(h, wc, g_kv.reshape(1, A_KV_RANK), wup, cos, sin)


def _small_kernel(h_ref, w_ref, g_ref, bf_ref, c0_ref, cos_ref, sin_ref, o_ref, carry_ref):
    tm = h_ref.shape[0]

    @pl.when(pl.program_id(0) == 0)
    def _():
        carry_ref[...] = c0_ref[...]

    y = jnp.dot(h_ref[...], w_ref[...], preferred_element_type=F32)
    lane = lax.broadcasted_iota(jnp.int32, y.shape, 1)
    is_ki = lane < SM_WI
    ms = jnp.sum(jnp.where(is_ki, y * y, 0.0), axis=-1, keepdims=True) * (1.0 / IDX_DIM)
    ki = _rope_tile(y * lax.rsqrt(ms + EPS) * g_ref[...], cos_ref[...], sin_ref[...], IDX_DIM // 2)
    wi = y * (IDX_HEADS ** -0.5 * IDX_DIM ** -0.5)
    z = y + bf_ref[...]
    logf = -(jnp.maximum(-z, 0.0) + jnp.log(1.0 + jnp.exp(-jnp.abs(z))))
    hi = logf.astype(BF16)
    r1 = logf - hi.astype(F32)
    mid = r1.astype(BF16)
    lo = (r1 - mid.astype(F32)).astype(BF16)
    row = lax.broadcasted_iota(jnp.int32, (tm, tm), 0)
    col = lax.broadcasted_iota(jnp.int32, (tm, tm), 1)
    tri = (col <= row).astype(BF16)
    csum = (jnp.dot(tri, hi, preferred_element_type=F32) + jnp.dot(tri, mid, preferred_element_type=F32)
            + jnp.dot(tri, lo, preferred_element_type=F32)) + carry_ref[0:1, :]
    carry_ref[...] = jnp.broadcast_to(csum[tm - 1:tm, :], carry_ref.shape)
    o_ref[...] = jnp.where(is_ki, ki, jnp.where(lane < SM_FB, wi, csum))


def _proj_small(h, w, col0, g_idx_pad, bf_pad, c0, cos, sin, tm):
    n, d = h.shape
    cb = col0 // LANES
    return pl.pallas_call(
        _small_kernel,
        grid=(n // tm,),
        in_specs=[pl.BlockSpec((tm, d), lambda i: (i, 0)),
                  pl.BlockSpec((d, LANES), lambda i: (0, cb)),
                  pl.BlockSpec((1, LANES), lambda i: (0, 0)),
                  pl.BlockSpec((1, LANES), lambda i: (0, 0)),
                  pl.BlockSpec((8, LANES), lambda i: (0, 0)),
                  pl.BlockSpec((tm, LANES), lambda i: (i, 0)),
                  pl.BlockSpec((tm, LANES), lambda i: (i, 0))],
        out_specs=pl.BlockSpec((tm, LANES), lambda i: (i, 0)),
        out_shape=jax.ShapeDtypeStruct((n, LANES), F32),
        scratch_shapes=[pltpu.VMEM((8, LANES), F32)],
        compiler_params=_cparams(("arbitrary",)),
        name="proj_small",
    )(h, w, g_idx_pad, bf_pad, c0, cos, sin)


COUNT_ROWS = 64
COUNT_UNROLL = 4


def _key_to_float(key):
    bits = key ^ ((key >> 31) & jnp.int32(0x7FFFFFFF))
    return lax.bitcast_convert_type(bits, F32)


def _select_kernel(qi_ref, kir_ref, kim_ref, wt_ref, or_ref, om_ref, s_ref, *, tq, topk):
    i = pl.program_id(0)
    n_real = or_ref.shape[1]
    n_chunks = n_real // tq
    krow = lax.broadcasted_iota(jnp.int32, (tq, tq), 0)
    qcol = lax.broadcasted_iota(jnp.int32, (tq, tq), 1)
    mrow = lax.broadcasted_iota(jnp.int32, (META_PAD, tq), 0)
    wt = wt_ref[...]

    def scores(kc):
        s = jnp.zeros((kc.shape[0], tq), F32)
        for h in range(IDX_HEADS):
            d = lax.dot_general(kc, qi_ref[h], NT_DIMS, preferred_element_type=F32)
            s = s + wt[h:h + 1, :] * jnp.maximum(d, 0.0)
        return s

    def causal(c):
        return krow + (c - i) * tq <= qcol

    def score_chunk(c, _):
        off = pl.multiple_of(c * tq, tq)
        s_ref[pl.ds(off, tq), :] = jnp.where(causal(c), scores(kir_ref[pl.ds(off, tq), :]), NEG)
        return 0

    lax.fori_loop(0, i + 1, score_chunk, 0)
    s_ref[n_real:, :] = jnp.where(mrow < N_META, scores(kim_ref[...]), NEG)

    def count(cmp, thr):
        def body(c, acc):
            off = c * tq
            for r in range(tq // COUNT_ROWS):
                blk = s_ref[pl.ds(pl.multiple_of(off + r * COUNT_ROWS, COUNT_ROWS), COUNT_ROWS), :]
                acc = acc + jnp.where(cmp(blk, thr), 1.0, 0.0)
            return acc

        groups = (i + 1) // COUNT_UNROLL

        def body_group(g, acc):
            for u in range(COUNT_UNROLL):
                acc = body(g * COUNT_UNROLL + u, acc)
            return acc

        acc = lax.fori_loop(0, groups, body_group, jnp.zeros((COUNT_ROWS, tq), F32))
        acc = lax.fori_loop(groups * COUNT_UNROLL, i + 1, body, acc)
        for r in range(META_PAD // COUNT_ROWS):
            blk = s_ref[n_real + r * COUNT_ROWS:n_real + (r + 1) * COUNT_ROWS, :]
            acc = acc + jnp.where(cmp(blk, thr), 1.0, 0.0)
        return jnp.sum(acc, axis=0, keepdims=True)

    kf = jnp.float32(topk)

    def bit_step(b, carry):
        t_key, cnt_t = carry
        cand = t_key + lax.shift_left(jnp.int32(1), 31 - b)
        cnt = count(lambda s, t: s >= t, _key_to_float(cand))
        take = cnt >= kf
        return jnp.where(take, cand, t_key), jnp.where(take, cnt, cnt_t)

    t0 = jnp.full((1, tq), jnp.iinfo(jnp.int32).min, jnp.int32)
    c0 = jnp.broadcast_to(((i + 1) * tq + META_PAD).astype(F32), (1, tq))
    t_key, cnt_t = lax.fori_loop(0, 32, bit_step, (t0, c0))
    thr = _key_to_float(t_key)

    def fill_chunk(c, _):
        or_ref[:, pl.ds(pl.multiple_of(c * tq, tq), tq)] = jnp.full((tq, tq), NEG, or_ref.dtype)
        return 0

    lax.fori_loop(i + 1, n_chunks, fill_chunk, 0)

    ambiguous = jnp.logical_and(cnt_t > kf, thr > NEG)
    any_amb = jnp.max(jnp.where(ambiguous, 1.0, 0.0)) > 0.0

    @pl.when(jnp.logical_not(any_amb))
    def _():
        def out_chunk(c, _):
            off = pl.multiple_of(c * tq, tq)
            sel = jnp.where(s_ref[pl.ds(off, tq), :] >= thr, 0.0, NEG)
            or_ref[:, pl.ds(off, tq)] = jnp.where(causal(c), sel, NEG).T.astype(or_ref.dtype)
            return 0

        lax.fori_loop(0, i + 1, out_chunk, 0)
        sel = jnp.where(s_ref[n_real:, :] >= thr, 0.0, NEG)
        om_ref[...] = jnp.where(mrow < N_META, sel, NEG).T.astype(om_ref.dtype)

    @pl.when(any_amb)
    def _():
        cnt_gt = count(lambda s, t: s > t, thr)
        quota = jnp.where(ambiguous, kf - cnt_gt, jnp.float32(3e38))

        def tie_select(s, valid, running):
            rows = s.shape[0]
            r = lax.broadcasted_iota(jnp.int32, (rows, rows), 0)
            c_ = lax.broadcasted_iota(jnp.int32, (rows, rows), 1)
            lower = (c_ <= r).astype(BF16)
            eq = jnp.where(s == thr, 1.0, 0.0)
            rank = jnp.dot(lower, eq.astype(BF16), preferred_element_type=F32) + running
            keep = jnp.where(s > thr, 0.0, jnp.where(jnp.logical_and(eq > 0.0, rank <= quota), 0.0, NEG))
            return jnp.where(valid, keep, NEG), running + jnp.sum(eq, axis=0, keepdims=True)

        bias_m, running = tie_select(s_ref[n_real:, :], mrow < N_META, jnp.zeros((1, tq), F32))
        om_ref[...] = bias_m.T.astype(om_ref.dtype)

        def out_chunk(c, running):
            off = pl.multiple_of(c * tq, tq)
            bias, running = tie_select(s_ref[pl.ds(off, tq), :], causal(c), running)
            or_ref[:, pl.ds(off, tq)] = bias.T.astype(or_ref.dtype)
            return running

        lax.fori_loop(0, i + 1, out_chunk, running)


def _select_bias(qi_hm, ki_real, ki_meta, wt, topk, tq):
    _, t, _ = qi_hm.shape
    return pl.pallas_call(
        functools.partial(_select_kernel, tq=tq, topk=topk),
        grid=(t // tq,),
        in_specs=[pl.BlockSpec((IDX_HEADS, tq, IDX_DIM), lambda i: (0, i, 0)),
                  pl.BlockSpec((t, IDX_DIM), lambda i: (0, 0)),
                  pl.BlockSpec((META_PAD, IDX_DIM), lambda i: (0, 0)),
                  pl.BlockSpec((IDX_HEADS, tq), lambda i: (0, i))],
        out_specs=[pl.BlockSpec((tq, t), lambda i: (i, 0)), pl.BlockSpec((tq, META_PAD), lambda i: (i, 0))],
        out_shape=[jax.ShapeDtypeStruct((t, t), BF16), jax.ShapeDtypeStruct((t, META_PAD), BF16)],
        scratch_shapes=[pltpu.VMEM((t + META_PAD, tq), F32)],
        compiler_params=_cparams(("parallel",)),
        name="select_bias",
    )(qi_hm, ki_real, ki_meta, wt)


def _attn_kernel(*refs, t, fox):
    if fox:
        (qi_ref, kj_ref, q_ref, k_ref, v_ref, km_ref, vm_ref, cq_ref, ck_ref, ckm_ref,
         o_ref, m_ref, l_ref, acc_ref, r_ref) = refs
        s_ref = None
    else:
        (qi_ref, kj_ref, q_ref, k_ref, v_ref, km_ref, vm_ref, b_ref, bm_ref,
         o_ref, m_ref, l_ref, acc_ref, s_ref) = refs
    stage_logits = s_ref is not None
    step = pl.program_id(0)
    i = qi_ref[step]
    j = kj_ref[step]
    nheads = q_ref.shape[1] // HEAD_DIM

    def block(k_blk_ref, v_blk_ref, bias_fn):
        width = k_blk_ref.shape[0]

        def logits(h):
            sl = slice(h * HEAD_DIM, (h + 1) * HEAD_DIM)
            return lax.dot_general(q_ref[:, sl], k_blk_ref[:, sl], NT_DIMS, preferred_element_type=F32)

        if stage_logits:
            for h in range(nheads):
                s_ref[h, :, :width] = logits(h)
        s_next = None if stage_logits else logits(0)
        for h in range(nheads):
            sl = slice(h * HEAD_DIM, (h + 1) * HEAD_DIM)
            if stage_logits:
                s = s_ref[h, :, :width]
            else:
                s, s_next = s_next, (logits(h + 1) if h + 1 < nheads else None)
            u = bias_fn(s, h)
            m_prev = m_ref[h]
            mx = jnp.max(u, axis=1, keepdims=True)
            if fox:
                r = r_ref[h]
                m_next = jnp.maximum(m_prev, mx + r)
                sub = m_next - r
            else:
                m_next = jnp.maximum(m_prev, mx)
                sub = m_next
            alpha = jnp.exp2(m_prev - m_next)
            ps = [jnp.exp2(u[:, c * LANES:(c + 1) * LANES] - sub) for c in range(width // LANES)]
            psum = ps[0]
            for pc in ps[1:]:
                psum = psum + pc
            p = ps[0] if len(ps) == 1 else jnp.concatenate(ps, axis=1)
            l_ref[h] = alpha * l_ref[h] + psum
            m_ref[h] = m_next
            acc_ref[h] = alpha * acc_ref[h] + jnp.dot(p.astype(BF16), v_blk_ref[:, sl],
                                                      preferred_element_type=F32)

    @pl.when(j == 0)
    def _():
        m_ref[...] = jnp.full(m_ref.shape, NEG, F32)
        l_ref[...] = jnp.zeros(l_ref.shape, F32)
        acc_ref[...] = jnp.zeros(acc_ref.shape, F32)
        mcol = lax.broadcasted_iota(jnp.int32, (t, META_PAD), 1)
        if fox:
            for h in range(nheads):
                r_ref[h] = jnp.broadcast_to(cq_ref[:, h:h + 1] * LOG2E, (t, LANES))

            def bias_fn(u, h):
                return jnp.where(mcol < N_META, u - ckm_ref[h:h + 1, :] * LOG2E, NEG)
        else:
            bias_m = bm_ref[...].astype(F32)

            def bias_fn(u, h):
                return u + bias_m
        block(km_ref, vm_ref, bias_fn)

    if fox:
        @pl.when(j < i)
        def _():
            block(k_ref, v_ref, lambda u, h: u - ck_ref[h:h + 1, :] * LOG2E)

        @pl.when(j == i)
        def _():
            row = lax.broadcasted_iota(jnp.int32, (t, t), 0)
            col = lax.broadcasted_iota(jnp.int32, (t, t), 1)
            block(k_ref, v_ref, lambda u, h: jnp.where(col <= row, u - ck_ref[h:h + 1, :] * LOG2E, NEG))
    else:
        bias = b_ref[...].astype(F32)
        block(k_ref, v_ref, lambda u, h: u + bias)

    @pl.when(j == i)
    def _():
        for h in range(nheads):
            sl = slice(h * HEAD_DIM, (h + 1) * HEAD_DIM)
            o_ref[:, sl] = (acc_ref[h] / jnp.sum(l_ref[h], axis=1, keepdims=True)).astype(o_ref.dtype)


def _attention(qkv, meta_kv, extra, *, fox, t_blk):
    q_arr, q_col, k_arr, k_col, v_arr, v_col = qkv
    km_arr, km_col, vm_arr, vm_col = meta_kv
    t = q_arr.shape[0]
    w = A_WIDTH
    nheads = w // HEAD_DIM
    nq = t // t_blk
    qi = jnp.asarray([i for i in range(nq) for _ in range(i + 1)], jnp.int32)
    kj = jnp.asarray([j for i in range(nq) for j in range(i + 1)], jnp.int32)

    in_specs = [pl.BlockSpec((t_blk, w), lambda s, qi, kj: (qi[s], q_col)),
                pl.BlockSpec((t_blk, w), lambda s, qi, kj: (kj[s], k_col)),
                pl.BlockSpec((t_blk, w), lambda s, qi, kj: (kj[s], v_col)),
                pl.BlockSpec((META_PAD, w), lambda s, qi, kj: (0, km_col)),
                pl.BlockSpec((META_PAD, w), lambda s, qi, kj: (0, vm_col))]
    scratch = [pltpu.VMEM((nheads, t_blk, LANES), F32), pltpu.VMEM((nheads, t_blk, LANES), F32),
               pltpu.VMEM((nheads, t_blk, HEAD_DIM), F32)]
    args = (q_arr, k_arr, v_arr, km_arr, vm_arr)
    if fox:
        cq, ct_real, ct_meta = extra
        in_specs += [pl.BlockSpec((t_blk, nheads), lambda s, qi, kj: (qi[s], 0)),
                     pl.BlockSpec((nheads, t_blk), lambda s, qi, kj: (0, kj[s])),
                     pl.BlockSpec((nheads, META_PAD), lambda s, qi, kj: (0, 0))]
        args += (cq, ct_real, ct_meta)
        scratch.append(pltpu.VMEM((nheads, t_blk, LANES), F32))
    else:
        bias_real, bias_meta = extra
        in_specs += [pl.BlockSpec((t_blk, t_blk), lambda s, qi, kj: (qi[s], kj[s])),
                     pl.BlockSpec((t_blk, META_PAD), lambda s, qi, kj: (qi[s], 0))]
        args += (bias_real, bias_meta)
        scratch.append(pltpu.VMEM((nheads, t_blk, t_blk), F32))
    return pl.pallas_call(
        functools.partial(_attn_kernel, t=t_blk, fox=fox),
        grid_spec=pltpu.PrefetchScalarGridSpec(
            num_scalar_prefetch=2,
            grid=(int(qi.shape[0]),),
            in_specs=in_specs,
            out_specs=pl.BlockSpec((t_blk, w), lambda s, qi, kj: (qi[s], 0)),
            scratch_shapes=scratch),
        out_shape=jax.ShapeDtypeStruct((t, w), BF16),
        compiler_params=_cparams(("arbitrary",)),
        name="attn_fox" if fox else "attn_dsa",
    )(qi, kj, *args)


def _route(logits):
    lane = lax.broadcasted_iota(jnp.int32, logits.shape, 1).astype(F32)
    big = jnp.float32(LANES)
    ninf = -jnp.inf
    gl = jnp.where(lane < N_GROUPS, logits, ninf)
    gmax = jnp.max(gl, axis=1, keepdims=True)
    g_sel = jnp.min(jnp.where(gl == gmax, lane, big), axis=1, keepdims=True)
    p_group = 1.0 / jnp.sum(jnp.exp(gl - gmax), axis=1, keepdims=True)
    e_lo = RT_E0 + g_sel * EXPERTS_PER_GROUP
    in_grp = jnp.logical_and(lane >= e_lo, lane < e_lo + EXPERTS_PER_GROUP)
    el = jnp.where(in_grp, logits, ninf)
    emax = jnp.max(el, axis=1, keepdims=True)
    ex = jnp.exp(el - emax)
    prob = ex / jnp.sum(ex, axis=1, keepdims=True)
    prob = jnp.where(in_grp, prob, -1.0)
    p1 = jnp.max(prob, axis=1, keepdims=True)
    i1 = jnp.min(jnp.where(prob == p1, lane, big), axis=1, keepdims=True)
    rest = jnp.where(lane == i1, -1.0, prob)
    p2 = jnp.max(rest, axis=1, keepdims=True)
    i2 = jnp.min(jnp.where(rest == p2, lane, big), axis=1, keepdims=True)
    tot = p1 + p2
    w1 = p_group * (p1 / tot)
    w2 = p_group * (p2 / tot)
    return jnp.where(lane == i1, w1, jnp.where(lane == i2, w2, 0.0)), g_sel


def _merge_kernel(h_ref, ya_ref, yb_ref, wga_ref, wgb_ref, wa_ref, wb_ref, wo_ref, x_ref, g_ref,
                  wr_ref, br_ref, h1_ref, xr_ref, plan_ref, acc_ref, cnt_ref):
    n = pl.program_id(1)

    @pl.when(jnp.logical_and(pl.program_id(0) == 0, n == 0))
    def _():
        cnt_ref[...] = jnp.zeros(cnt_ref.shape, F32)

    @pl.when(n == 0)
    def _():
        acc_ref[...] = jnp.zeros(acc_ref.shape, F32)

    h = h_ref[...]
    ga = jnp.dot(h, wga_ref[...], preferred_element_type=F32)
    gb = jnp.dot(h, wgb_ref[...], preferred_element_type=F32)
    pa = jnp.dot(ya_ref[...], wa_ref[...], preferred_element_type=F32)
    pb = jnp.dot(yb_ref[...], wb_ref[...], preferred_element_type=F32)
    merged = jax.nn.sigmoid(ga) * pa + jax.nn.sigmoid(gb) * pb
    acc_ref[...] += jnp.dot(merged.astype(BF16), wo_ref[...], preferred_element_type=F32)

    @pl.when(n == pl.num_programs(1) - 1)
    def _():
        h1 = x_ref[...] + acc_ref[...]
        h1_ref[...] = h1
        ms = jnp.mean(h1 * h1, axis=-1, keepdims=True)
        t = h1 * lax.rsqrt(ms + EPS) * g_ref[...]
        t_hi = t.astype(BF16)
        t_lo = (t - t_hi.astype(F32)).astype(BF16)
        wr = wr_ref[...]
        y = jnp.dot(t_hi, wr, preferred_element_type=F32)
        logits = (y[:, :LANES] + y[:, LANES:]
                  + jnp.dot(t_lo, wr[:, :LANES], preferred_element_type=F32)) + br_ref[...]
        gates, g_sel = _route(logits)
        d = t.shape[1]
        xr_ref[:, :d] = t
        xr_ref[:, d:] = gates
        tm = t.shape[0]
        lane = lax.broadcasted_iota(jnp.int32, gates.shape, 1).astype(F32)
        onehot = jnp.where(lane == g_sel, 1.0, 0.0).astype(BF16)
        rr = lax.broadcasted_iota(jnp.int32, (tm, tm), 0)
        cc = lax.broadcasted_iota(jnp.int32, (tm, tm), 1)
        cum = jnp.dot((cc <= rr).astype(BF16), onehot, preferred_element_type=F32) + cnt_ref[0:1, :]
        cnt_ref[...] = jnp.broadcast_to(cum[tm - 1:tm, :], cnt_ref.shape)
        plan_ref[...] = jnp.where(lane < N_GROUPS, cum, jnp.where(lane == N_GROUPS, g_sel, 0.0))


def _merge(h, ya, yb, w_all, ga_col0, gb_col0, wa, wb, wo, x, g_ffn, wr_split, br, tm, tn):
    t, d = h.shape
    row = lambda i, n: (i, 0)
    ga0, gb0 = ga_col0 // tn, gb_col0 // tn
    wga = wgb = w_all
    return pl.pallas_call(
        _merge_kernel,
        grid=(t // tm, d // tn),
        in_specs=[pl.BlockSpec((tm, d), row),
                  pl.BlockSpec((tm, A_WIDTH), row),
                  pl.BlockSpec((tm, B_WIDTH), row),
                  pl.BlockSpec((d, tn), lambda i, n: (0, ga0 + n)),
                  pl.BlockSpec((d, tn), lambda i, n: (0, gb0 + n)),
                  pl.BlockSpec((A_WIDTH, tn), lambda i, n: (0, n)),
                  pl.BlockSpec((B_WIDTH, tn), lambda i, n: (0, n)),
                  pl.BlockSpec((tn, d), lambda i, n: (n, 0)),
                  pl.BlockSpec((tm, d), row),
                  pl.BlockSpec((1, d), lambda i, n: (0, 0)),
                  pl.BlockSpec((d, 2 * LANES), lambda i, n: (0, 0)),
                  pl.BlockSpec((1, LANES), lambda i, n: (0, 0))],
        out_specs=[pl.BlockSpec((tm, d), row), pl.BlockSpec((tm, d + LANES), row),
                   pl.BlockSpec((tm, LANES), row)],
        out_shape=[jax.ShapeDtypeStruct((t, d), F32), jax.ShapeDtypeStruct((t, d + LANES), F32),
                   jax.ShapeDtypeStruct((t, LANES), F32)],
        scratch_shapes=[pltpu.VMEM((tm, d), F32), pltpu.VMEM((8, LANES), F32)],
        compiler_params=_cparams(("arbitrary", "arbitrary")),
        name="merge_out_router",
    )(h, ya, yb, wga, wgb, wa, wb, wo, x, g_ffn.reshape(1, d), wr_split, br)


DMA_LOOP_UNROLL = 8


def _row_copy(src_ref, src_row, dst_ref, dst_row, sem):
    return pltpu.make_async_copy(src_ref.at[pl.ds(src_row, 1), :], dst_ref.at[pl.ds(dst_row, 1), :], sem)


def _dispatch_kernel(pos_ref, x_ref, init_ref, xs_ref, sem):
    del init_ref
    i = pl.program_id(0)
    tm = x_ref.shape[0]

    def issue(p, _):
        for prio in range(2):
            r = 2 * p + prio
            _row_copy(x_ref, r, xs_ref, pos_ref[i * tm + r], sem).start(priority=prio)
        return 0

    lax.fori_loop(0, tm // 2, issue, 0, unroll=DMA_LOOP_UNROLL // 2)

    def drain(r, _):
        _row_copy(x_ref, r, xs_ref, pos_ref[i * tm + r], sem).wait()
        return 0

    lax.fori_loop(0, tm, drain, 0, unroll=DMA_LOOP_UNROLL)


def _dispatch(pos, xr, n_slots, tm):
    t, w = xr.shape
    return pl.pallas_call(
        _dispatch_kernel,
        grid_spec=pltpu.PrefetchScalarGridSpec(
            num_scalar_prefetch=1,
            grid=(t // tm,),
            in_specs=[pl.BlockSpec((tm, w), lambda i, pos: (i, 0)),
                      pl.BlockSpec(memory_space=pl.ANY)],
            out_specs=pl.BlockSpec(memory_space=pl.ANY),
            scratch_shapes=[pltpu.SemaphoreType.DMA]),
        out_shape=jax.ShapeDtypeStruct((n_slots, w), xr.dtype),
        input_output_aliases={2: 0},
        compiler_params=_cparams(("arbitrary",)),
        name="moe_dispatch",
    )(pos, xr, jnp.zeros((n_slots, w), xr.dtype))


def _group_ffn_kernel(tg_ref, tv_ref, x_ref, wg_ref, wu_ref, wd_ref, o_ref, acc_ref, xb_ref):
    j = pl.program_id(0)
    k = pl.program_id(1)
    d = o_ref.shape[1]

    @pl.when(k == 0)
    def _():
        acc_ref[...] = jnp.zeros(acc_ref.shape, F32)
        xb_ref[...] = x_ref[:, :d].astype(BF16)

    @pl.when(tv_ref[j] > 0)
    def _():
        gates = x_ref[:, d:]
        lane = lax.broadcasted_iota(jnp.int32, gates.shape, 1)
        e_lane = RT_E0 + tg_ref[j] * EXPERTS_PER_GROUP + k
        g_e = jnp.sum(jnp.where(lane == e_lane, gates, 0.0), axis=1, keepdims=True)
        xb = xb_ref[...]
        hg = jnp.dot(xb, wg_ref[0].astype(BF16), preferred_element_type=F32)
        hu = jnp.dot(xb, wu_ref[0].astype(BF16), preferred_element_type=F32)
        act = jax.nn.silu(hg) * hu * g_e
        acc_ref[...] += jnp.dot(act.astype(BF16), wd_ref[0].astype(BF16), preferred_element_type=F32)

    @pl.when(k == pl.num_programs(1) - 1)
    def _():
        o_ref[...] = acc_ref[...]


def _group_ffn(tile_group, tile_valid, xs, wg, wu, wd, ts):
    n_slots, w = xs.shape
    d = w - LANES
    f = wg.shape[2]
    expert = lambda j, k, tg, tv: (tg[j] * EXPERTS_PER_GROUP + k, 0, 0)
    return pl.pallas_call(
        _group_ffn_kernel,
        grid_spec=pltpu.PrefetchScalarGridSpec(
            num_scalar_prefetch=2,
            grid=(n_slots // ts, EXPERTS_PER_GROUP),
            in_specs=[pl.BlockSpec((ts, w), lambda j, k, tg, tv: (j, 0)),
                      pl.BlockSpec((1, d, f), expert),
                      pl.BlockSpec((1, d, f), expert),
                      pl.BlockSpec((1, f, d), expert)],
            out_specs=pl.BlockSpec((ts, d), lambda j, k, tg, tv: (j, 0)),
            scratch_shapes=[pltpu.VMEM((ts, d), F32), pltpu.VMEM((ts, d), BF16)]),
        out_shape=jax.ShapeDtypeStruct((n_slots, d), F32),
        compiler_params=_cparams(("arbitrary", "arbitrary")),
        name="moe_group_ffn",
    )(tile_group, tile_valid, xs, wg, wu, wd)


def _combine_kernel(pos_ref, ys_ref, h1_ref, gf_ref, o_ref, ybuf_ref, sem):
    i = pl.program_id(0)
    tm = h1_ref.shape[0]

    def issue(p, _):
        for prio in range(2):
            r = 2 * p + prio
            _row_copy(ys_ref, pos_ref[i * tm + r], ybuf_ref, r, sem).start(priority=prio)
        return 0

    lax.fori_loop(0, tm // 2, issue, 0, unroll=DMA_LOOP_UNROLL // 2)

    def drain(r, _):
        _row_copy(ys_ref, pos_ref[i * tm + r], ybuf_ref, r, sem).wait()
        return 0

    lax.fori_loop(0, tm, drain, 0, unroll=DMA_LOOP_UNROLL)
    h2 = h1_ref[...] + ybuf_ref[...]
    ms = jnp.mean(h2 * h2, axis=-1, keepdims=True)
    o_ref[...] = h2 * lax.rsqrt(ms + EPS) * gf_ref[...]


def _combine(pos, ys, h1, g_final, tm):
    t, d = h1.shape
    return pl.pallas_call(
        _combine_kernel,
        grid_spec=pltpu.PrefetchScalarGridSpec(
            num_scalar_prefetch=1,
            grid=(t // tm,),
            in_specs=[pl.BlockSpec(memory_space=pl.ANY),
                      pl.BlockSpec((tm, d), lambda i, pos: (i, 0)),
                      pl.BlockSpec((1, d), lambda i, pos: (0, 0))],
            out_specs=pl.BlockSpec((tm, d), lambda i, pos: (i, 0)),
            scratch_shapes=[pltpu.VMEM((tm, d), F32), pltpu.SemaphoreType.DMA]),
        out_shape=jax.ShapeDtypeStruct((t, d), F32),
        compiler_params=_cparams(("arbitrary",)),
        name="moe_combine_final",
    )(pos, ys, h1, g_final.reshape(1, d))


def _group_plan(plan, ts):
    t = plan.shape[0]
    g_sel = plan[:, N_GROUPS].astype(jnp.int32)
    cum_sel = jnp.take_along_axis(plan[:, :N_GROUPS], g_sel[:, None], axis=1)[:, 0]
    counts = plan[t - 1, :N_GROUPS].astype(jnp.int32)
    padded = (counts + ts - 1) // ts * ts
    ends = jnp.cumsum(padded)
    starts = ends - padded
    pos = starts[g_sel] + cum_sel.astype(jnp.int32) - 1
    n_tiles = (t + N_GROUPS * (ts - 1)) // ts
    tile_start = jnp.arange(n_tiles, dtype=jnp.int32) * ts
    tile_group = jnp.minimum(jnp.sum(tile_start[:, None] >= ends[None, :], axis=1), N_GROUPS - 1)
    tile_valid = (tile_start < ends[N_GROUPS - 1]).astype(jnp.int32)
    return pos, tile_group.astype(jnp.int32), tile_valid, n_tiles * ts


def _pick(n, pref):
    return pref if n % pref == 0 else n


def _pad_rows(a, rows):
    return jnp.pad(a, ((0, rows - a.shape[0]), (0, 0)))


def _layer(x, meta, g_mix, w_in, g_kv, w_kv_up, g_idx_k, b_f, w_branch_a, w_branch_b, w_out,
           g_ffn, w_group, b_group, w_expert, b_expert, w_gate_e, w_up_e, w_down_e, g_final):
    t, d = x.shape
    n_all = t + N_META
    topk = min(TOPK_MAX, n_all // 4)

    o = [0]
    for wdt in (A_WIDTH, A_KV_RANK, IDX_HEADS * IDX_DIM, IDX_DIM, IDX_HEADS, B_WIDTH, B_WIDTH, B_WIDTH,
                B_HEADS, D_MODEL, D_MODEL):
        o.append(o[-1] + wdt)
    w_in_t = w_in.T
    w_all = _pack_w_in(w_in_t, [(o[0], o[1]), (o[2], o[3]), (o[5], o[8]), (o[9], o[10]), (o[10], o[11]),
                                (o[1], o[2])])
    w_small = _pack_small(w_in_t, (o[3], o[5]), (o[8], o[9]))
    col_qa = 0
    col_qi = col_qa + A_WIDTH
    col_b = col_qi + IDX_HEADS * IDX_DIM
    col_ga = col_b + 3 * B_WIDTH
    col_gb = col_ga + D_MODEL
    col_ckv = col_gb + D_MODEL
    w_up_bf = w_kv_up.astype(BF16)
    g_idx_pad = jnp.zeros((1, LANES), F32).at[0, SM_KI:SM_KI + IDX_DIM].set(g_idx_k)
    bf_pad = jnp.zeros((1, LANES), F32).at[0, SM_FB:SM_FB + B_HEADS].set(b_f)

    pos_meta = jnp.arange(N_META)
    pos_real = jnp.arange(N_META, n_all)
    cos64_m, sin64_m = _rope_tables(pos_meta, HEAD_DIM // 2)
    cos64_r, sin64_r = _rope_tables(pos_real, HEAD_DIM // 2)
    cos32_m, sin32_m = _rope_tables(pos_meta, IDX_DIM // 2)
    cos32_r, sin32_r = _rope_tables(pos_real, IDX_DIM // 2)

    tm = _pick(t, 512)

    hm = _rmsnorm(meta, g_mix, N_META)
    ka_m, va_m = _proj_kv(hm, w_all, col_ckv, g_kv, w_up_bf, cos64_m, sin64_m, N_META)
    small_m = _proj_small(hm, w_small, 0, g_idx_pad, bf_pad, jnp.zeros((8, LANES), F32), cos32_m, sin32_m,
                          N_META)
    b_m = _proj_plain(hm, w_all, col_b, 3 * B_WIDTH, N_META, B_WIDTH)
    c_carry = jnp.broadcast_to(small_m[N_META - 1:N_META, :], (8, LANES))

    h = _rmsnorm(x, g_mix, tm)
    qk_scale = HEAD_DIM ** -0.5 * LOG2E
    tmw = _pick(t, 1024)
    qa = _proj_rope(h, w_all, col_qa, A_WIDTH, cos64_r, sin64_r, HEAD_DIM // 2, tmw, A_WIDTH, out_scale=qk_scale)
    ka_r, va_r = _proj_kv(h, w_all, col_ckv, g_kv, w_up_bf, cos64_r, sin64_r, tm)
    qi_hm = _proj_rope(h, w_all, col_qi, IDX_HEADS * IDX_DIM, cos32_r, sin32_r, IDX_DIM // 2, tmw,
                       IDX_HEADS * IDX_DIM, head_major=True)
    small_r = _proj_small(h, w_small, 0, g_idx_pad, bf_pad, c_carry, cos32_r, sin32_r, tm)
    b_r = _proj_plain(h, w_all, col_b, 3 * B_WIDTH, tmw, B_WIDTH, first_tile_scale=qk_scale)

    ki_real = small_r[:, SM_KI:SM_KI + IDX_DIM].astype(BF16)
    ki_meta = _pad_rows(small_m[:, SM_KI:SM_KI + IDX_DIM], META_PAD).astype(BF16)
    wt = small_r[:, SM_WI:SM_WI + IDX_HEADS].T
    cq = small_r[:, SM_FB:SM_FB + B_HEADS]
    ct_meta = _pad_rows(small_m[:, SM_FB:SM_FB + B_HEADS], META_PAD).T

    bias_real, bias_meta = _select_bias(qi_hm, ki_real, ki_meta, wt, topk, _pick(t, 256))
    ya = _attention((qa, 0, ka_r, 0, va_r, 0),
                    (_pad_rows(ka_m, META_PAD), 0, _pad_rows(va_m, META_PAD), 0),
                    (bias_real, bias_meta), fox=False, t_blk=_pick(t, 512))
    b_m_pad = _pad_rows(b_m, META_PAD)
    yb = _attention((b_r, 0, b_r, 1, b_r, 2), (b_m_pad, 1, b_m_pad, 2),
                    (cq, cq.T, ct_meta), fox=True, t_blk=_pick(t, 512))

    w_r = jnp.concatenate([w_group, w_expert, jnp.zeros((d, LANES - N_GROUPS - N_EXPERTS), F32)], axis=1)
    wr_hi = w_r.astype(BF16)
    wr_split = jnp.concatenate([wr_hi, (w_r - wr_hi.astype(F32)).astype(BF16)], axis=1)
    b_r_pad = jnp.concatenate([b_group, b_expert, jnp.zeros((LANES - N_GROUPS - N_EXPERTS,), F32)]).reshape(1, LANES)
    h1, xr, plan = _merge(h, ya, yb, w_all, col_ga, col_gb, w_branch_a.astype(BF16), w_branch_b.astype(BF16),
                          w_out.astype(BF16), x, g_ffn, wr_split, b_r_pad, tm, 256)

    ts = _pick(t, 512)
    pos, tile_group, tile_valid, n_slots = _group_plan(plan, ts)
    xs = _dispatch(pos, xr, n_slots, _pick(t, 256))
    ys = _group_ffn(tile_group, tile_valid, xs, w_gate_e, w_up_e, w_down_e.astype(BF16), ts)
    return _combine(pos, ys, h1, g_final, _pick(t, 256))


def kernel(x, meta_tokens, g_mix, w_in, g_kv, w_kv_up, g_idx_k, b_f, w_branch_a, w_branch_b, w_out, g_ffn,
           w_group, b_group, w_expert, b_expert, w_gate_e, w_up_e, w_down_e, g_final):
    assert x.shape[0] == 1 and g_mix.shape[0] == 1, "single batch, single layer"
    out = _layer(x[0], meta_tokens.astype(x.dtype), g_mix[0], w_in[0], g_kv[0], w_kv_up[0], g_idx_k[0], b_f[0],
                 w_branch_a[0], w_branch_b[0], w_out[0], g_ffn[0], w_group[0], b_group[0], w_expert[0],
                 b_expert[0], w_gate_e[0], w_up_e[0], w_down_e[0], g_final)
    return out[None]
```
